```python
import math
import jax, jax.numpy as jnp
from jax import lax
import numpy as np

D_MODEL = 1024
BATCH = 16
SEQ = 2048
DEPTH = 2

EPS = 1e-6
CHUNK = 64
CONV_WIDTH = 4
GDN_HEADS = 8
GDN_DK = 64
GDN_DV = 64
RET_HEADS = 8
RET_DK = 64
RET_DV = 64
ROPE_BASE = 10000.0
MLSTM_HEADS = 8
MLSTM_DK = 64
MLSTM_DV = 128
D_FF_DENSE = 2816
N_EXPERTS = 8
TOP_K = 2
D_FF_EXPERT = 3584
MOE_BLOCK = 512

GDN_QKV = GDN_HEADS * (2 * GDN_DK + GDN_DV)
GDN_Z = GDN_HEADS * GDN_DV
RET_QK = RET_HEADS * RET_DK
RET_V = RET_HEADS * RET_DV
EVEN_SPLITS = [GDN_QKV, GDN_HEADS, GDN_HEADS, GDN_Z, RET_QK, RET_QK, RET_V, RET_V]
EVEN_IN = sum(EVEN_SPLITS)
EVEN_MIX = GDN_HEADS * GDN_DV + RET_HEADS * RET_DV
ODD_SPLITS = [MLSTM_HEADS * MLSTM_DK, MLSTM_HEADS * MLSTM_DK, MLSTM_HEADS * MLSTM_DV,
              MLSTM_HEADS * MLSTM_DV, MLSTM_HEADS, MLSTM_HEADS]
ODD_IN = sum(ODD_SPLITS)
ODD_MIX = MLSTM_HEADS * MLSTM_DV

kernel_name = 'hybrid_gdn_retention_mlstm_moe'


def split_cols(t, sizes):
    idx = [int(s) for s in np.cumsum(sizes)[:-1]]
    return jnp.split(t, idx, axis=-1)


def rmsnorm(x, w):
    xf = x.astype(jnp.float32)
    y = xf * lax.rsqrt(jnp.mean(xf * xf, axis=-1, keepdims=True) + EPS)
    return (y * w.astype(jnp.float32)).astype(x.dtype)


def head_groupnorm(x, w):
    mu = jnp.mean(x, axis=-1, keepdims=True)
    xc = x - mu
    var = jnp.mean(xc * xc, axis=-1, keepdims=True)
    return xc * lax.rsqrt(var + EPS) * w.astype(jnp.float32)


def l2norm(x):
    return x * lax.rsqrt(jnp.sum(x * x, axis=-1, keepdims=True) + EPS)


def rotary(t, pos):
    half = t.shape[-1] // 2
    inv_freq = jnp.power(ROPE_BASE, -jnp.arange(half, dtype=jnp.float32) / half)
    ang = pos[:, None] * inv_freq[None, :]
    cos = jnp.cos(ang)[:, None, :]
    sin = jnp.sin(ang)[:, None, :]
    t1, t2 = t[..., :half], t[..., half:]
    return jnp.concatenate([t1 * cos - t2 * sin, t1 * sin + t2 * cos], axis=-1)


def causal_depthwise_conv(x, w):
    ch = x.shape[-1]
    return lax.conv_general_dilated(
        x, w[:, None, :].astype(x.dtype), window_strides=(1,),
        padding=[(CONV_WIDTH - 1, 0)], dimension_numbers=('NWC', 'WIO', 'NWC'),
        feature_group_count=ch)


def to_chunks(t):
    b, t_len, h = t.shape[:3]
    t = t.reshape((b, t_len // CHUNK, CHUNK, h) + t.shape[3:])
    t = jnp.moveaxis(t, 3, 2)
    return jnp.moveaxis(t, 1, 0)


def from_chunks(t):
    n, b, h, c, d = t.shape
    return jnp.transpose(t, (1, 0, 3, 2, 4)).reshape(b, n * c, h, d)


def swiglu(h, w1, w3, w2):
    return (jax.nn.silu(h @ w1) * (h @ w3)) @ w2


def gated_delta_rule(q, k, v, g, beta):
    b, _, h, dk = q.shape
    dv = v.shape[-1]
    q, k, v, g, beta = (to_chunks(t) for t in (q, k, v, g, beta))
    causal = jnp.tril(jnp.ones((CHUNK, CHUNK), bool))
    strict = jnp.tril(jnp.ones((CHUNK, CHUNK), bool), -1)
    gc = jnp.cumsum(g, axis=-1)
    decay = jnp.exp(jnp.where(causal, gc[..., :, None] - gc[..., None, :], -jnp.inf))
    kb = k * beta[..., None]
    a_mat = jnp.where(strict, jnp.einsum('nbhck,nbhsk->nbhcs', kb, k) * decay, 0.0) + jnp.eye(CHUNK, dtype=jnp.float32)
    u = lax.linalg.triangular_solve(a_mat, v * beta[..., None], left_side=True, lower=True, unit_diagonal=True)
    w = lax.linalg.triangular_solve(a_mat, kb * jnp.exp(gc)[..., None], left_side=True, lower=True, unit_diagonal=True)
    qk = jnp.einsum('nbhck,nbhsk->nbhcs', q, k) * decay
    q_dec = q * jnp.exp(gc)[..., None]
    k_dec = k * jnp.exp(gc[..., -1:] - gc)[..., None]
    chunk_decay = jnp.exp(gc[..., -1])

    def step(state, xs):
        u_n, w_n, qk_n, q_n, k_n, dec_n = xs
        v_new = u_n - jnp.einsum('bhck,bhkv->bhcv', w_n, state)
        o = jnp.einsum('bhck,bhkv->bhcv', q_n, state) + jnp.einsum('bhcs,bhsv->bhcv', qk_n, v_new)
        state = state * dec_n[..., None, None] + jnp.einsum('bhck,bhcv->bhkv', k_n, v_new)
        return state, o

    s0 = jnp.zeros((b, h, dk, dv), jnp.float32)
    _, o = lax.scan(step, s0, (u, w, qk, q_dec, k_dec, chunk_decay))
    return from_chunks(o)


def retention(q, k, v, log_gamma):
    b, _, h, dk = q.shape
    dv = v.shape[-1]
    q, k, v = (to_chunks(t) for t in (q, k, v))
    idx = jnp.arange(CHUNK, dtype=jnp.float32)
    causal = jnp.tril(jnp.ones((CHUNK, CHUNK), bool))
    lg = log_gamma[:, None]
    decay = jnp.exp(jnp.where(causal, (idx[:, None] - idx[None, :]) * log_gamma[:, None, None], -jnp.inf))
    intra = jnp.einsum('nbhcs,nbhsv->nbhcv', jnp.einsum('nbhck,nbhsk->nbhcs', q, k) * decay, v)
    q_in = q * jnp.exp((idx + 1.0) * lg)[:, :, None]
    k_st = k * jnp.exp((CHUNK - 1.0 - idx) * lg)[:, :, None]
    chunk_decay = jnp.exp(CHUNK * log_gamma)[:, None, None]

    def step(state, xs):
        intra_n, q_n, k_n, v_n = xs
        o = intra_n + jnp.einsum('bhck,bhkv->bhcv', q_n, state)
        state = state * chunk_decay + jnp.einsum('bhck,bhcv->bhkv', k_n, v_n)
        return state, o

    s0 = jnp.zeros((b, h, dk, dv), jnp.float32)
    _, o = lax.scan(step, s0, (intra, q_in, k_st, v))
    return from_chunks(o)


def mlstm_chunkwise(q, k, v, ig, logf):
    b, _, h, dk = q.shape
    dv = v.shape[-1]
    q, k, v, ig, logf = (to_chunks(t) for t in (q, k, v, ig, logf))
    causal = jnp.tril(jnp.ones((CHUNK, CHUNK), bool))
    b_cum = jnp.cumsum(logf, axis=-1)
    log_w = jnp.where(causal, b_cum[..., :, None] - b_cum[..., None, :] + ig[..., None, :], -jnp.inf)
    m_intra = jnp.max(log_w, axis=-1)
    qk = jnp.einsum('nbhck,nbhsk->nbhcs', q, k)
    b_end = b_cum[..., -1]
    log_w_end = b_end[..., None] - b_cum + ig
    m_end = jnp.max(log_w_end, axis=-1)

    def step(carry, xs):
        c_s, n_s, m_s = carry
        q_n, k_n, v_n, qk_n, lw_n, mi_n, bc_n, lwe_n, me_n, be_n = xs
        m_t = jnp.maximum(bc_n + m_s[..., None], mi_n)
        inter = jnp.exp(bc_n + m_s[..., None] - m_t)
        s = qk_n * jnp.exp(lw_n - m_t[..., None])
        num = inter[..., None] * jnp.einsum('bhck,bhkv->bhcv', q_n, c_s) + jnp.einsum('bhcs,bhsv->bhcv', s, v_n)
        den = inter * jnp.einsum('bhck,bhk->bhc', q_n, n_s) + jnp.sum(s, axis=-1)
        h_n = num / jnp.maximum(jnp.abs(den), jnp.exp(-m_t))[..., None]
        m_new = jnp.maximum(be_n + m_s, me_n)
        scale = jnp.exp(be_n + m_s - m_new)
        wk = k_n * jnp.exp(lwe_n - m_new[..., None])[..., None]
        c_new = scale[..., None, None] * c_s + jnp.einsum('bhck,bhcv->bhkv', wk, v_n)
        n_new = scale[..., None] * n_s + jnp.sum(wk, axis=-2)
        return (c_new, n_new, m_new), h_n

    carry0 = (jnp.zeros((b, h, dk, dv), jnp.float32), jnp.zeros((b, h, dk), jnp.float32),
              jnp.zeros((b, h), jnp.float32))
    _, o = lax.scan(step, carry0, (q, k, v, qk, log_w, m_intra, b_cum, log_w_end, m_end, b_end))
    return from_chunks(o)


def moe_swiglu(h, router, w1, w3, w2):
    b, t, d = h.shape
    n = b * t
    hf = h.reshape(n, d)
    logits = (hf @ router).astype(jnp.float32)
    top_logit, top_idx = lax.top_k(logits, TOP_K)
    gates = jax.nn.softmax(top_logit, axis=-1).astype(h.dtype)
    nk = n * TOP_K
    flat_e = top_idx.reshape(nk)
    flat_tok = jnp.repeat(jnp.arange(n, dtype=jnp.int32), TOP_K)
    flat_gate = gates.reshape(nk)
    order = jnp.argsort(flat_e)
    sorted_e = flat_e[order]
    counts = jnp.zeros((N_EXPERTS,), jnp.int32).at[flat_e].add(1)
    padded = (counts + MOE_BLOCK - 1) // MOE_BLOCK * MOE_BLOCK
    pad_end = jnp.cumsum(padded)
    pad_start = pad_end - padded
    sort_start = jnp.cumsum(counts) - counts
    dest = pad_start[sorted_e] + jnp.arange(nk, dtype=jnp.int32) - sort_start[sorted_e]
    n_blocks = -(-nk // MOE_BLOCK) + N_EXPERTS
    n_slots = n_blocks * MOE_BLOCK
    slot_tok = jnp.full((n_slots,), n, jnp.int32).at[dest].set(flat_tok[order])
    slot_gate = jnp.zeros((n_slots,), h.dtype).at[dest].set(flat_gate[order])
    block_expert = jnp.minimum(
        jnp.searchsorted(pad_end, jnp.arange(n_blocks, dtype=jnp.int32) * MOE_BLOCK, side='right'),
        N_EXPERTS - 1)
    h_pad = jnp.concatenate([hf, jnp.zeros((1, d), h.dtype)], axis=0)
    xb = h_pad[slot_tok].reshape(n_blocks, MOE_BLOCK, d)

    def expert_block(args):
        xs, e = args
        return swiglu(xs, w1[e], w3[e], w2[e])

    yb = lax.map(expert_block, (xb, block_expert)).reshape(n_slots, d)
    out = jnp.zeros((n + 1, d), h.dtype).at[slot_tok].add(yb * slot_gate[:, None])
    return out[:n].reshape(b, t, d)


def even_layer(x, mix_norm, w_in, conv_w, a_log, dt_bias, gdn_norm, ret_norm, w_out,
               ffn_norm, ffn_w1, ffn_w3, ffn_w2):
    b, t, _ = x.shape
    f32 = jnp.float32
    proj = rmsnorm(x, mix_norm) @ w_in
    qkv, beta_pre, alpha_pre, z, rq, rk, rv, rg = split_cols(proj, EVEN_SPLITS)
    qkv = jax.nn.silu(causal_depthwise_conv(qkv, conv_w)).astype(f32)
    gq, gk, gv = split_cols(qkv, [GDN_HEADS * GDN_DK, GDN_HEADS * GDN_DK, GDN_HEADS * GDN_DV])
    gq = l2norm(gq.reshape(b, t, GDN_HEADS, GDN_DK)) * (GDN_DK ** -0.5)
    gk = l2norm(gk.reshape(b, t, GDN_HEADS, GDN_DK))
    gv = gv.reshape(b, t, GDN_HEADS, GDN_DV)
    beta = jax.nn.sigmoid(beta_pre.astype(f32))
    g = -jnp.exp(a_log.astype(f32)) * jax.nn.softplus(alpha_pre.astype(f32) + dt_bias.astype(f32))
    o_gdn = gated_delta_rule(gq, gk, gv, g, beta)
    o_gdn = rmsnorm(o_gdn, gdn_norm) * jax.nn.silu(z.astype(f32).reshape(b, t, GDN_HEADS, GDN_DV))
    pos = jnp.arange(t, dtype=f32)
    rq = rotary(rq.astype(f32).reshape(b, t, RET_HEADS, RET_DK), pos)
    rk = rotary(rk.astype(f32).reshape(b, t, RET_HEADS, RET_DK), pos) * (RET_DK ** -0.5)
    rv = rv.astype(f32).reshape(b, t, RET_HEADS, RET_DV)
    log_gamma = jnp.log1p(-jnp.exp2(-5.0 - jnp.arange(RET_HEADS, dtype=f32)))
    o_ret = retention(rq, rk, rv, log_gamma)
    o_ret = head_groupnorm(o_ret, ret_norm.reshape(RET_HEADS, RET_DV)) * jax.nn.silu(
        rg.astype(f32).reshape(b, t, RET_HEADS, RET_DV))
    mix = jnp.concatenate([o_gdn.reshape(b, t, -1), o_ret.reshape(b, t, -1)], axis=-1).astype(x.dtype)
    x = x + mix @ w_out
    return x + swiglu(rmsnorm(x, ffn_norm), ffn_w1, ffn_w3, ffn_w2)


def odd_layer(x, mix_norm, w_in, gate_bias, mlstm_norm, w_out, ffn_norm, router, exp_w1, exp_w3, exp_w2):
    b, t, _ = x.shape
    f32 = jnp.float32
    proj = rmsnorm(x, mix_norm) @ w_in
    q, k, v, o_pre, i_pre, f_pre = split_cols(proj, ODD_SPLITS)
    q = q.astype(f32).reshape(b, t, MLSTM_HEADS, MLSTM_DK)
    k = k.astype(f32).reshape(b, t, MLSTM_HEADS, MLSTM_DK) * (MLSTM_DK ** -0.5)
    v = v.astype(f32).reshape(b, t, MLSTM_HEADS, MLSTM_DV)
    gb = gate_bias.astype(f32)
    ig = i_pre.astype(f32) + gb[:MLSTM_HEADS]
    logf = jax.nn.log_sigmoid(f_pre.astype(f32) + gb[MLSTM_HEADS:])
    h = mlstm_chunkwise(q, k, v, ig, logf)
    h = rmsnorm(h, mlstm_norm.reshape(MLSTM_HEADS, MLSTM_DV)) * jax.nn.sigmoid(
        o_pre.astype(f32).reshape(b, t, MLSTM_HEADS, MLSTM_DV))
    x = x + h.reshape(b, t, -1).astype(x.dtype) @ w_out
    return x + moe_swiglu(rmsnorm(x, ffn_norm), router, exp_w1, exp_w3, exp_w2)


def setup_inputs(seed: int = 0) -> dict:
    key = jax.random.key(seed)
    ks = jax.random.split(key, 32)
    f32 = jnp.float32

    def nrm(k, shape, scale):
        return jax.random.normal(k, shape, f32) * scale

    def gain(k, n):
        return 1.0 + 0.02 * jax.random.normal(k, (n,), f32)

    dt = jnp.exp(jax.random.uniform(ks[5], (GDN_HEADS,), f32, math.log(1e-3), math.log(1e-1)))
    gate_bias = jnp.concatenate([
        0.1 * jax.random.normal(ks[15], (MLSTM_HEADS,), f32),
        jnp.linspace(3.0, 6.0, MLSTM_HEADS, dtype=f32) + 0.1 * jax.random.normal(ks[16], (MLSTM_HEADS,), f32)])
    return {
        'x': nrm(ks[0], (BATCH, SEQ, D_MODEL), 1.0),
        'mix_norm_0': gain(ks[1], D_MODEL),
        'w_in_0': nrm(ks[2], (D_MODEL, EVEN_IN), D_MODEL ** -0.5),
        'conv_w_0': nrm(ks[3], (CONV_WIDTH, GDN_QKV), CONV_WIDTH ** -0.5),
        'a_log_0': jnp.log(jax.random.uniform(ks[4], (GDN_HEADS,), f32, 1.0, 16.0)),
        'dt_bias_0': dt + jnp.log(-jnp.expm1(-dt)),
        'gdn_norm_0': gain(ks[6], GDN_DV),
        'ret_norm_0': gain(ks[7], RET_HEADS * RET_DV),
        'w_out_0': nrm(ks[8], (EVEN_MIX, D_MODEL), EVEN_MIX ** -0.5),
        'ffn_norm_0': gain(ks[9], D_MODEL),
        'ffn_w1_0': nrm(ks[10], (D_MODEL, D_FF_DENSE), D_MODEL ** -0.5),
        'ffn_w3_0': nrm(ks[11], (D_MODEL, D_FF_DENSE), D_MODEL ** -0.5),
        'ffn_w2_0': nrm(ks[12], (D_FF_DENSE, D_MODEL), D_FF_DENSE ** -0.5),
        'mix_norm_1': gain(ks[13], D_MODEL),
        'w_in_1': nrm(ks[14], (D_MODEL, ODD_IN), D_MODEL ** -0.5),
        'gate_bias_1': gate_bias,
        'mlstm_norm_1': gain(ks[17], ODD_MIX),
        'w_out_1': nrm(ks[18], (ODD_MIX, D_MODEL), ODD_MIX ** -0.5),
        'ffn_norm_1': gain(ks[19], D_MODEL),
        'router_1': nrm(ks[20], (D_MODEL, N_EXPERTS), D_MODEL ** -0.5),
        'exp_w1_1': nrm(ks[21], (N_EXPERTS, D_MODEL, D_FF_EXPERT), D_MODEL ** -0.5),
        'exp_w3_1': nrm(ks[22], (N_EXPERTS, D_MODEL, D_FF_EXPERT), D_MODEL ** -0.5),
        'exp_w2_1': nrm(ks[23], (N_EXPERTS, D_FF_EXPERT, D_MODEL), D_FF_EXPERT ** -0.5),
        'final_norm': gain(ks[24], D_MODEL),
    }


def reference(x, mix_norm_0, w_in_0, conv_w_0, a_log_0, dt_bias_0, gdn_norm_0, ret_norm_0, w_out_0,
              ffn_norm_0, ffn_w1_0, ffn_w3_0, ffn_w2_0, mix_norm_1, w_in_1, gate_bias_1, mlstm_norm_1,
              w_out_1, ffn_norm_1, router_1, exp_w1_1, exp_w3_1, exp_w2_1, final_norm):
    even_params = [(mix_norm_0, w_in_0, conv_w_0, a_log_0, dt_bias_0, gdn_norm_0, ret_norm_0, w_out_0,
                    ffn_norm_0, ffn_w1_0, ffn_w3_0, ffn_w2_0)]
    odd_params = [(mix_norm_1, w_in_1, gate_bias_1, mlstm_norm_1, w_out_1, ffn_norm_1, router_1,
                   exp_w1_1, exp_w3_1, exp_w2_1)]
    for layer in range(DEPTH):
        if layer % 2 == 0:
            x = even_layer(x, *even_params[layer // 2])
        else:
            x = odd_layer(x, *odd_params[layer // 2])
    return rmsnorm(x, final_norm)
```

```python
import functools
import math

import numpy as np
import jax
import jax.numpy as jnp
from jax import lax
from jax.experimental import pallas as pl
from jax.experimental.pallas import tpu as pltpu

F32 = jnp.float32
BF16 = jnp.bfloat16
HIGHEST = lax.Precision.HIGHEST

EPS = 1e-6
CHUNK = 64
CONV_WIDTH = 4
N_HEADS = 8
HEAD_DK = 64
GDN_DV = 64
RET_DV = 64
MLSTM_DV = 128
ROPE_BASE = 10000.0
N_EXPERTS = 8
MOE_BLOCK = 512
GATE_LANES = 128
VMEM_LIMIT = 52 * 1024 * 1024


def _cparams(sem):
    return pltpu.CompilerParams(dimension_semantics=sem, vmem_limit_bytes=VMEM_LIMIT)


def _dot(a, b):
    return jnp.dot(a.astype(BF16), b.astype(BF16), preferred_element_type=F32)


def _dot_nt(a, b):
    return lax.dot_general(a.astype(BF16), b.astype(BF16), (((1,), (1,)), ((), ())),
                           preferred_element_type=F32)


def _dot_tn(a, b):
    return lax.dot_general(a.astype(BF16), b.astype(BF16), (((0,), (0,)), ((), ())),
                           preferred_element_type=F32)


def _dot_f32(a, b):
    return jnp.dot(a, b, precision=HIGHEST, preferred_element_type=F32)


def _rms(x, w):
    return x * lax.rsqrt(jnp.mean(x * x, axis=-1, keepdims=True) + EPS) * w


def _silu(x):
    return x * (1.0 / (1.0 + jnp.exp(-x)))


def _sigmoid(x):
    return 1.0 / (1.0 + jnp.exp(-x))


def _softplus(x):
    return jnp.maximum(x, 0.0) + jnp.log1p(jnp.exp(-jnp.abs(x)))


def _causal(n, strict=False):
    r = lax.broadcasted_iota(jnp.int32, (n, n), 0)
    c = lax.broadcasted_iota(jnp.int32, (n, n), 1)
    return (r > c) if strict else (r >= c)


def _norm_proj_kernel(x_ref, nw_ref, w_ref, main_ref, gate_ref, *, n_main, col_chunk):
    h = _rms(x_ref[...], nw_ref[...]).astype(BF16)
    for c0 in range(0, n_main, col_chunk):
        main_ref[:, c0:c0 + col_chunk] = jnp.dot(
            h, w_ref[:, c0:c0 + col_chunk], preferred_element_type=F32).astype(BF16)
    gate_ref[...] = jnp.dot(h, w_ref[:, n_main:], preferred_element_type=F32)


def _norm_proj(x2d, norm_w, w_cat, n_main, tm=512, col_chunk=512):
    n, d = x2d.shape
    tm = min(tm, n)
    n_cat = w_cat.shape[1]
    return pl.pallas_call(
        functools.partial(_norm_proj_kernel, n_main=n_main, col_chunk=col_chunk),
        grid=(n // tm,),
        in_specs=[pl.BlockSpec((tm, d), lambda i: (i, 0)),
                  pl.BlockSpec((1, d), lambda i: (0, 0)),
                  pl.BlockSpec((d, n_cat), lambda i: (0, 0))],
        out_specs=[pl.BlockSpec((tm, n_main), lambda i: (i, 0)),
                   pl.BlockSpec((tm, GATE_LANES), lambda i: (i, 0))],
        out_shape=[jax.ShapeDtypeStruct((n, n_main), BF16),
                   jax.ShapeDtypeStruct((n, GATE_LANES), F32)],
        compiler_params=_cparams(("parallel",)),
        name="norm_proj",
    )(x2d, norm_w.reshape(1, d), w_cat)


def _gdn_kernel(qkv_ref, z_ref, gate_ref, gate_t_ref, conv_ref, arow_ref, dtrow_ref, acol_ref, dtcol_ref,
                norm_ref, ltri_ref, utri_ref, lvl_ref, out_ref, state_ref, xbuf_ref):
    c_idx = pl.program_id(1)
    n_qk = N_HEADS * HEAD_DK

    @pl.when(c_idx == 0)
    def _():
        state_ref[...] = jnp.zeros_like(state_ref)
        xbuf_ref[0:8, :] = jnp.zeros((8, xbuf_ref.shape[1]), F32)

    xbuf_ref[8:8 + CHUNK, :] = qkv_ref[...].astype(F32)
    pre = None
    for j in range(CONV_WIDTH):
        term = xbuf_ref[pl.ds(8 - (CONV_WIDTH - 1) + j, CHUNK), :] * conv_ref[j:j + 1, :]
        pre = term if pre is None else pre + term
    xbuf_ref[0:8, :] = xbuf_ref[CHUNK:CHUNK + 8, :]
    act = _silu(pre)
    q_all, k_all, v_all = act[:, :n_qk], act[:, n_qk:2 * n_qk], act[:, 2 * n_qk:]

    gl = gate_ref[...]
    beta_c = _sigmoid(gl)
    g_c = -jnp.exp(arow_ref[...]) * _softplus(gl + dtrow_ref[...])
    gc_c = _dot_f32(ltri_ref[...], g_c)
    gt = gate_t_ref[...]
    g_r = -jnp.exp(acol_ref[...]) * _softplus(gt + dtcol_ref[...])
    gc_r = _dot_f32(g_r, utri_ref[...])

    causal = _causal(CHUNK)
    strict = _causal(CHUNK, strict=True)
    eye = (lax.broadcasted_iota(jnp.int32, (CHUNK, CHUNK), 0)
           == lax.broadcasted_iota(jnp.int32, (CHUNK, CHUNK), 1)).astype(F32)
    zf = z_ref[...].astype(F32)
    outs = []
    for h in range(N_HEADS):
        lo, hi = h * HEAD_DK, (h + 1) * HEAD_DK
        qh, kh, vh = q_all[:, lo:hi], k_all[:, lo:hi], v_all[:, lo:hi]
        qh = qh * lax.rsqrt(jnp.sum(qh * qh, axis=-1, keepdims=True) + EPS) * (HEAD_DK ** -0.5)
        kh = kh * lax.rsqrt(jnp.sum(kh * kh, axis=-1, keepdims=True) + EPS)
        bc = beta_c[:, h:h + 1]
        gcc = gc_c[:, 8 + h:9 + h]
        gcr = gc_r[8 + h:9 + h, :]
        decay = jnp.exp(jnp.where(causal, gcc - gcr, -jnp.inf))
        kb = kh * bc
        lmat = jnp.where(strict, _dot_nt(kb, kh) * decay, 0.0)
        tinv = eye - lmat * lvl_ref[0]
        for lvl in range(1, 6):
            tinv = tinv - _dot(tinv, _dot(lmat * lvl_ref[lvl], tinv))
        egc = jnp.exp(gcc)
        uw = _dot(tinv, jnp.concatenate([vh * bc, kb * egc], axis=1))
        u, w = uw[:, :GDN_DV], uw[:, GDN_DV:]
        qk = _dot_nt(qh, kh) * decay
        state = state_ref[h]
        v_new = u - _dot(w, state)
        o = _dot(qh * egc, state) + _dot(qk, v_new)
        g_last = gcc[CHUNK - 1:CHUNK, :]
        state_ref[h] = state * jnp.exp(g_last) + _dot_tn(kh * jnp.exp(g_last - gcc), v_new)
        outs.append(_rms(o, norm_ref[...]) * _silu(zf[:, lo:hi]))
    out_ref[...] = jnp.concatenate(outs, axis=1).astype(out_ref.dtype)


def _level_masks():
    i = np.arange(CHUNK)[:, None]
    j = np.arange(CHUNK)[None, :]
    masks = []
    for lvl in range(6):
        same_parent = (i >> (lvl + 1)) == (j >> (lvl + 1))
        masks.append((same_parent & (((i >> lvl) & 1) == 1) & (((j >> lvl) & 1) == 0)).astype(np.float32))
    return jnp.asarray(np.stack(masks))


def _tri_consts():
    i = np.arange(CHUNK)[:, None]
    j = np.arange(CHUNK)[None, :]
    return jnp.asarray((j <= i).astype(np.float32)), jnp.asarray((i <= j).astype(np.float32))


def _lane_row(vals, offset):
    return jnp.zeros((1, GATE_LANES), F32).at[0, offset:offset + vals.shape[0]].set(vals.astype(F32))


def _sub_col(vals, offset):
    return jnp.zeros((16, 1), F32).at[offset:offset + vals.shape[0], 0].set(vals.astype(F32))


def _gdn(proj, gates, gates_t, conv_w, a_log, dt_bias, gdn_norm, b, t):
    nc = t // CHUNK
    n_qkv = N_HEADS * (2 * HEAD_DK + GDN_DV)
    n_z = N_HEADS * GDN_DV
    proj3 = proj.reshape(b, t, proj.shape[-1])
    ltri, utri = _tri_consts()
    const2 = lambda shape: pl.BlockSpec(shape, lambda i, c: (0,) * len(shape))
    return pl.pallas_call(
        _gdn_kernel,
        grid=(b, nc),
        in_specs=[pl.BlockSpec((None, CHUNK, n_qkv), lambda i, c: (i, c, 0)),
                  pl.BlockSpec((None, CHUNK, n_z), lambda i, c: (i, c, n_qkv // n_z)),
                  pl.BlockSpec((None, CHUNK, GATE_LANES), lambda i, c: (i, c, 0)),
                  pl.BlockSpec((None, None, 16, CHUNK), lambda i, c: (i, c, 0, 0)),
                  const2((CONV_WIDTH, n_qkv)), const2((1, GATE_LANES)), const2((1, GATE_LANES)),
                  const2((16, 1)), const2((16, 1)), const2((1, GDN_DV)),
                  const2((CHUNK, CHUNK)), const2((CHUNK, CHUNK)), const2((6, CHUNK, CHUNK))],
        out_specs=pl.BlockSpec((None, CHUNK, n_z), lambda i, c: (i, c, 0)),
        out_shape=jax.ShapeDtypeStruct((b, t, n_z), BF16),
        scratch_shapes=[pltpu.VMEM((N_HEADS, HEAD_DK, GDN_DV), F32),
                        pltpu.VMEM((CHUNK + 8, n_qkv), F32)],
        compiler_params=_cparams(("parallel", "arbitrary")),
        name="gdn_mixer",
    )(proj3, proj3, gates.reshape(b, t, GATE_LANES), gates_t, conv_w.astype(F32),
      _lane_row(a_log, 8), _lane_row(dt_bias, 8), _sub_col(a_log, 8), _sub_col(dt_bias, 8),
      gdn_norm.reshape(1, GDN_DV).astype(F32), ltri, utri, _level_masks())


def _ret_kernel(q_ref, k_ref, v_ref, g_ref, cos_ref, sin_ref, decay_ref, qs_ref, ks_ref, cd_ref, norm_ref,
                out_ref, state_ref):
    @pl.when(pl.program_id(1) == 0)
    def _():
        state_ref[...] = jnp.zeros_like(state_ref)

    width = N_HEADS * HEAD_DK
    half = HEAD_DK // 2
    first_half = (lax.broadcasted_iota(jnp.int32, (CHUNK, width), 1) % HEAD_DK) < half

    def rot(t_in):
        swapped = jnp.where(first_half, pltpu.roll(t_in, width - half, 1), pltpu.roll(t_in, half, 1))
        return t_in * cos_ref[...] + swapped * sin_ref[...]

    q_all = rot(q_ref[...].astype(F32))
    k_all = rot(k_ref[...].astype(F32)) * (HEAD_DK ** -0.5)
    v_all = v_ref[...].astype(F32)
    gate = _silu(g_ref[...].astype(F32))
    q_in = q_all * qs_ref[...]
    k_st = k_all * ks_ref[...]
    outs = []
    for h in range(N_HEADS):
        lo, hi = h * HEAD_DK, (h + 1) * HEAD_DK
        qh, kh, vh = q_all[:, lo:hi], k_all[:, lo:hi], v_all[:, lo:hi]
        state = state_ref[h]
        o = _dot(_dot_nt(qh, kh) * decay_ref[h], vh) + _dot(q_in[:, lo:hi], state)
        state_ref[h] = state * cd_ref[h] + _dot_tn(k_st[:, lo:hi], vh)
        xc = o - jnp.mean(o, axis=-1, keepdims=True)
        var = jnp.mean(xc * xc, axis=-1, keepdims=True)
        outs.append(xc * lax.rsqrt(var + EPS))
    out_ref[...] = (jnp.concatenate(outs, axis=1) * norm_ref[...] * gate).astype(out_ref.dtype)


def _retention(proj, ret_norm, b, t):
    nc = t // CHUNK
    width = N_HEADS * HEAD_DK
    half = HEAD_DK // 2
    proj3 = proj.reshape(b, t, proj.shape[-1])
    pos = jnp.arange(t, dtype=F32)
    inv_freq = jnp.power(ROPE_BASE, -jnp.arange(half, dtype=F32) / half)
    ang = pos[:, None] * inv_freq[None, :]
    cos, sin = jnp.cos(ang), jnp.sin(ang)
    cos_t = jnp.tile(jnp.concatenate([cos, cos], axis=1), (1, N_HEADS))
    sin_t = jnp.tile(jnp.concatenate([-sin, sin], axis=1), (1, N_HEADS))
    log_gamma = jnp.log1p(-jnp.exp2(-5.0 - jnp.arange(N_HEADS, dtype=F32)))
    idx = jnp.arange(CHUNK, dtype=F32)
    causal = jnp.tril(jnp.ones((CHUNK, CHUNK), bool))
    decay = jnp.exp(jnp.where(causal, (idx[:, None] - idx[None, :]) * log_gamma[:, None, None], -jnp.inf))
    qs = jnp.repeat(jnp.exp((idx + 1.0) * log_gamma[:, None]).T, HEAD_DK, axis=1)
    ks = jnp.repeat(jnp.exp((CHUNK - 1.0 - idx) * log_gamma[:, None]).T, HEAD_DK, axis=1)
    cd = jnp.broadcast_to(jnp.exp(CHUNK * log_gamma)[:, None, None], (N_HEADS, 1, RET_DV))
    base = (N_HEADS * (2 * HEAD_DK + GDN_DV) + N_HEADS * GDN_DV) // width
    col = lambda k: pl.BlockSpec((None, CHUNK, width), lambda i, c: (i, c, base + k))
    const2 = lambda shape: pl.BlockSpec(shape, lambda i, c: (0,) * len(shape))
    return pl.pallas_call(
        _ret_kernel,
        grid=(b, nc),
        in_specs=[col(0), col(1), col(2), col(3),
                  pl.BlockSpec((CHUNK, width), lambda i, c: (c, 0)),
                  pl.BlockSpec((CHUNK, width), lambda i, c: (c, 0)),
                  const2((N_HEADS, CHUNK, CHUNK)), const2((CHUNK, width)), const2((CHUNK, width)),
                  const2((N_HEADS, 1, RET_DV)), const2((1, width))],
        out_specs=pl.BlockSpec((None, CHUNK, width), lambda i, c: (i, c, 0)),
        out_shape=jax.ShapeDtypeStruct((b, t, width), BF16),
        scratch_shapes=[pltpu.VMEM((N_HEADS, HEAD_DK, RET_DV), F32)],
        compiler_params=_cparams(("parallel", "arbitrary")),
        name="retention_mixer",
    )(proj3, proj3, proj3, proj3, cos_t, sin_t, decay, qs, ks, cd, ret_norm.reshape(1, width).astype(F32))


def _mlstm_kernel(q_ref, k_ref, v_ref, o_ref, gate_ref, gate_t_ref, brow_ref, bcol_ref, norm_ref,
                  ltri_ref, utri_ref, out_ref, state_ref, m_ref):
    @pl.when(pl.program_id(1) == 0)
    def _():
        state_ref[...] = jnp.zeros_like(state_ref)
        m_ref[...] = jnp.zeros_like(m_ref)

    q_all = q_ref[...].astype(F32)
    k_all = k_ref[...].astype(F32) * (HEAD_DK ** -0.5)
    v_all = v_ref[...].astype(F32)
    ogate = _sigmoid(o_ref[...].astype(F32))
    gl = gate_ref[...] + brow_ref[...]
    bcum_c = _dot_f32(ltri_ref[...], -_softplus(-gl))
    gt = gate_t_ref[...] + bcol_ref[...]
    bcum_r = _dot_f32(-_softplus(-gt), utri_ref[...])

    causal = _causal(CHUNK)
    ones_col = (lax.broadcasted_iota(jnp.int32, (CHUNK, MLSTM_DV), 1) == 0).astype(F32)
    outs = []
    for h in range(N_HEADS):
        lo, hi = h * HEAD_DK, (h + 1) * HEAD_DK
        qh, kh = q_all[:, lo:hi], k_all[:, lo:hi]
        vx = jnp.concatenate([v_all[:, h * MLSTM_DV:(h + 1) * MLSTM_DV], ones_col], axis=1)
        bc = bcum_c[:, 8 + h:9 + h]
        br = bcum_r[8 + h:9 + h, :]
        ig_c = gl[:, h:h + 1]
        ig_r = gt[h:h + 1, :]
        log_w = jnp.where(causal, bc - br + ig_r, -jnp.inf)
        m_intra = jnp.max(log_w, axis=-1, keepdims=True)
        b_end = bc[CHUNK - 1:CHUNK, :]
        lw_end = b_end - bc + ig_c
        m_end = jnp.max(lw_end, axis=0, keepdims=True)
        m_s = m_ref[h][:, 0:1]
        cx = state_ref[h]
        m_t = jnp.maximum(bc + m_s, m_intra)
        inter = jnp.exp(bc + m_s - m_t)
        s = _dot_nt(qh, kh) * jnp.exp(log_w - m_t)
        numx = inter * _dot(qh, cx) + _dot(s, vx)
        den = numx[:, MLSTM_DV:MLSTM_DV + 1]
        hh = numx[:, :MLSTM_DV] / jnp.maximum(jnp.abs(den), jnp.exp(-m_t))
        m_new = jnp.maximum(b_end + m_s, m_end)
        wk = kh * jnp.exp(lw_end - m_new)
        state_ref[h] = jnp.exp(b_end + m_s - m_new) * cx + _dot_tn(wk, vx)
        m_ref[h] = jnp.broadcast_to(m_new, (1, GATE_LANES))
        outs.append(_rms(hh, norm_ref[:, h * MLSTM_DV:(h + 1) * MLSTM_DV]))
    out_ref[...] = (jnp.concatenate(outs, axis=1) * ogate).astype(out_ref.dtype)


def _mlstm(proj, gates, gates_t, gate_bias, mlstm_norm, b, t):
    nc = t // CHUNK
    n_qk = N_HEADS * HEAD_DK
    n_v = N_HEADS * MLSTM_DV
    proj3 = proj.reshape(b, t, proj.shape[-1])
    ltri, utri = _tri_consts()
    const2 = lambda shape: pl.BlockSpec(shape, lambda i, c: (0,) * len(shape))
    return pl.pallas_call(
        _mlstm_kernel,
        grid=(b, nc),
        in_specs=[pl.BlockSpec((None, CHUNK, n_qk), lambda i, c: (i, c, 0)),
                  pl.BlockSpec((None, CHUNK, n_qk), lambda i, c: (i, c, 1)),
                  pl.BlockSpec((None, CHUNK, n_v), lambda i, c: (i, c, 2 * n_qk // n_v)),
                  pl.BlockSpec((None, CHUNK, n_v), lambda i, c: (i, c, 2 * n_qk // n_v + 1)),
                  pl.BlockSpec((None, CHUNK, GATE_LANES), lambda i, c: (i, c, 0)),
                  pl.BlockSpec((None, None, 16, CHUNK), lambda i, c: (i, c, 0, 0)),
                  const2((1, GATE_LANES)), const2((16, 1)), const2((1, n_v)),
                  const2((CHUNK, CHUNK)), const2((CHUNK, CHUNK))],
        out_specs=pl.BlockSpec((None, CHUNK, n_v), lambda i, c: (i, c, 0)),
        out_shape=jax.ShapeDtypeStruct((b, t, n_v), BF16),
        scratch_shapes=[pltpu.VMEM((N_HEADS, HEAD_DK, 2 * MLSTM_DV), F32),
                        pltpu.VMEM((N_HEADS, 1, GATE_LANES), F32)],
        compiler_params=_cparams(("parallel", "arbitrary")),
        name="mlstm_mixer",
    )(proj3, proj3, proj3, proj3, gates.reshape(b, t, GATE_LANES), gates_t,
      _lane_row(gate_bias, 0), _sub_col(gate_bias, 0), mlstm_norm.reshape(1, n_v).astype(F32), ltri, utri)


def _gates_transposed(gates, b, t):
    nc = t // CHUNK
    return jnp.transpose(gates[:, :16].reshape(b, nc, CHUNK, 16), (0, 1, 3, 2))


def _outproj_ffn_kernel(x_ref, ma_ref, mb_ref, wo_ref, nw_ref, w1_ref, w3_ref, w2_ref, out_ref,
                        x1_ref, h_ref, acc_ref):
    j = pl.program_id(1)
    n_a = ma_ref.shape[1]

    @pl.when(j == 0)
    def _():
        x1 = (x_ref[...] + jnp.dot(ma_ref[...], wo_ref[:n_a, :], preferred_element_type=F32)
              + jnp.dot(mb_ref[...], wo_ref[n_a:, :], preferred_element_type=F32))
        x1_ref[...] = x1
        h_ref[...] = _rms(x1, nw_ref[...]).astype(BF16)
        acc_ref[...] = jnp.zeros_like(acc_ref)

    h = h_ref[...]
    a = jnp.dot(h, w1_ref[...], preferred_element_type=F32)
    g = jnp.dot(h, w3_ref[...], preferred_element_type=F32)
    acc_ref[...] += jnp.dot((_silu(a) * g).astype(BF16), w2_ref[...], preferred_element_type=F32)

    @pl.when(j == pl.num_programs(1) - 1)
    def _():
        out_ref[...] = x1_ref[...] + acc_ref[...]


def _outproj_ffn(x2d, mix_a, mix_b, w_out, ffn_norm, w1, w3, w2, tm=512, n_ff_chunks=2):
    n, d = x2d.shape
    tm = min(tm, n)
    d_ff = w1.shape[1]
    tf = d_ff // n_ff_chunks
    n_a, n_b = mix_a.shape[1], mix_b.shape[1]
    return pl.pallas_call(
        _outproj_ffn_kernel,
        grid=(n // tm, n_ff_chunks),
        in_specs=[pl.BlockSpec((tm, d), lambda i, j: (i, 0)),
                  pl.BlockSpec((tm, n_a), lambda i, j: (i, 0)),
                  pl.BlockSpec((tm, n_b), lambda i, j: (i, 0)),
                  pl.BlockSpec((n_a + n_b, d), lambda i, j: (0, 0)),
                  pl.BlockSpec((1, d), lambda i, j: (0, 0)),
                  pl.BlockSpec((d, tf), lambda i, j: (0, j)),
                  pl.BlockSpec((d, tf), lambda i, j: (0, j)),
                  pl.BlockSpec((tf, d), lambda i, j: (j, 0))],
        out_specs=pl.BlockSpec((tm, d), lambda i, j: (i, 0)),
        out_shape=jax.ShapeDtypeStruct((n, d), F32),
        scratch_shapes=[pltpu.VMEM((tm, d), F32), pltpu.VMEM((tm, d), BF16), pltpu.VMEM((tm, d), F32)],
        compiler_params=_cparams(("parallel", "arbitrary")),
        name="outproj_ffn",
    )(x2d, mix_a, mix_b, w_out, ffn_norm.reshape(1, d), w1, w3, w2)


def _outproj_router_kernel(x_ref, m_ref, wo_ref, nw_ref, r_ref, x3_ref, hn_ref, idx_ref, gate_ref):
    x3 = x_ref[...] + jnp.dot(m_ref[...], wo_ref[...], preferred_element_type=F32)
    x3_ref[...] = x3
    hn = _rms(x3, nw_ref[...])
    hn_ref[...] = hn
    logits = _dot_f32(hn, r_ref[...])
    lane = lax.broadcasted_iota(jnp.int32, logits.shape, 1)
    logits = jnp.where(lane < N_EXPERTS, logits, -jnp.inf)
    m0 = jnp.max(logits, axis=-1, keepdims=True)
    i0 = jnp.min(jnp.where(logits == m0, lane, GATE_LANES), axis=-1, keepdims=True)
    rest = jnp.where(lane == i0, -jnp.inf, logits)
    m1 = jnp.max(rest, axis=-1, keepdims=True)
    i1 = jnp.min(jnp.where(rest == m1, lane, GATE_LANES), axis=-1, keepdims=True)
    e1 = jnp.exp(m1 - m0)
    inv = 1.0 / (1.0 + e1)
    idx_ref[...] = jnp.where(lane == 0, i0, jnp.where(lane == 1, i1, 0))
    gate_ref[...] = jnp.where(lane == 0, inv, jnp.where(lane == 1, e1 * inv, 0.0))


def _outproj_router(x2d, mix, w_out, ffn_norm, router, tm=512):
    n, d = x2d.shape
    tm = min(tm, n)
    n_m = mix.shape[1]
    router_pad = jnp.zeros((d, GATE_LANES), F32).at[:, :N_EXPERTS].set(router.astype(F32))
    row = lambda w: pl.BlockSpec((tm, w), lambda i: (i, 0))
    return pl.pallas_call(
        _outproj_router_kernel,
        grid=(n // tm,),
        in_specs=[row(d), row(n_m),
                  pl.BlockSpec((n_m, d), lambda i: (0, 0)),
                  pl.BlockSpec((1, d), lambda i: (0, 0)),
                  pl.BlockSpec((d, GATE_LANES), lambda i: (0, 0))],
        out_specs=[row(d), row(d), row(GATE_LANES), row(GATE_LANES)],
        out_shape=[jax.ShapeDtypeStruct((n, d), F32), jax.ShapeDtypeStruct((n, d), F32),
                   jax.ShapeDtypeStruct((n, GATE_LANES), jnp.int32),
                   jax.ShapeDtypeStruct((n, GATE_LANES), F32)],
        compiler_params=_cparams(("parallel",)),
        name="outproj_router",
    )(x2d, mix, w_out, ffn_norm.reshape(1, d), router_pad)


def _rank_kernel(idx_ref, tri_ref, rank_ref, count_ref, carry_ref):
    @pl.when(pl.program_id(0) == 0)
    def _():
        carry_ref[...] = jnp.zeros_like(carry_ref)

    idx = idx_ref[...]
    lane = lax.broadcasted_iota(jnp.int32, idx.shape, 1)
    e0, e1 = idx[:, 0:1], idx[:, 1:2]
    member = ((lane == e0) | (lane == e1)).astype(F32)
    before = _dot(tri_ref[...], member) + carry_ref[...]
    r0 = jnp.sum(jnp.where(lane == e0, before, 0.0), axis=-1, keepdims=True)
    r1 = jnp.sum(jnp.where(lane == e1, before, 0.0), axis=-1, keepdims=True)
    rank_ref[...] = jnp.where(lane == 0, r0, jnp.where(lane == 1, r1, 0.0)).astype(jnp.int32)
    carry_ref[...] += jnp.sum(member, axis=0, keepdims=True)
    count_ref[...] = carry_ref[...].astype(jnp.int32)


def _expert_ranks(ridx, tm=512):
    n = ridx.shape[0]
    tm = min(tm, n)
    i = np.arange(tm)
    tri = jnp.asarray((i[None, :] < i[:, None]).astype(np.float32)).astype(BF16)
    return pl.pallas_call(
        _rank_kernel,
        grid=(n // tm,),
        in_specs=[pl.BlockSpec((tm, GATE_LANES), lambda i: (i, 0)),
                  pl.BlockSpec((tm, tm), lambda i: (0, 0))],
        out_specs=[pl.BlockSpec((tm, GATE_LANES), lambda i: (i, 0)),
                   pl.BlockSpec((1, GATE_LANES), lambda i: (0, 0))],
        out_shape=[jax.ShapeDtypeStruct((n, GATE_LANES), jnp.int32),
                   jax.ShapeDtypeStruct((1, GATE_LANES), jnp.int32)],
        scratch_shapes=[pltpu.VMEM((1, GATE_LANES), F32)],
        compiler_params=_cparams(("arbitrary",)),
        name="expert_ranks",
    )(ridx, tri)


def _row_copy(src_ref, src_row, dst_ref, dst_row, sem):
    return pltpu.make_async_copy(src_ref.at[pl.ds(src_row, 1)], dst_ref.at[pl.ds(dst_row, 1)], sem)


def _dispatch_kernel(dest_ref, hn_ref, slots_in_ref, slots_ref, sem, *, tm, n_tok):
    del slots_in_ref
    base = pl.program_id(0) * tm

    def start(r, carry):
        _row_copy(hn_ref, r, slots_ref, dest_ref[base + r], sem).start()
        _row_copy(hn_ref, r, slots_ref, dest_ref[n_tok + base + r], sem).start()
        return carry

    def wait(r, carry):
        _row_copy(hn_ref, 0, slots_ref, 0, sem).wait()
        _row_copy(hn_ref, 0, slots_ref, 0, sem).wait()
        return carry

    lax.fori_loop(0, tm, start, 0)
    lax.fori_loop(0, tm, wait, 0)


def _dispatch(dest_flat, hn, n_slots, tm=256):
    n, d = hn.shape
    tm = min(tm, n)
    grid_spec = pltpu.PrefetchScalarGridSpec(
        num_scalar_prefetch=1,
        grid=(n // tm,),
        in_specs=[pl.BlockSpec((tm, d), lambda i, dest: (i, 0)),
                  pl.BlockSpec(memory_space=pl.ANY)],
        out_specs=pl.BlockSpec(memory_space=pl.ANY),
        scratch_shapes=[pltpu.SemaphoreType.DMA(())],
    )
    return pl.pallas_call(
        functools.partial(_dispatch_kernel, tm=tm, n_tok=n),
        grid_spec=grid_spec,
        out_shape=jax.ShapeDtypeStruct((n_slots, d), hn.dtype),
        input_output_aliases={2: 0},
        compiler_params=_cparams(("arbitrary",)),
        name="row_dispatch",
    )(dest_flat, hn, jnp.zeros((n_slots, d), hn.dtype))


def _moe_kernel(be_ref, nused_ref, x_ref, w1_ref, w3_ref, w2_ref, out_ref):
    i, j = pl.program_id(0), pl.program_id(1)

    @pl.when(i < nused_ref[0])
    def _():
        x = x_ref[...].astype(BF16)
        a = jnp.dot(x, w1_ref[...], preferred_element_type=F32)
        g = jnp.dot(x, w3_ref[...], preferred_element_type=F32)
        y = jnp.dot((_silu(a) * g).astype(BF16), w2_ref[...], preferred_element_type=F32)

        @pl.when(j == 0)
        def _():
            out_ref[...] = y

        @pl.when(j > 0)
        def _():
            out_ref[...] += y

    @pl.when(i >= nused_ref[0])
    def _():
        out_ref[...] = jnp.zeros_like(out_ref)


def _moe_ffn(block_expert, n_used, slots, w1, w3, w2, n_ff_chunks=2):
    n_slots, d = slots.shape
    n_blocks = n_slots // MOE_BLOCK
    d_ff = w1.shape[2]
    tf = d_ff // n_ff_chunks
    last_j = n_ff_chunks - 1

    def blk(i, nused):
        return jnp.minimum(i, nused[0] - 1)

    def chunk(i, j, nused):
        return jnp.where(i < nused[0], j, last_j)

    grid_spec = pltpu.PrefetchScalarGridSpec(
        num_scalar_prefetch=2,
        grid=(n_blocks, n_ff_chunks),
        in_specs=[pl.BlockSpec((MOE_BLOCK, d), lambda i, j, be, nu: (blk(i, nu), 0)),
                  pl.BlockSpec((None, d, tf), lambda i, j, be, nu: (be[blk(i, nu)], 0, chunk(i, j, nu))),
                  pl.BlockSpec((None, d, tf), lambda i, j, be, nu: (be[blk(i, nu)], 0, chunk(i, j, nu))),
                  pl.BlockSpec((None, tf, d), lambda i, j, be, nu: (be[blk(i, nu)], chunk(i, j, nu), 0))],
        out_specs=pl.BlockSpec((MOE_BLOCK, d), lambda i, j, be, nu: (i, 0)),
    )
    return pl.pallas_call(
        _moe_kernel,
        grid_spec=grid_spec,
        out_shape=jax.ShapeDtypeStruct((n_slots, d), F32),
        compiler_params=_cparams(("arbitrary", "arbitrary")),
        name="moe_ffn",
    )(block_expert, n_used, slots, w1, w3, w2)


def _combine_kernel(dest_ref, x_ref, gate_ref, nw_ref, y_ref, out_ref, ybuf_ref, sem, *, tm, n_tok):
    base = pl.program_id(0) * tm

    def start(r, carry):
        _row_copy(y_ref, dest_ref[base + r], ybuf_ref.at[0], r, sem).start()
        _row_copy(y_ref, dest_ref[n_tok + base + r], ybuf_ref.at[1], r, sem).start()
        return carry

    def wait(r, carry):
        _row_copy(y_ref, 0, ybuf_ref.at[0], 0, sem).wait()
        _row_copy(y_ref, 0, ybuf_ref.at[1], 0, sem).wait()
        return carry

    lax.fori_loop(0, tm, start, 0)
    lax.fori_loop(0, tm, wait, 0)
    gate = gate_ref[...]
    y = x_ref[...] + (ybuf_ref[0] * gate[:, 0:1] + ybuf_ref[1] * gate[:, 1:2])
    out_ref[...] = _rms(y, nw_ref[...])


def _combine(dest_flat, x2d, rgate, final_norm, yb, tm=256):
    n, d = x2d.shape
    tm = min(tm, n)
    grid_spec = pltpu.PrefetchScalarGridSpec(
        num_scalar_prefetch=1,
        grid=(n // tm,),
        in_specs=[pl.BlockSpec((tm, d), lambda i, dest: (i, 0)),
                  pl.BlockSpec((tm, GATE_LANES), lambda i, dest: (i, 0)),
                  pl.BlockSpec((1, d), lambda i, dest: (0, 0)),
                  pl.BlockSpec(memory_space=pl.ANY)],
        out_specs=pl.BlockSpec((tm, d), lambda i, dest: (i, 0)),
        scratch_shapes=[pltpu.VMEM((2, tm, d), F32), pltpu.SemaphoreType.DMA(())],
    )
    return pl.pallas_call(
        functools.partial(_combine_kernel, tm=tm, n_tok=n),
        grid_spec=grid_spec,
        out_shape=jax.ShapeDtypeStruct((n, d), F32),
        compiler_params=_cparams(("arbitrary",)),
        name="combine_norm",
    )(dest_flat, x2d, rgate, final_norm.reshape(1, d), yb)


def _split_in_proj(w_in, lo, hi):
    main = jnp.concatenate([w_in[:, :lo], w_in[:, hi:]], axis=1)
    gate = jnp.pad(w_in[:, lo:hi], ((0, 0), (0, GATE_LANES - (hi - lo))))
    return jnp.concatenate([main, gate], axis=1).astype(BF16), main.shape[1]


def kernel(x, mix_norm_0, w_in_0, conv_w_0, a_log_0, dt_bias_0, gdn_norm_0, ret_norm_0, w_out_0, ffn_norm_0,
           ffn_w1_0, ffn_w3_0, ffn_w2_0, mix_norm_1, w_in_1, gate_bias_1, mlstm_norm_1, w_out_1, ffn_norm_1,
           router_1, exp_w1_1, exp_w3_1, exp_w2_1, final_norm):
    b, t, d = x.shape
    n = b * t
    x2d = x.reshape(n, d)

    gdn_qkv = N_HEADS * (2 * HEAD_DK + GDN_DV)
    w0, n_main0 = _split_in_proj(w_in_0, gdn_qkv, gdn_qkv + 2 * N_HEADS)
    proj0, gates0 = _norm_proj(x2d, mix_norm_0, w0, n_main0)
    o_gdn = _gdn(proj0, gates0, _gates_transposed(gates0, b, t), conv_w_0, a_log_0, dt_bias_0, gdn_norm_0, b, t)
    o_ret = _retention(proj0, ret_norm_0, b, t)
    x2 = _outproj_ffn(x2d, o_gdn.reshape(n, -1), o_ret.reshape(n, -1), w_out_0.astype(BF16), ffn_norm_0,
                      ffn_w1_0.astype(BF16), ffn_w3_0.astype(BF16), ffn_w2_0.astype(BF16))

    n_main1 = w_in_1.shape[1] - 2 * N_HEADS
    w1, _ = _split_in_proj(w_in_1, n_main1, n_main1 + 2 * N_HEADS)
    proj1, gates1 = _norm_proj(x2, mix_norm_1, w1, n_main1)
    h_mix = _mlstm(proj1, gates1, _gates_transposed(gates1, b, t), gate_bias_1, mlstm_norm_1, b, t)
    x3, hn, ridx, rgate = _outproj_router(x2, h_mix.reshape(n, -1), w_out_1.astype(BF16), ffn_norm_1, router_1)

    rank, counts = _expert_ranks(ridx)
    counts = counts[0, :N_EXPERTS]
    padded = (counts + MOE_BLOCK - 1) // MOE_BLOCK * MOE_BLOCK
    pad_end = jnp.cumsum(padded)
    pad_start = pad_end - padded
    dest_flat = jnp.concatenate([pad_start[ridx[:, 0]] + rank[:, 0], pad_start[ridx[:, 1]] + rank[:, 1]])
    n_blocks = -(-(2 * n) // MOE_BLOCK) + N_EXPERTS
    block_row0 = jnp.arange(n_blocks, dtype=jnp.int32) * MOE_BLOCK
    block_expert = jnp.minimum(jnp.sum((pad_end[None, :] <= block_row0[:, None]).astype(jnp.int32), axis=1),
                               N_EXPERTS - 1)
    n_used = (pad_end[-1:] // MOE_BLOCK).astype(jnp.int32)

    slots = _dispatch(dest_flat, hn, n_blocks * MOE_BLOCK)
    yb = _moe_ffn(block_expert, n_used, slots, exp_w1_1.astype(BF16), exp_w3_1.astype(BF16),
                  exp_w2_1.astype(BF16))
    out = _combine(dest_flat, x3, rgate, final_norm, yb)
    return out.reshape(b, t, d)
```

```python
import functools
import math

import numpy as np
import jax
import jax.numpy as jnp
from jax import lax
from jax.experimental import pallas as pl
from jax.experimental.pallas import tpu as pltpu

F32 = jnp.float32
BF16 = jnp.bfloat16
HIGHEST = lax.Precision.HIGHEST

EPS = 1e-6
CHUNK = 64
CONV_WIDTH = 4
N_HEADS = 8
HEAD_DK = 64
GDN_DV = 64
RET_DV = 64
MLSTM_DV = 128
ROPE_BASE = 10000.0
N_EXPERTS = 8
MOE_BLOCK = 512
GATE_LANES = 128
VMEM_LIMIT = 52 * 1024 * 1024
DMA_UNROLL = 8


def _cparams(sem):
    return pltpu.CompilerParams(dimension_semantics=sem, vmem_limit_bytes=VMEM_LIMIT)


def _dot(a, b):
    return jnp.dot(a.astype(BF16), b.astype(BF16), preferred_element_type=F32)


def _dot_nt(a, b):
    return lax.dot_general(a.astype(BF16), b.astype(BF16), (((1,), (1,)), ((), ())),
                           preferred_element_type=F32)


def _dot_tn(a, b):
    return lax.dot_general(a.astype(BF16), b.astype(BF16), (((0,), (0,)), ((), ())),
                           preferred_element_type=F32)


def _dot_f32(a, b):
    return jnp.dot(a, b, precision=HIGHEST, preferred_element_type=F32)


def _rms(x, w):
    return x * lax.rsqrt(jnp.mean(x * x, axis=-1, keepdims=True) + EPS) * w


def _silu(x):
    return x * (1.0 / (1.0 + jnp.exp(-x)))


def _sigmoid(x):
    return 1.0 / (1.0 + jnp.exp(-x))


def _softplus(x):
    return jnp.maximum(x, 0.0) + jnp.log1p(jnp.exp(-jnp.abs(x)))


def _causal(n, strict=False):
    r = lax.broadcasted_iota(jnp.int32, (n, n), 0)
    c = lax.broadcasted_iota(jnp.int32, (n, n), 1)
    return (r > c) if strict else (r >= c)


def _norm_proj_kernel(x_ref, nw_ref, w_ref, main_ref, gate_ref, *, n_main, col_chunk):
    h = _rms(x_ref[...], nw_ref[...]).astype(BF16)
    for c0 in range(0, n_main, col_chunk):
        main_ref[:, c0:c0 + col_chunk] = jnp.dot(
            h, w_ref[:, c0:c0 + col_chunk], preferred_element_type=F32).astype(BF16)
    gate_ref[...] = jnp.dot(h, w_ref[:, n_main:], preferred_element_type=F32)


def _norm_proj(x2d, norm_w, w_cat, n_main, tm=512, col_chunk=512):
    n, d = x2d.shape
    tm = min(tm, n)
    n_cat = w_cat.shape[1]
    return pl.pallas_call(
        functools.partial(_norm_proj_kernel, n_main=n_main, col_chunk=col_chunk),
        grid=(n // tm,),
        in_specs=[pl.BlockSpec((tm, d), lambda i: (i, 0)),
                  pl.BlockSpec((1, d), lambda i: (0, 0)),
                  pl.BlockSpec((d, n_cat), lambda i: (0, 0))],
        out_specs=[pl.BlockSpec((tm, n_main), lambda i: (i, 0)),
                   pl.BlockSpec((tm, GATE_LANES), lambda i: (i, 0))],
        out_shape=[jax.ShapeDtypeStruct((n, n_main), BF16),
                   jax.ShapeDtypeStruct((n, GATE_LANES), F32)],
        compiler_params=_cparams(("parallel",)),
        name="norm_proj",
    )(x2d, norm_w.reshape(1, d), w_cat)


def _gdn_kernel(qkv_ref, z_ref, gate_ref, gate_t_ref, conv_ref, arow_ref, dtrow_ref, acol_ref, dtcol_ref,
                norm_ref, ltri_ref, utri_ref, lvl_ref, ones_ref, out_ref, state_ref, xbuf_ref):
    n_b = qkv_ref.shape[0]
    n_qk = N_HEADS * HEAD_DK

    @pl.when(pl.program_id(1) == 0)
    def _():
        state_ref[...] = jnp.zeros_like(state_ref)
        xbuf_ref[:, 0:8, :] = jnp.zeros((n_b, 8, xbuf_ref.shape[2]), F32)

    causal = _causal(CHUNK)
    strict = _causal(CHUNK, strict=True)
    eye = (lax.broadcasted_iota(jnp.int32, (CHUNK, CHUNK), 0)
           == lax.broadcasted_iota(jnp.int32, (CHUNK, CHUNK), 1)).astype(F32)
    sl = [slice(h * HEAD_DK, (h + 1) * HEAD_DK) for h in range(N_HEADS)]

    q, k, v, bc, gcc, gcr = [], [], [], [], [], []
    for i in range(n_b):
        xbuf_ref[i, 8:8 + CHUNK, :] = qkv_ref[i].astype(F32)
        pre = None
        for j in range(CONV_WIDTH):
            term = xbuf_ref[i, pl.ds(8 - (CONV_WIDTH - 1) + j, CHUNK), :] * conv_ref[j:j + 1, :]
            pre = term if pre is None else pre + term
        xbuf_ref[i, 0:8, :] = xbuf_ref[i, CHUNK:CHUNK + 8, :]
        act = _silu(pre)
        q_all, k_all, v_all = act[:, :n_qk], act[:, n_qk:2 * n_qk], act[:, 2 * n_qk:]
        q_all = q_all * lax.rsqrt(_dot(q_all * q_all, ones_ref[...]) + EPS) * (HEAD_DK ** -0.5)
        k_all = k_all * lax.rsqrt(_dot(k_all * k_all, ones_ref[...]) + EPS)
        gl = gate_ref[i]
        beta_c = _sigmoid(gl)
        g_c = -jnp.exp(arow_ref[...]) * _softplus(gl + dtrow_ref[...])
        gc_c = _dot_f32(ltri_ref[...], g_c)
        g_r = -jnp.exp(acol_ref[...]) * _softplus(gate_t_ref[i] + dtcol_ref[...])
        gc_r = _dot_f32(g_r, utri_ref[...])
        for h in range(N_HEADS):
            q.append(q_all[:, sl[h]])
            k.append(k_all[:, sl[h]])
            v.append(v_all[:, sl[h]])
            bc.append(beta_c[:, h:h + 1])
            gcc.append(gc_c[:, 8 + h:9 + h])
            gcr.append(gc_r[8 + h:9 + h, :])

    chains = range(n_b * N_HEADS)
    decay = [jnp.exp(jnp.where(causal, gcc[c] - gcr[c], -jnp.inf)) for c in chains]
    kb = [k[c] * bc[c] for c in chains]
    scores = [_dot_nt(jnp.concatenate([kb[c], q[c]], axis=0), k[c]) for c in chains]
    lmat = [jnp.where(strict, scores[c][:CHUNK] * decay[c], 0.0) for c in chains]
    qk = [scores[c][CHUNK:] * decay[c] for c in chains]
    tinv = [eye - lmat[c] * lvl_ref[0] for c in chains]
    for lvl in range(1, 6):
        inner = [_dot(lmat[c] * lvl_ref[lvl], tinv[c]) for c in chains]
        tinv = [tinv[c] - _dot(tinv[c], inner[c]) for c in chains]
    egc = [jnp.exp(gcc[c]) for c in chains]
    uw = [_dot(tinv[c], jnp.concatenate([v[c] * bc[c], kb[c] * egc[c]], axis=1)) for c in chains]
    state = [state_ref[c] for c in chains]
    v_new = [uw[c][:, :GDN_DV] - _dot(uw[c][:, GDN_DV:], state[c]) for c in chains]
    o = [_dot(q[c] * egc[c], state[c]) + _dot(qk[c], v_new[c]) for c in chains]
    g_last = [gcc[c][CHUNK - 1:CHUNK, :] for c in chains]
    k_dec = [k[c] * jnp.exp(g_last[c] - gcc[c]) for c in chains]
    for c in chains:
        state_ref[c] = state[c] * jnp.exp(g_last[c]) + _dot_tn(k_dec[c], v_new[c])
    for i in range(n_b):
        o_all = jnp.concatenate(o[i * N_HEADS:(i + 1) * N_HEADS], axis=1)
        mean_sq = _dot(o_all * o_all, ones_ref[...]) * (1.0 / GDN_DV)
        out_ref[i] = (o_all * lax.rsqrt(mean_sq + EPS) * norm_ref[...]
                      * _silu(z_ref[i].astype(F32))).astype(out_ref.dtype)


def _level_masks():
    i = np.arange(CHUNK)[:, None]
    j = np.arange(CHUNK)[None, :]
    masks = []
    for lvl in range(6):
        same_parent = (i >> (lvl + 1)) == (j >> (lvl + 1))
        masks.append((same_parent & (((i >> lvl) & 1) == 1) & (((j >> lvl) & 1) == 0)).astype(np.float32))
    return jnp.asarray(np.stack(masks))


def _tri_consts():
    i = np.arange(CHUNK)[:, None]
    j = np.arange(CHUNK)[None, :]
    return jnp.asarray((j <= i).astype(np.float32)), jnp.asarray((i <= j).astype(np.float32))


def _lane_row(vals, offset):
    return jnp.zeros((1, GATE_LANES), F32).at[0, offset:offset + vals.shape[0]].set(vals.astype(F32))


def _sub_col(vals, offset):
    return jnp.zeros((16, 1), F32).at[offset:offset + vals.shape[0], 0].set(vals.astype(F32))


def _head_ones(n_heads, width):
    g = np.arange(n_heads * width) // width
    return jnp.asarray((g[:, None] == g[None, :]).astype(np.float32)).astype(BF16)


def _batch_per_step(b):
    return 2 if b % 2 == 0 else 1


def _gdn(proj, gates, gates_t, conv_w, a_log, dt_bias, gdn_norm, b, t):
    nc = t // CHUNK
    nb = _batch_per_step(b)
    n_qkv = N_HEADS * (2 * HEAD_DK + GDN_DV)
    n_z = N_HEADS * GDN_DV
    proj3 = proj.reshape(b, t, proj.shape[-1])
    ltri, utri = _tri_consts()
    const2 = lambda shape: pl.BlockSpec(shape, lambda i, c: (0,) * len(shape))
    return pl.pallas_call(
        _gdn_kernel,
        grid=(b // nb, nc),
        in_specs=[pl.BlockSpec((nb, CHUNK, n_qkv), lambda i, c: (i, c, 0)),
                  pl.BlockSpec((nb, CHUNK, n_z), lambda i, c: (i, c, n_qkv // n_z)),
                  pl.BlockSpec((nb, CHUNK, GATE_LANES), lambda i, c: (i, c, 0)),
                  pl.BlockSpec((nb, None, 16, CHUNK), lambda i, c: (i, c, 0, 0)),
                  const2((CONV_WIDTH, n_qkv)), const2((1, GATE_LANES)), const2((1, GATE_LANES)),
                  const2((16, 1)), const2((16, 1)), const2((1, n_z)),
                  const2((CHUNK, CHUNK)), const2((CHUNK, CHUNK)), const2((6, CHUNK, CHUNK)),
                  const2((n_z, n_z))],
        out_specs=pl.BlockSpec((nb, CHUNK, n_z), lambda i, c: (i, c, 0)),
        out_shape=jax.ShapeDtypeStruct((b, t, n_z), BF16),
        scratch_shapes=[pltpu.VMEM((nb * N_HEADS, HEAD_DK, GDN_DV), F32),
                        pltpu.VMEM((nb, CHUNK + 8, n_qkv), F32)],
        compiler_params=_cparams(("parallel", "arbitrary")),
        name="gdn_mixer",
    )(proj3, proj3, gates.reshape(b, t, GATE_LANES), gates_t, conv_w.astype(F32),
      _lane_row(a_log, 8), _lane_row(dt_bias, 8), _sub_col(a_log, 8), _sub_col(dt_bias, 8),
      jnp.tile(gdn_norm.astype(F32), N_HEADS).reshape(1, n_z), ltri, utri, _level_masks(),
      _head_ones(N_HEADS, HEAD_DK))


def _ret_kernel(q_ref, k_ref, v_ref, g_ref, cos_ref, sin_ref, decay_ref, qs_ref, ks_ref, cd_ref, norm_ref,
                ones_ref, out_ref, state_ref):
    @pl.when(pl.program_id(1) == 0)
    def _():
        state_ref[...] = jnp.zeros_like(state_ref)

    width = N_HEADS * HEAD_DK
    half = HEAD_DK // 2
    first_half = (lax.broadcasted_iota(jnp.int32, (CHUNK, width), 1) % HEAD_DK) < half

    def rot(t_in):
        swapped = jnp.where(first_half, pltpu.roll(t_in, width - half, 1), pltpu.roll(t_in, half, 1))
        return t_in * cos_ref[...] + swapped * sin_ref[...]

    n_b = q_ref.shape[0]
    sl = [slice(h * HEAD_DK, (h + 1) * HEAD_DK) for h in range(N_HEADS)]
    q, k, v, q_in, k_st = [], [], [], [], []
    for i in range(n_b):
        q_all = rot(q_ref[i].astype(F32))
        k_all = rot(k_ref[i].astype(F32)) * (HEAD_DK ** -0.5)
        v_all = v_ref[i].astype(F32)
        qi_all = q_all * qs_ref[...]
        ks_all = k_all * ks_ref[...]
        for h in range(N_HEADS):
            q.append(q_all[:, sl[h]])
            k.append(k_all[:, sl[h]])
            v.append(v_all[:, sl[h]])
            q_in.append(qi_all[:, sl[h]])
            k_st.append(ks_all[:, sl[h]])
    chains = range(n_b * N_HEADS)
    state = [state_ref[c] for c in chains]
    scores = [_dot_nt(q[c], k[c]) * decay_ref[c % N_HEADS] for c in chains]
    inter = [_dot(q_in[c], state[c]) for c in chains]
    o = [_dot(scores[c], v[c]) + inter[c] for c in chains]
    for c in chains:
        state_ref[c] = state[c] * cd_ref[c % N_HEADS] + _dot_tn(k_st[c], v[c])
    inv_dv = 1.0 / RET_DV
    for i in range(n_b):
        o_all = jnp.concatenate(o[i * N_HEADS:(i + 1) * N_HEADS], axis=1)
        xc = o_all - _dot(o_all, ones_ref[...]) * inv_dv
        var = _dot(xc * xc, ones_ref[...]) * inv_dv
        out_ref[i] = (xc * lax.rsqrt(var + EPS) * norm_ref[...]
                      * _silu(g_ref[i].astype(F32))).astype(out_ref.dtype)


def _retention(proj, ret_norm, b, t):
    nc = t // CHUNK
    width = N_HEADS * HEAD_DK
    half = HEAD_DK // 2
    proj3 = proj.reshape(b, t, proj.shape[-1])
    pos = jnp.arange(t, dtype=F32)
    inv_freq = jnp.power(ROPE_BASE, -jnp.arange(half, dtype=F32) / half)
    ang = pos[:, None] * inv_freq[None, :]
    cos, sin = jnp.cos(ang), jnp.sin(ang)
    cos_t = jnp.tile(jnp.concatenate([cos, cos], axis=1), (1, N_HEADS))
    sin_t = jnp.tile(jnp.concatenate([-sin, sin], axis=1), (1, N_HEADS))
    log_gamma = jnp.log1p(-jnp.exp2(-5.0 - jnp.arange(N_HEADS, dtype=F32)))
    idx = jnp.arange(CHUNK, dtype=F32)
    causal = jnp.tril(jnp.ones((CHUNK, CHUNK), bool))
    decay = jnp.exp(jnp.where(causal, (idx[:, None] - idx[None, :]) * log_gamma[:, None, None], -jnp.inf))
    qs = jnp.repeat(jnp.exp((idx + 1.0) * log_gamma[:, None]).T, HEAD_DK, axis=1)
    ks = jnp.repeat(jnp.exp((CHUNK - 1.0 - idx) * log_gamma[:, None]).T, HEAD_DK, axis=1)
    cd = jnp.broadcast_to(jnp.exp(CHUNK * log_gamma)[:, None, None], (N_HEADS, 1, RET_DV))
    base = (N_HEADS * (2 * HEAD_DK + GDN_DV) + N_HEADS * GDN_DV) // width
    nb = _batch_per_step(b)
    col = lambda k: pl.BlockSpec((nb, CHUNK, width), lambda i, c: (i, c, base + k))
    const2 = lambda shape: pl.BlockSpec(shape, lambda i, c: (0,) * len(shape))
    return pl.pallas_call(
        _ret_kernel,
        grid=(b // nb, nc),
        in_specs=[col(0), col(1), col(2), col(3),
                  pl.BlockSpec((CHUNK, width), lambda i, c: (c, 0)),
                  pl.BlockSpec((CHUNK, width), lambda i, c: (c, 0)),
                  const2((N_HEADS, CHUNK, CHUNK)), const2((CHUNK, width)), const2((CHUNK, width)),
                  const2((N_HEADS, 1, RET_DV)), const2((1, width)), const2((width, width))],
        out_specs=pl.BlockSpec((nb, CHUNK, width), lambda i, c: (i, c, 0)),
        out_shape=jax.ShapeDtypeStruct((b, t, width), BF16),
        scratch_shapes=[pltpu.VMEM((nb * N_HEADS, HEAD_DK, RET_DV), F32)],
        compiler_params=_cparams(("parallel", "arbitrary")),
        name="retention_mixer",
    )(proj3, proj3, proj3, proj3, cos_t, sin_t, decay, qs, ks, cd, ret_norm.reshape(1, width).astype(F32),
      _head_ones(N_HEADS, RET_DV))


def _mlstm_kernel(q_ref, k_ref, v_ref, o_ref, gate_ref, gate_t_ref, brow_ref, bcol_ref, norm_ref,
                  ltri_ref, utri_ref, out_ref, state_ref, m_ref):
    @pl.when(pl.program_id(1) == 0)
    def _():
        state_ref[...] = jnp.zeros_like(state_ref)
        m_ref[...] = jnp.zeros_like(m_ref)

    n_b = q_ref.shape[0]
    causal = _causal(CHUNK)
    ones_col = (lax.broadcasted_iota(jnp.int32, (CHUNK, MLSTM_DV), 1) == 0).astype(F32)
    sl = [slice(h * HEAD_DK, (h + 1) * HEAD_DK) for h in range(N_HEADS)]
    sv = [slice(h * MLSTM_DV, (h + 1) * MLSTM_DV) for h in range(N_HEADS)]

    q, k, vx, bc, br, ig_c, ig_r = [], [], [], [], [], [], []
    for i in range(n_b):
        q_all = q_ref[i].astype(F32)
        k_all = k_ref[i].astype(F32) * (HEAD_DK ** -0.5)
        v_all = v_ref[i].astype(F32)
        gl = gate_ref[i] + brow_ref[...]
        bcum_c = _dot_f32(ltri_ref[...], -_softplus(-gl))
        gt = gate_t_ref[i] + bcol_ref[...]
        bcum_r = _dot_f32(-_softplus(-gt), utri_ref[...])
        for h in range(N_HEADS):
            q.append(q_all[:, sl[h]])
            k.append(k_all[:, sl[h]])
            vx.append(jnp.concatenate([v_all[:, sv[h]], ones_col], axis=1))
            bc.append(bcum_c[:, 8 + h:9 + h])
            br.append(bcum_r[8 + h:9 + h, :])
            ig_c.append(gl[:, h:h + 1])
            ig_r.append(gt[h:h + 1, :])

    chains = range(n_b * N_HEADS)
    log_w = [jnp.where(causal, bc[c] - br[c] + ig_r[c], -jnp.inf) for c in chains]
    m_intra = [jnp.max(log_w[c], axis=-1, keepdims=True) for c in chains]
    b_end = [bc[c][CHUNK - 1:CHUNK, :] for c in chains]
    lw_end = [b_end[c] - bc[c] + ig_c[c] for c in chains]
    m_end = [jnp.max(lw_end[c], axis=0, keepdims=True) for c in chains]
    m_s = [m_ref[c][:, 0:1] for c in chains]
    cx = [state_ref[c] for c in chains]
    m_t = [jnp.maximum(bc[c] + m_s[c], m_intra[c]) for c in chains]
    inter = [jnp.exp(bc[c] + m_s[c] - m_t[c]) for c in chains]
    qk = [_dot_nt(q[c], k[c]) for c in chains]
    qc = [_dot(q[c], cx[c]) for c in chains]
    s = [qk[c] * jnp.exp(log_w[c] - m_t[c]) for c in chains]
    numx = [inter[c] * qc[c] + _dot(s[c], vx[c]) for c in chains]
    m_new = [jnp.maximum(b_end[c] + m_s[c], m_end[c]) for c in chains]
    wk = [k[c] * jnp.exp(lw_end[c] - m_new[c]) for c in chains]
    for c in chains:
        state_ref[c] = jnp.exp(b_end[c] + m_s[c] - m_new[c]) * cx[c] + _dot_tn(wk[c], vx[c])
        m_ref[c] = jnp.broadcast_to(m_new[c], (1, GATE_LANES))
    hh = [numx[c][:, :MLSTM_DV] / jnp.maximum(jnp.abs(numx[c][:, MLSTM_DV:MLSTM_DV + 1]), jnp.exp(-m_t[c]))
          for c in chains]
    outs = [_rms(hh[c], norm_ref[:, sv[c % N_HEADS]]) for c in chains]
    for i in range(n_b):
        out_ref[i] = (jnp.concatenate(outs[i * N_HEADS:(i + 1) * N_HEADS], axis=1)
                      * _sigmoid(o_ref[i].astype(F32))).astype(out_ref.dtype)


def _mlstm(proj, gates, gates_t, gate_bias, mlstm_norm, b, t):
    nc = t // CHUNK
    n_qk = N_HEADS * HEAD_DK
    n_v = N_HEADS * MLSTM_DV
    proj3 = proj.reshape(b, t, proj.shape[-1])
    ltri, utri = _tri_consts()
    nb = _batch_per_step(b)
    const2 = lambda shape: pl.BlockSpec(shape, lambda i, c: (0,) * len(shape))
    return pl.pallas_call(
        _mlstm_kernel,
        grid=(b // nb, nc),
        in_specs=[pl.BlockSpec((nb, CHUNK, n_qk), lambda i, c: (i, c, 0)),
                  pl.BlockSpec((nb, CHUNK, n_qk), lambda i, c: (i, c, 1)),
                  pl.BlockSpec((nb, CHUNK, n_v), lambda i, c: (i, c, 2 * n_qk // n_v)),
                  pl.BlockSpec((nb, CHUNK, n_v), lambda i, c: (i, c, 2 * n_qk // n_v + 1)),
                  pl.BlockSpec((nb, CHUNK, GATE_LANES), lambda i, c: (i, c, 0)),
                  pl.BlockSpec((nb, None, 16, CHUNK), lambda i, c: (i, c, 0, 0)),
                  const2((1, GATE_LANES)), const2((16, 1)), const2((1, n_v)),
                  const2((CHUNK, CHUNK)), const2((CHUNK, CHUNK))],
        out_specs=pl.BlockSpec((nb, CHUNK, n_v), lambda i, c: (i, c, 0)),
        out_shape=jax.ShapeDtypeStruct((b, t, n_v), BF16),
        scratch_shapes=[pltpu.VMEM((nb * N_HEADS, HEAD_DK, 2 * MLSTM_DV), F32),
                        pltpu.VMEM((nb * N_HEADS, 1, GATE_LANES), F32)],
        compiler_params=_cparams(("parallel", "arbitrary")),
        name="mlstm_mixer",
    )(proj3, proj3, proj3, proj3, gates.reshape(b, t, GATE_LANES), gates_t,
      _lane_row(gate_bias, 0), _sub_col(gate_bias, 0), mlstm_norm.reshape(1, n_v).astype(F32), ltri, utri)


def _gates_transposed(gates, b, t):
    nc = t // CHUNK
    return jnp.transpose(gates[:, :16].reshape(b, nc, CHUNK, 16), (0, 1, 3, 2))


def _outproj_ffn_kernel(x_ref, ma_ref, mb_ref, wo_ref, nw_ref, w1_ref, w3_ref, w2_ref, out_ref,
                        x1_ref, h_ref, acc_ref):
    j = pl.program_id(1)
    n_a = ma_ref.shape[1]

    @pl.when(j == 0)
    def _():
        x1 = (x_ref[...] + jnp.dot(ma_ref[...], wo_ref[:n_a, :], preferred_element_type=F32)
              + jnp.dot(mb_ref[...], wo_ref[n_a:, :], preferred_element_type=F32))
        x1_ref[...] = x1
        h_ref[...] = _rms(x1, nw_ref[...]).astype(BF16)
        acc_ref[...] = jnp.zeros_like(acc_ref)

    h = h_ref[...]
    a = jnp.dot(h, w1_ref[...], preferred_element_type=F32)
    g = jnp.dot(h, w3_ref[...], preferred_element_type=F32)
    acc_ref[...] += jnp.dot((_silu(a) * g).astype(BF16), w2_ref[...], preferred_element_type=F32)

    @pl.when(j == pl.num_programs(1) - 1)
    def _():
        out_ref[...] = x1_ref[...] + acc_ref[...]


def _outproj_ffn(x2d, mix_a, mix_b, w_out, ffn_norm, w1, w3, w2, tm=512, n_ff_chunks=2):
    n, d = x2d.shape
    tm = min(tm, n)
    d_ff = w1.shape[1]
    tf = d_ff // n_ff_chunks
    n_a, n_b = mix_a.shape[1], mix_b.shape[1]
    return pl.pallas_call(
        _outproj_ffn_kernel,
        grid=(n // tm, n_ff_chunks),
        in_specs=[pl.BlockSpec((tm, d), lambda i, j: (i, 0)),
                  pl.BlockSpec((tm, n_a), lambda i, j: (i, 0)),
                  pl.BlockSpec((tm, n_b), lambda i, j: (i, 0)),
                  pl.BlockSpec((n_a + n_b, d), lambda i, j: (0, 0)),
                  pl.BlockSpec((1, d), lambda i, j: (0, 0)),
                  pl.BlockSpec((d, tf), lambda i, j: (0, j)),
                  pl.BlockSpec((d, tf), lambda i, j: (0, j)),
                  pl.BlockSpec((tf, d), lambda i, j: (j, 0))],
        out_specs=pl.BlockSpec((tm, d), lambda i, j: (i, 0)),
        out_shape=jax.ShapeDtypeStruct((n, d), F32),
        scratch_shapes=[pltpu.VMEM((tm, d), F32), pltpu.VMEM((tm, d), BF16), pltpu.VMEM((tm, d), F32)],
        compiler_params=_cparams(("parallel", "arbitrary")),
        name="outproj_ffn",
    )(x2d, mix_a, mix_b, w_out, ffn_norm.reshape(1, d), w1, w3, w2)


def _outproj_router_kernel(x_ref, m_ref, wo_ref, nw_ref, r_ref, x3_ref, hn_ref, idx_ref, gate_ref):
    x3 = x_ref[...] + jnp.dot(m_ref[...], wo_ref[...], preferred_element_type=F32)
    x3_ref[...] = x3
    hn = _rms(x3, nw_ref[...])
    hn_ref[...] = hn
    logits = _dot_f32(hn, r_ref[...])
    lane = lax.broadcasted_iota(jnp.int32, logits.shape, 1)
    logits = jnp.where(lane < N_EXPERTS, logits, -jnp.inf)
    m0 = jnp.max(logits, axis=-1, keepdims=True)
    i0 = jnp.min(jnp.where(logits == m0, lane, GATE_LANES), axis=-1, keepdims=True)
    rest = jnp.where(lane == i0, -jnp.inf, logits)
    m1 = jnp.max(rest, axis=-1, keepdims=True)
    i1 = jnp.min(jnp.where(rest == m1, lane, GATE_LANES), axis=-1, keepdims=True)
    e1 = jnp.exp(m1 - m0)
    inv = 1.0 / (1.0 + e1)
    idx_ref[...] = jnp.where(lane == 0, i0, jnp.where(lane == 1, i1, 0))
    gate_ref[...] = jnp.where(lane == 0, inv, jnp.where(lane == 1, e1 * inv, 0.0))


def _outproj_router(x2d, mix, w_out, ffn_norm, router, tm=512):
    n, d = x2d.shape
    tm = min(tm, n)
    n_m = mix.shape[1]
    router_pad = jnp.zeros((d, GATE_LANES), F32).at[:, :N_EXPERTS].set(router.astype(F32))
    row = lambda w: pl.BlockSpec((tm, w), lambda i: (i, 0))
    return pl.pallas_call(
        _outproj_router_kernel,
        grid=(n // tm,),
        in_specs=[row(d), row(n_m),
                  pl.BlockSpec((n_m, d), lambda i: (0, 0)),
                  pl.BlockSpec((1, d), lambda i: (0, 0)),
                  pl.BlockSpec((d, GATE_LANES), lambda i: (0, 0))],
        out_specs=[row(d), row(d), row(GATE_LANES), row(GATE_LANES)],
        out_shape=[jax.ShapeDtypeStruct((n, d), F32), jax.ShapeDtypeStruct((n, d), F32),
                   jax.ShapeDtypeStruct((n, GATE_LANES), jnp.int32),
                   jax.ShapeDtypeStruct((n, GATE_LANES), F32)],
        compiler_params=_cparams(("parallel",)),
        name="outproj_router",
    )(x2d, mix, w_out, ffn_norm.reshape(1, d), router_pad)


def _rank_kernel(idx_ref, tri_ref, rank_ref, count_ref, carry_ref):
    @pl.when(pl.program_id(0) == 0)
    def _():
        carry_ref[...] = jnp.zeros_like(carry_ref)

    idx = idx_ref[...]
    lane = lax.broadcasted_iota(jnp.int32, idx.shape, 1)
    e0, e1 = idx[:, 0:1], idx[:, 1:2]
    member = ((lane == e0) | (lane == e1)).astype(F32)
    before = _dot(tri_ref[...], member) + carry_ref[...]
    r0 = jnp.sum(jnp.where(lane == e0, before, 0.0), axis=-1, keepdims=True)
    r1 = jnp.sum(jnp.where(lane == e1, before, 0.0), axis=-1, keepdims=True)
    rank_ref[...] = jnp.where(lane == 0, r0, jnp.where(lane == 1, r1, 0.0)).astype(jnp.int32)
    carry_ref[...] += jnp.sum(member, axis=0, keepdims=True)
    count_ref[...] = carry_ref[...].astype(jnp.int32)


def _expert_ranks(ridx, tm=512):
    n = ridx.shape[0]
    tm = min(tm, n)
    i = np.arange(tm)
    tri = jnp.asarray((i[None, :] < i[:, None]).astype(np.float32)).astype(BF16)
    return pl.pallas_call(
        _rank_kernel,
        grid=(n // tm,),
        in_specs=[pl.BlockSpec((tm, GATE_LANES), lambda i: (i, 0)),
                  pl.BlockSpec((tm, tm), lambda i: (0, 0))],
        out_specs=[pl.BlockSpec((tm, GATE_LANES), lambda i: (i, 0)),
                   pl.BlockSpec((1, GATE_LANES), lambda i: (0, 0))],
        out_shape=[jax.ShapeDtypeStruct((n, GATE_LANES), jnp.int32),
                   jax.ShapeDtypeStruct((1, GATE_LANES), jnp.int32)],
        scratch_shapes=[pltpu.VMEM((1, GATE_LANES), F32)],
        compiler_params=_cparams(("arbitrary",)),
        name="expert_ranks",
    )(ridx, tri)


def _row_copy(src_ref, src_row, dst_ref, dst_row, sem):
    return pltpu.make_async_copy(src_ref.at[pl.ds(src_row, 1)], dst_ref.at[pl.ds(dst_row, 1)], sem)


def _dispatch_kernel(dest_ref, hn_ref, slots_in_ref, slots_ref, sem, *, tm, n_tok):
    del slots_in_ref
    base = pl.program_id(0) * tm

    def start(r, carry):
        _row_copy(hn_ref, r, slots_ref, dest_ref[base + r], sem).start()
        _row_copy(hn_ref, r, slots_ref, dest_ref[n_tok + base + r], sem).start()
        return carry

    lax.fori_loop(0, tm, start, 0, unroll=DMA_UNROLL)
    for _ in range(2):
        pltpu.make_async_copy(hn_ref, slots_ref.at[pl.ds(0, tm)], sem).wait()


def _dispatch(dest_flat, hn, n_slots, tm=256):
    n, d = hn.shape
    tm = min(tm, n)
    grid_spec = pltpu.PrefetchScalarGridSpec(
        num_scalar_prefetch=1,
        grid=(n // tm,),
        in_specs=[pl.BlockSpec((tm, d), lambda i, dest: (i, 0)),
                  pl.BlockSpec(memory_space=pl.ANY)],
        out_specs=pl.BlockSpec(memory_space=pl.ANY),
        scratch_shapes=[pltpu.SemaphoreType.DMA(())],
    )
    return pl.pallas_call(
        functools.partial(_dispatch_kernel, tm=tm, n_tok=n),
        grid_spec=grid_spec,
        out_shape=jax.ShapeDtypeStruct((n_slots, d), hn.dtype),
        input_output_aliases={2: 0},
        compiler_params=_cparams(("arbitrary",)),
        name="row_dispatch",
    )(dest_flat, hn, jnp.zeros((n_slots, d), hn.dtype))


def _moe_kernel(be_ref, nused_ref, x_ref, w1_ref, w3_ref, w2_ref, out_ref):
    i, j = pl.program_id(0), pl.program_id(1)

    @pl.when(i < nused_ref[0])
    def _():
        x = x_ref[...].astype(BF16)
        a = jnp.dot(x, w1_ref[...], preferred_element_type=F32)
        g = jnp.dot(x, w3_ref[...], preferred_element_type=F32)
        y = jnp.dot((_silu(a) * g).astype(BF16), w2_ref[...], preferred_element_type=F32)

        @pl.when(j == 0)
        def _():
            out_ref[...] = y

        @pl.when(j > 0)
        def _():
            out_ref[...] += y

    @pl.when(i >= nused_ref[0])
    def _():
        out_ref[...] = jnp.zeros_like(out_ref)


def _moe_ffn(block_expert, n_used, slots, w1, w3, w2, n_ff_chunks=2):
    n_slots, d = slots.shape
    n_blocks = n_slots // MOE_BLOCK
    d_ff = w1.shape[2]
    tf = d_ff // n_ff_chunks
    last_j = n_ff_chunks - 1

    def blk(i, nused):
        return jnp.minimum(i, nused[0] - 1)

    def chunk(i, j, nused):
        return jnp.where(i < nused[0], j, last_j)

    grid_spec = pltpu.PrefetchScalarGridSpec(
        num_scalar_prefetch=2,
        grid=(n_blocks, n_ff_chunks),
        in_specs=[pl.BlockSpec((MOE_BLOCK, d), lambda i, j, be, nu: (blk(i, nu), 0)),
                  pl.BlockSpec((None, d, tf), lambda i, j, be, nu: (be[blk(i, nu)], 0, chunk(i, j, nu))),
                  pl.BlockSpec((None, d, tf), lambda i, j, be, nu: (be[blk(i, nu)], 0, chunk(i, j, nu))),
                  pl.BlockSpec((None, tf, d), lambda i, j, be, nu: (be[blk(i, nu)], chunk(i, j, nu), 0))],
        out_specs=pl.BlockSpec((MOE_BLOCK, d), lambda i, j, be, nu: (i, 0)),
    )
    return pl.pallas_call(
        _moe_kernel,
        grid_spec=grid_spec,
        out_shape=jax.ShapeDtypeStruct((n_slots, d), F32),
        compiler_params=_cparams(("arbitrary", "arbitrary")),
        name="moe_ffn",
    )(block_expert, n_used, slots, w1, w3, w2)


def _combine_kernel(dest_ref, x_ref, gate_ref, nw_ref, y_ref, out_ref, ybuf_ref, sem, *, tm, n_tok):
    base = pl.program_id(0) * tm

    def start(r, carry):
        _row_copy(y_ref, dest_ref[base + r], ybuf_ref.at[0], r, sem).start()
        _row_copy(y_ref, dest_ref[n_tok + base + r], ybuf_ref.at[1], r, sem).start()
        return carry

    lax.fori_loop(0, tm, start, 0, unroll=DMA_UNROLL)
    for slot in range(2):
        pltpu.make_async_copy(y_ref.at[pl.ds(0, tm)], ybuf_ref.at[slot], sem).wait()
    gate = gate_ref[...]
    y = x_ref[...] + (ybuf_ref[0] * gate[:, 0:1] + ybuf_ref[1] * gate[:, 1:2])
    out_ref[...] = _rms(y, nw_ref[...])


def _combine(dest_flat, x2d, rgate, final_norm, yb, tm=256):
    n, d = x2d.shape
    tm = min(tm, n)
    grid_spec = pltpu.PrefetchScalarGridSpec(
        num_scalar_prefetch=1,
        grid=(n // tm,),
        in_specs=[pl.BlockSpec((tm, d), lambda i, dest: (i, 0)),
                  pl.BlockSpec((tm, GATE_LANES), lambda i, dest: (i, 0)),
                  pl.BlockSpec((1, d), lambda i, dest: (0, 0)),
                  pl.BlockSpec(memory_space=pl.ANY)],
        out_specs=pl.BlockSpec((tm, d), lambda i, dest: (i, 0)),
        scratch_shapes=[pltpu.VMEM((2, tm, d), F32), pltpu.SemaphoreType.DMA(())],
    )
    return pl.pallas_call(
        functools.partial(_combine_kernel, tm=tm, n_tok=n),
        grid_spec=grid_spec,
        out_shape=jax.ShapeDtypeStruct((n, d), F32),
        compiler_params=_cparams(("arbitrary",)),
        name="combine_norm",
    )(dest_flat, x2d, rgate, final_norm.reshape(1, d), yb)


def _split_in_proj(w_in, lo, hi):
    main = jnp.concatenate([w_in[:, :lo], w_in[:, hi:]], axis=1)
    gate = jnp.pad(w_in[:, lo:hi], ((0, 0), (0, GATE_LANES - (hi - lo))))
    return jnp.concatenate([main, gate], axis=1).astype(BF16), main.shape[1]


def kernel(x, mix_norm_0, w_in_0, conv_w_0, a_log_0, dt_bias_0, gdn_norm_0, ret_norm_0, w_out_0, ffn_norm_0,
           ffn_w1_0, ffn_w3_0, ffn_w2_0, mix_norm_1, w_in_1, gate_bias_1, mlstm_norm_1, w_out_1, ffn_norm_1,
           router_1, exp_w1_1, exp_w3_1, exp_w2_1, final_norm):
    b, t, d = x.shape
    n = b * t
    x2d = x.reshape(n, d)

    gdn_qkv = N_HEADS * (2 * HEAD_DK + GDN_DV)
    w0, n_main0 = _split_in_proj(w_in_0, gdn_qkv, gdn_qkv + 2 * N_HEADS)
    proj0, gates0 = _norm_proj(x2d, mix_norm_0, w0, n_main0)
    o_gdn = _gdn(proj0, gates0, _gates_transposed(gates0, b, t), conv_w_0, a_log_0, dt_bias_0, gdn_norm_0, b, t)
    o_ret = _retention(proj0, ret_norm_0, b, t)
    x2 = _outproj_ffn(x2d, o_gdn.reshape(n, -1), o_ret.reshape(n, -1), w_out_0.astype(BF16), ffn_norm_0,
                      ffn_w1_0.astype(BF16), ffn_w3_0.astype(BF16), ffn_w2_0.astype(BF16))

    n_main1 = w_in_1.shape[1] - 2 * N_HEADS
    w1, _ = _split_in_proj(w_in_1, n_main1, n_main1 + 2 * N_HEADS)
    proj1, gates1 = _norm_proj(x2, mix_norm_1, w1, n_main1)
    h_mix = _mlstm(proj1, gates1, _gates_transposed(gates1, b, t), gate_bias_1, mlstm_norm_1, b, t)
    x3, hn, ridx, rgate = _outproj_router(x2, h_mix.reshape(n, -1), w_out_1.astype(BF16), ffn_norm_1, router_1)

    rank, counts = _expert_ranks(ridx)
    counts = counts[0, :N_EXPERTS]
    padded = (counts + MOE_BLOCK - 1) // MOE_BLOCK * MOE_BLOCK
    pad_end = jnp.cumsum(padded)
    pad_start = pad_end - padded
    dest_flat = jnp.concatenate([pad_start[ridx[:, 0]] + rank[:, 0], pad_start[ridx[:, 1]] + rank[:, 1]])
    n_blocks = -(-(2 * n) // MOE_BLOCK) + N_EXPERTS
    block_row0 = jnp.arange(n_blocks, dtype=jnp.int32) * MOE_BLOCK
    block_expert = jnp.minimum(jnp.sum((pad_end[None, :] <= block_row0[:, None]).astype(jnp.int32), axis=1),
                               N_EXPERTS - 1)
    n_used = (pad_end[-1:] // MOE_BLOCK).astype(jnp.int32)

    slots = _dispatch(dest_flat, hn, n_blocks * MOE_BLOCK)
    yb = _moe_ffn(block_expert, n_used, slots, exp_w1_1.astype(BF16), exp_w3_1.astype(BF16),
                  exp_w2_1.astype(BF16))
    out = _combine(dest_flat, x3, rgate, final_norm, yb)
    return out.reshape(b, t, d)
```

```python
import functools
import math

import numpy as np
import jax
import jax.numpy as jnp
from jax import lax
from jax.experimental import pallas as pl
from jax.experimental.pallas import tpu as pltpu

F32 = jnp.float32
BF16 = jnp.bfloat16
HIGHEST = lax.Precision.HIGHEST

EPS = 1e-6
CHUNK = 128
RET_CHUNK = 128
MLSTM_CHUNK = 128
CONV_WIDTH = 4
N_HEADS = 8
HEAD_DK = 64
GDN_DV = 64
RET_DV = 64
MLSTM_DV = 128
ROPE_BASE = 10000.0
N_EXPERTS = 8
MOE_BLOCK = 512
GATE_LANES = 128
VMEM_LIMIT = 52 * 1024 * 1024
DMA_UNROLL = 8


def _cparams(sem):
    return pltpu.CompilerParams(dimension_semantics=sem, vmem_limit_bytes=VMEM_LIMIT)


def _dot(a, b):
    return jnp.dot(a.astype(BF16), b.astype(BF16), preferred_element_type=F32)


def _dot_nt(a, b):
    return lax.dot_general(a.astype(BF16), b.astype(BF16), (((1,), (1,)), ((), ())),
                           preferred_element_type=F32)


def _dot_tn(a, b):
    return lax.dot_general(a.astype(BF16), b.astype(BF16), (((0,), (0,)), ((), ())),
                           preferred_element_type=F32)


def _dot_f32(a, b):
    return jnp.dot(a, b, precision=HIGHEST, preferred_element_type=F32)


def _rms(x, w):
    return x * lax.rsqrt(jnp.mean(x * x, axis=-1, keepdims=True) + EPS) * w


def _silu(x):
    return x * (1.0 / (1.0 + jnp.exp(-x)))


def _sigmoid(x):
    return 1.0 / (1.0 + jnp.exp(-x))


def _softplus(x):
    return jnp.maximum(x, 0.0) + jnp.log1p(jnp.exp(-jnp.abs(x)))


def _causal(n, strict=False):
    r = lax.broadcasted_iota(jnp.int32, (n, n), 0)
    c = lax.broadcasted_iota(jnp.int32, (n, n), 1)
    return (r > c) if strict else (r >= c)


def _norm_proj_kernel(x_ref, nw_ref, w_ref, main_ref, gate_ref, *, n_main, col_chunk):
    h = _rms(x_ref[...], nw_ref[...]).astype(BF16)
    for c0 in range(0, n_main, col_chunk):
        main_ref[:, c0:c0 + col_chunk] = jnp.dot(
            h, w_ref[:, c0:c0 + col_chunk], preferred_element_type=F32).astype(BF16)
    gate_ref[...] = jnp.dot(h, w_ref[:, n_main:], preferred_element_type=F32)


def _norm_proj(x2d, norm_w, w_cat, n_main, tm=512, col_chunk=512):
    n, d = x2d.shape
    tm = min(tm, n)
    n_cat = w_cat.shape[1]
    return pl.pallas_call(
        functools.partial(_norm_proj_kernel, n_main=n_main, col_chunk=col_chunk),
        grid=(n // tm,),
        in_specs=[pl.BlockSpec((tm, d), lambda i: (i, 0)),
                  pl.BlockSpec((1, d), lambda i: (0, 0)),
                  pl.BlockSpec((d, n_cat), lambda i: (0, 0))],
        out_specs=[pl.BlockSpec((tm, n_main), lambda i: (i, 0)),
                   pl.BlockSpec((tm, GATE_LANES), lambda i: (i, 0))],
        out_shape=[jax.ShapeDtypeStruct((n, n_main), BF16),
                   jax.ShapeDtypeStruct((n, GATE_LANES), F32)],
        compiler_params=_cparams(("parallel",)),
        name="norm_proj",
    )(x2d, norm_w.reshape(1, d), w_cat)


def _gdn_kernel(qkv_ref, z_ref, gate_ref, gate_t_ref, conv_ref, arow_ref, dtrow_ref, acol_ref, dtcol_ref,
                norm_ref, ltri_ref, utri_ref, lvl_ref, ones_ref, out_ref, state_ref, xbuf_ref):
    n_b = qkv_ref.shape[0]
    n_qk = N_HEADS * HEAD_DK

    @pl.when(pl.program_id(1) == 0)
    def _():
        state_ref[...] = jnp.zeros_like(state_ref)
        xbuf_ref[:, 0:8, :] = jnp.zeros((n_b, 8, xbuf_ref.shape[2]), F32)

    causal = _causal(CHUNK)
    strict = _causal(CHUNK, strict=True)
    eye = (lax.broadcasted_iota(jnp.int32, (CHUNK, CHUNK), 0)
           == lax.broadcasted_iota(jnp.int32, (CHUNK, CHUNK), 1)).astype(F32)
    sl = [slice(h * HEAD_DK, (h + 1) * HEAD_DK) for h in range(N_HEADS)]

    q, k, v, bc, gcc, gcr = [], [], [], [], [], []
    for i in range(n_b):
        xbuf_ref[i, 8:8 + CHUNK, :] = qkv_ref[i].astype(F32)
        pre = None
        for j in range(CONV_WIDTH):
            term = xbuf_ref[i, pl.ds(8 - (CONV_WIDTH - 1) + j, CHUNK), :] * conv_ref[j:j + 1, :]
            pre = term if pre is None else pre + term
        xbuf_ref[i, 0:8, :] = xbuf_ref[i, CHUNK:CHUNK + 8, :]
        act = _silu(pre)
        q_all, k_all, v_all = act[:, :n_qk], act[:, n_qk:2 * n_qk], act[:, 2 * n_qk:]
        q_all = q_all * lax.rsqrt(_dot(q_all * q_all, ones_ref[...]) + EPS) * (HEAD_DK ** -0.5)
        k_all = k_all * lax.rsqrt(_dot(k_all * k_all, ones_ref[...]) + EPS)
        gl = gate_ref[i]
        beta_c = _sigmoid(gl)
        g_c = -jnp.exp(arow_ref[...]) * _softplus(gl + dtrow_ref[...])
        gc_c = _dot_f32(ltri_ref[...], g_c)
        g_r = -jnp.exp(acol_ref[...]) * _softplus(gate_t_ref[i] + dtcol_ref[...])
        gc_r = _dot_f32(g_r, utri_ref[...])
        for h in range(N_HEADS):
            q.append(q_all[:, sl[h]])
            k.append(k_all[:, sl[h]])
            v.append(v_all[:, sl[h]])
            bc.append(beta_c[:, h:h + 1])
            gcc.append(gc_c[:, 8 + h:9 + h])
            gcr.append(gc_r[8 + h:9 + h, :])

    chains = range(n_b * N_HEADS)
    decay = [jnp.exp(jnp.where(causal, gcc[c] - gcr[c], -jnp.inf)) for c in chains]
    kb = [k[c] * bc[c] for c in chains]
    scores = [_dot_nt(jnp.concatenate([kb[c], q[c]], axis=0), k[c]) for c in chains]
    lmat = [jnp.where(strict, scores[c][:CHUNK] * decay[c], 0.0) for c in chains]
    qk = [scores[c][CHUNK:] * decay[c] for c in chains]
    tinv = [eye - lmat[c] * lvl_ref[0] for c in chains]
    for lvl in range(1, lvl_ref.shape[0]):
        inner = [_dot(lmat[c] * lvl_ref[lvl], tinv[c]) for c in chains]
        tinv = [tinv[c] - _dot(tinv[c], inner[c]) for c in chains]
    egc = [jnp.exp(gcc[c]) for c in chains]
    uw = [_dot(tinv[c], jnp.concatenate([v[c] * bc[c], kb[c] * egc[c]], axis=1)) for c in chains]
    state = [state_ref[c] for c in chains]
    v_new = [uw[c][:, :GDN_DV] - _dot(uw[c][:, GDN_DV:], state[c]) for c in chains]
    o = [_dot(q[c] * egc[c], state[c]) + _dot(qk[c], v_new[c]) for c in chains]
    g_last = [gcc[c][CHUNK - 1:CHUNK, :] for c in chains]
    k_dec = [k[c] * jnp.exp(g_last[c] - gcc[c]) for c in chains]
    for c in chains:
        state_ref[c] = state[c] * jnp.exp(g_last[c]) + _dot_tn(k_dec[c], v_new[c])
    for i in range(n_b):
        o_all = jnp.concatenate(o[i * N_HEADS:(i + 1) * N_HEADS], axis=1)
        mean_sq = _dot(o_all * o_all, ones_ref[...]) * (1.0 / GDN_DV)
        out_ref[i] = (o_all * lax.rsqrt(mean_sq + EPS) * norm_ref[...]
                      * _silu(z_ref[i].astype(F32))).astype(out_ref.dtype)


def _level_masks():
    i = np.arange(CHUNK)[:, None]
    j = np.arange(CHUNK)[None, :]
    masks = []
    for lvl in range(CHUNK.bit_length() - 1):
        same_parent = (i >> (lvl + 1)) == (j >> (lvl + 1))
        masks.append((same_parent & (((i >> lvl) & 1) == 1) & (((j >> lvl) & 1) == 0)).astype(np.float32))
    return jnp.asarray(np.stack(masks))


def _tri_consts(ck):
    i = np.arange(ck)[:, None]
    j = np.arange(ck)[None, :]
    return jnp.asarray((j <= i).astype(np.float32)), jnp.asarray((i <= j).astype(np.float32))


def _lane_row(vals, offset):
    return jnp.zeros((1, GATE_LANES), F32).at[0, offset:offset + vals.shape[0]].set(vals.astype(F32))


def _sub_col(vals, offset):
    return jnp.zeros((16, 1), F32).at[offset:offset + vals.shape[0], 0].set(vals.astype(F32))


def _head_ones(n_heads, width):
    g = np.arange(n_heads * width) // width
    return jnp.asarray((g[:, None] == g[None, :]).astype(np.float32)).astype(BF16)


def _batch_per_step(b):
    return 2 if b % 2 == 0 else 1


def _gdn(proj, gates, gates_t, conv_w, a_log, dt_bias, gdn_norm, b, t):
    nc = t // CHUNK
    nb = _batch_per_step(b)
    n_qkv = N_HEADS * (2 * HEAD_DK + GDN_DV)
    n_z = N_HEADS * GDN_DV
    proj3 = proj.reshape(b, t, proj.shape[-1])
    ltri, utri = _tri_consts(CHUNK)
    const2 = lambda shape: pl.BlockSpec(shape, lambda i, c: (0,) * len(shape))
    return pl.pallas_call(
        _gdn_kernel,
        grid=(b // nb, nc),
        in_specs=[pl.BlockSpec((nb, CHUNK, n_qkv), lambda i, c: (i, c, 0)),
                  pl.BlockSpec((nb, CHUNK, n_z), lambda i, c: (i, c, n_qkv // n_z)),
                  pl.BlockSpec((nb, CHUNK, GATE_LANES), lambda i, c: (i, c, 0)),
                  pl.BlockSpec((nb, None, 16, CHUNK), lambda i, c: (i, c, 0, 0)),
                  const2((CONV_WIDTH, n_qkv)), const2((1, GATE_LANES)), const2((1, GATE_LANES)),
                  const2((16, 1)), const2((16, 1)), const2((1, n_z)),
                  const2((CHUNK, CHUNK)), const2((CHUNK, CHUNK)),
                  const2((CHUNK.bit_length() - 1, CHUNK, CHUNK)),
                  const2((n_z, n_z))],
        out_specs=pl.BlockSpec((nb, CHUNK, n_z), lambda i, c: (i, c, 0)),
        out_shape=jax.ShapeDtypeStruct((b, t, n_z), BF16),
        scratch_shapes=[pltpu.VMEM((nb * N_HEADS, HEAD_DK, GDN_DV), F32),
                        pltpu.VMEM((nb, CHUNK + 8, n_qkv), F32)],
        compiler_params=_cparams(("parallel", "arbitrary")),
        name="gdn_mixer",
    )(proj3, proj3, gates.reshape(b, t, GATE_LANES), gates_t, conv_w.astype(F32),
      _lane_row(a_log, 8), _lane_row(dt_bias, 8), _sub_col(a_log, 8), _sub_col(dt_bias, 8),
      jnp.tile(gdn_norm.astype(F32), N_HEADS).reshape(1, n_z), ltri, utri, _level_masks(),
      _head_ones(N_HEADS, HEAD_DK))


def _ret_kernel(q_ref, k_ref, v_ref, g_ref, cos_ref, sin_ref, decay_ref, qs_ref, ks_ref, cd_ref, norm_ref,
                ones_ref, out_ref, state_ref):
    @pl.when(pl.program_id(1) == 0)
    def _():
        state_ref[...] = jnp.zeros_like(state_ref)

    width = N_HEADS * HEAD_DK
    half = HEAD_DK // 2
    first_half = (lax.broadcasted_iota(jnp.int32, (q_ref.shape[1], width), 1) % HEAD_DK) < half

    def rot(t_in):
        swapped = jnp.where(first_half, pltpu.roll(t_in, width - half, 1), pltpu.roll(t_in, half, 1))
        return t_in * cos_ref[...] + swapped * sin_ref[...]

    n_b = q_ref.shape[0]
    sl = [slice(h * HEAD_DK, (h + 1) * HEAD_DK) for h in range(N_HEADS)]
    q, k, v, q_in, k_st = [], [], [], [], []
    for i in range(n_b):
        q_all = rot(q_ref[i].astype(F32))
        k_all = rot(k_ref[i].astype(F32)) * (HEAD_DK ** -0.5)
        v_all = v_ref[i].astype(F32)
        qi_all = q_all * qs_ref[...]
        ks_all = k_all * ks_ref[...]
        for h in range(N_HEADS):
            q.append(q_all[:, sl[h]])
            k.append(k_all[:, sl[h]])
            v.append(v_all[:, sl[h]])
            q_in.append(qi_all[:, sl[h]])
            k_st.append(ks_all[:, sl[h]])
    chains = range(n_b * N_HEADS)
    state = [state_ref[c] for c in chains]
    scores = [_dot_nt(q[c], k[c]) * decay_ref[c % N_HEADS] for c in chains]
    inter = [_dot(q_in[c], state[c]) for c in chains]
    o = [_dot(scores[c], v[c]) + inter[c] for c in chains]
    for c in chains:
        state_ref[c] = state[c] * cd_ref[c % N_HEADS] + _dot_tn(k_st[c], v[c])
    inv_dv = 1.0 / RET_DV
    for i in range(n_b):
        o_all = jnp.concatenate(o[i * N_HEADS:(i + 1) * N_HEADS], axis=1)
        xc = o_all - _dot(o_all, ones_ref[...]) * inv_dv
        var = _dot(xc * xc, ones_ref[...]) * inv_dv
        out_ref[i] = (xc * lax.rsqrt(var + EPS) * norm_ref[...]
                      * _silu(g_ref[i].astype(F32))).astype(out_ref.dtype)


def _retention(proj, ret_norm, b, t):
    ck = RET_CHUNK
    nc = t // ck
    width = N_HEADS * HEAD_DK
    half = HEAD_DK // 2
    proj3 = proj.reshape(b, t, proj.shape[-1])
    pos = jnp.arange(t, dtype=F32)
    inv_freq = jnp.power(ROPE_BASE, -jnp.arange(half, dtype=F32) / half)
    ang = pos[:, None] * inv_freq[None, :]
    cos, sin = jnp.cos(ang), jnp.sin(ang)
    cos_t = jnp.tile(jnp.concatenate([cos, cos], axis=1), (1, N_HEADS))
    sin_t = jnp.tile(jnp.concatenate([-sin, sin], axis=1), (1, N_HEADS))
    log_gamma = jnp.log1p(-jnp.exp2(-5.0 - jnp.arange(N_HEADS, dtype=F32)))
    idx = jnp.arange(ck, dtype=F32)
    causal = jnp.tril(jnp.ones((ck, ck), bool))
    decay = jnp.exp(jnp.where(causal, (idx[:, None] - idx[None, :]) * log_gamma[:, None, None], -jnp.inf))
    qs = jnp.repeat(jnp.exp((idx + 1.0) * log_gamma[:, None]).T, HEAD_DK, axis=1)
    ks = jnp.repeat(jnp.exp((ck - 1.0 - idx) * log_gamma[:, None]).T, HEAD_DK, axis=1)
    cd = jnp.broadcast_to(jnp.exp(ck * log_gamma)[:, None, None], (N_HEADS, 1, RET_DV))
    base = (N_HEADS * (2 * HEAD_DK + GDN_DV) + N_HEADS * GDN_DV) // width
    nb = _batch_per_step(b)
    col = lambda k: pl.BlockSpec((nb, ck, width), lambda i, c: (i, c, base + k))
    const2 = lambda shape: pl.BlockSpec(shape, lambda i, c: (0,) * len(shape))
    return pl.pallas_call(
        _ret_kernel,
        grid=(b // nb, nc),
        in_specs=[col(0), col(1), col(2), col(3),
                  pl.BlockSpec((ck, width), lambda i, c: (c, 0)),
                  pl.BlockSpec((ck, width), lambda i, c: (c, 0)),
                  const2((N_HEADS, ck, ck)), const2((ck, width)), const2((ck, width)),
                  const2((N_HEADS, 1, RET_DV)), const2((1, width)), const2((width, width))],
        out_specs=pl.BlockSpec((nb, ck, width), lambda i, c: (i, c, 0)),
        out_shape=jax.ShapeDtypeStruct((b, t, width), BF16),
        scratch_shapes=[pltpu.VMEM((nb * N_HEADS, HEAD_DK, RET_DV), F32)],
        compiler_params=_cparams(("parallel", "arbitrary")),
        name="retention_mixer",
    )(proj3, proj3, proj3, proj3, cos_t, sin_t, decay, qs, ks, cd, ret_norm.reshape(1, width).astype(F32),
      _head_ones(N_HEADS, RET_DV))


def _mlstm_kernel(q_ref, k_ref, v_ref, o_ref, gate_ref, gate_t_ref, brow_ref, bcol_ref, norm_ref,
                  ltri_ref, utri_ref, out_ref, state_ref, m_ref):
    @pl.when(pl.program_id(1) == 0)
    def _():
        state_ref[...] = jnp.zeros_like(state_ref)
        m_ref[...] = jnp.zeros_like(m_ref)

    n_b, ck = q_ref.shape[0], q_ref.shape[1]
    causal = _causal(ck)
    ones_col = (lax.broadcasted_iota(jnp.int32, (ck, MLSTM_DV), 1) == 0).astype(F32)
    sl = [slice(h * HEAD_DK, (h + 1) * HEAD_DK) for h in range(N_HEADS)]
    sv = [slice(h * MLSTM_DV, (h + 1) * MLSTM_DV) for h in range(N_HEADS)]

    q, k, vx, bc, br, ig_c, ig_r = [], [], [], [], [], [], []
    for i in range(n_b):
        q_all = q_ref[i].astype(F32)
        k_all = k_ref[i].astype(F32) * (HEAD_DK ** -0.5)
        v_all = v_ref[i].astype(F32)
        gl = gate_ref[i] + brow_ref[...]
        bcum_c = _dot_f32(ltri_ref[...], -_softplus(-gl))
        gt = gate_t_ref[i] + bcol_ref[...]
        bcum_r = _dot_f32(-_softplus(-gt), utri_ref[...])
        for h in range(N_HEADS):
            q.append(q_all[:, sl[h]])
            k.append(k_all[:, sl[h]])
            vx.append(jnp.concatenate([v_all[:, sv[h]], ones_col], axis=1))
            bc.append(bcum_c[:, 8 + h:9 + h])
            br.append(bcum_r[8 + h:9 + h, :])
            ig_c.append(gl[:, h:h + 1])
            ig_r.append(gt[h:h + 1, :])

    chains = range(n_b * N_HEADS)
    log_w = [jnp.where(causal, bc[c] - br[c] + ig_r[c], -jnp.inf) for c in chains]
    m_intra = [jnp.max(log_w[c], axis=-1, keepdims=True) for c in chains]
    b_end = [bc[c][ck - 1:ck, :] for c in chains]
    lw_end = [b_end[c] - bc[c] + ig_c[c] for c in chains]
    m_end = [jnp.max(lw_end[c], axis=0, keepdims=True) for c in chains]
    m_s = [m_ref[c][:, 0:1] for c in chains]
    cx = [state_ref[c] for c in chains]
    m_t = [jnp.maximum(bc[c] + m_s[c], m_intra[c]) for c in chains]
    inter = [jnp.exp(bc[c] + m_s[c] - m_t[c]) for c in chains]
    qk = [_dot_nt(q[c], k[c]) for c in chains]
    qc = [_dot(q[c], cx[c]) for c in chains]
    s = [qk[c] * jnp.exp(log_w[c] - m_t[c]) for c in chains]
    numx = [inter[c] * qc[c] + _dot(s[c], vx[c]) for c in chains]
    m_new = [jnp.maximum(b_end[c] + m_s[c], m_end[c]) for c in chains]
    wk = [k[c] * jnp.exp(lw_end[c] - m_new[c]) for c in chains]
    for c in chains:
        state_ref[c] = jnp.exp(b_end[c] + m_s[c] - m_new[c]) * cx[c] + _dot_tn(wk[c], vx[c])
        m_ref[c] = jnp.broadcast_to(m_new[c], (1, GATE_LANES))
    hh = [numx[c][:, :MLSTM_DV] / jnp.maximum(jnp.abs(numx[c][:, MLSTM_DV:MLSTM_DV + 1]), jnp.exp(-m_t[c]))
          for c in chains]
    outs = [_rms(hh[c], norm_ref[:, sv[c % N_HEADS]]) for c in chains]
    for i in range(n_b):
        out_ref[i] = (jnp.concatenate(outs[i * N_HEADS:(i + 1) * N_HEADS], axis=1)
                      * _sigmoid(o_ref[i].astype(F32))).astype(out_ref.dtype)


def _mlstm(proj, gates, gates_t, gate_bias, mlstm_norm, b, t):
    ck = MLSTM_CHUNK
    nc = t // ck
    n_qk = N_HEADS * HEAD_DK
    n_v = N_HEADS * MLSTM_DV
    proj3 = proj.reshape(b, t, proj.shape[-1])
    ltri, utri = _tri_consts(ck)
    nb = _batch_per_step(b)
    const2 = lambda shape: pl.BlockSpec(shape, lambda i, c: (0,) * len(shape))
    return pl.pallas_call(
        _mlstm_kernel,
        grid=(b // nb, nc),
        in_specs=[pl.BlockSpec((nb, ck, n_qk), lambda i, c: (i, c, 0)),
                  pl.BlockSpec((nb, ck, n_qk), lambda i, c: (i, c, 1)),
                  pl.BlockSpec((nb, ck, n_v), lambda i, c: (i, c, 2 * n_qk // n_v)),
                  pl.BlockSpec((nb, ck, n_v), lambda i, c: (i, c, 2 * n_qk // n_v + 1)),
                  pl.BlockSpec((nb, ck, GATE_LANES), lambda i, c: (i, c, 0)),
                  pl.BlockSpec((nb, None, 16, ck), lambda i, c: (i, c, 0, 0)),
                  const2((1, GATE_LANES)), const2((16, 1)), const2((1, n_v)),
                  const2((ck, ck)), const2((ck, ck))],
        out_specs=pl.BlockSpec((nb, ck, n_v), lambda i, c: (i, c, 0)),
        out_shape=jax.ShapeDtypeStruct((b, t, n_v), BF16),
        scratch_shapes=[pltpu.VMEM((nb * N_HEADS, HEAD_DK, 2 * MLSTM_DV), F32),
                        pltpu.VMEM((nb * N_HEADS, 1, GATE_LANES), F32)],
        compiler_params=_cparams(("parallel", "arbitrary")),
        name="mlstm_mixer",
    )(proj3, proj3, proj3, proj3, gates.reshape(b, t, GATE_LANES), gates_t,
      _lane_row(gate_bias, 0), _sub_col(gate_bias, 0), mlstm_norm.reshape(1, n_v).astype(F32), ltri, utri)


def _gates_transposed(gates, b, t, ck):
    return jnp.transpose(gates[:, :16].reshape(b, t // ck, ck, 16), (0, 1, 3, 2))


def _outproj_ffn_kernel(x_ref, ma_ref, mb_ref, wo_ref, nw_ref, w1_ref, w3_ref, w2_ref, out_ref,
                        x1_ref, h_ref, acc_ref):
    j = pl.program_id(1)
    n_a = ma_ref.shape[1]

    @pl.when(j == 0)
    def _():
        x1 = (x_ref[...] + jnp.dot(ma_ref[...], wo_ref[:n_a, :], preferred_element_type=F32)
              + jnp.dot(mb_ref[...], wo_ref[n_a:, :], preferred_element_type=F32))
        x1_ref[...] = x1
        h_ref[...] = _rms(x1, nw_ref[...]).astype(BF16)
        acc_ref[...] = jnp.zeros_like(acc_ref)

    h = h_ref[...]
    a = jnp.dot(h, w1_ref[...], preferred_element_type=F32)
    g = jnp.dot(h, w3_ref[...], preferred_element_type=F32)
    acc_ref[...] += jnp.dot((_silu(a) * g).astype(BF16), w2_ref[...], preferred_element_type=F32)

    @pl.when(j == pl.num_programs(1) - 1)
    def _():
        out_ref[...] = x1_ref[...] + acc_ref[...]


def _outproj_ffn(x2d, mix_a, mix_b, w_out, ffn_norm, w1, w3, w2, tm=512, n_ff_chunks=2):
    n, d = x2d.shape
    tm = min(tm, n)
    d_ff = w1.shape[1]
    tf = d_ff // n_ff_chunks
    n_a, n_b = mix_a.shape[1], mix_b.shape[1]
    return pl.pallas_call(
        _outproj_ffn_kernel,
        grid=(n // tm, n_ff_chunks),
        in_specs=[pl.BlockSpec((tm, d), lambda i, j: (i, 0)),
                  pl.BlockSpec((tm, n_a), lambda i, j: (i, 0)),
                  pl.BlockSpec((tm, n_b), lambda i, j: (i, 0)),
                  pl.BlockSpec((n_a + n_b, d), lambda i, j: (0, 0)),
                  pl.BlockSpec((1, d), lambda i, j: (0, 0)),
                  pl.BlockSpec((d, tf), lambda i, j: (0, j)),
                  pl.BlockSpec((d, tf), lambda i, j: (0, j)),
                  pl.BlockSpec((tf, d), lambda i, j: (j, 0))],
        out_specs=pl.BlockSpec((tm, d), lambda i, j: (i, 0)),
        out_shape=jax.ShapeDtypeStruct((n, d), F32),
        scratch_shapes=[pltpu.VMEM((tm, d), F32), pltpu.VMEM((tm, d), BF16), pltpu.VMEM((tm, d), F32)],
        compiler_params=_cparams(("parallel", "arbitrary")),
        name="outproj_ffn",
    )(x2d, mix_a, mix_b, w_out, ffn_norm.reshape(1, d), w1, w3, w2)


def _outproj_router_kernel(x_ref, m_ref, wo_ref, nw_ref, r_ref, x3_ref, hn_ref, idx_ref, gate_ref):
    x3 = x_ref[...] + jnp.dot(m_ref[...], wo_ref[...], preferred_element_type=F32)
    x3_ref[...] = x3
    hn = _rms(x3, nw_ref[...])
    hn_ref[...] = hn
    hn_hi = hn.astype(BF16)
    hn_lo = (hn - hn_hi.astype(F32)).astype(BF16)
    r_hi, r_lo = r_ref[:, :GATE_LANES], r_ref[:, GATE_LANES:]
    logits = (jnp.dot(hn_hi, r_hi, preferred_element_type=F32)
              + (jnp.dot(hn_hi, r_lo, preferred_element_type=F32)
                 + jnp.dot(hn_lo, r_hi, preferred_element_type=F32)))
    lane = lax.broadcasted_iota(jnp.int32, logits.shape, 1)
    logits = jnp.where(lane < N_EXPERTS, logits, -jnp.inf)
    m0 = jnp.max(logits, axis=-1, keepdims=True)
    i0 = jnp.min(jnp.where(logits == m0, lane, GATE_LANES), axis=-1, keepdims=True)
    rest = jnp.where(lane == i0, -jnp.inf, logits)
    m1 = jnp.max(rest, axis=-1, keepdims=True)
    i1 = jnp.min(jnp.where(rest == m1, lane, GATE_LANES), axis=-1, keepdims=True)
    e1 = jnp.exp(m1 - m0)
    inv = 1.0 / (1.0 + e1)
    idx_ref[...] = jnp.where(lane == 0, i0, jnp.where(lane == 1, i1, 0))
    gate_ref[...] = jnp.where(lane == 0, inv, jnp.where(lane == 1, e1 * inv, 0.0))


def _outproj_router(x2d, mix, w_out, ffn_norm, router, tm=512):
    n, d = x2d.shape
    tm = min(tm, n)
    n_m = mix.shape[1]
    router_pad = jnp.zeros((d, GATE_LANES), F32).at[:, :N_EXPERTS].set(router.astype(F32))
    router_hi = router_pad.astype(BF16)
    router_lo = (router_pad - router_hi.astype(F32)).astype(BF16)
    router_split = jnp.concatenate([router_hi, router_lo], axis=1)
    row = lambda w: pl.BlockSpec((tm, w), lambda i: (i, 0))
    return pl.pallas_call(
        _outproj_router_kernel,
        grid=(n // tm,),
        in_specs=[row(d), row(n_m),
                  pl.BlockSpec((n_m, d), lambda i: (0, 0)),
                  pl.BlockSpec((1, d), lambda i: (0, 0)),
                  pl.BlockSpec((d, 2 * GATE_LANES), lambda i: (0, 0))],
        out_specs=[row(d), row(d), row(GATE_LANES), row(GATE_LANES)],
        out_shape=[jax.ShapeDtypeStruct((n, d), F32), jax.ShapeDtypeStruct((n, d), F32),
                   jax.ShapeDtypeStruct((n, GATE_LANES), jnp.int32),
                   jax.ShapeDtypeStruct((n, GATE_LANES), F32)],
        compiler_params=_cparams(("parallel",)),
        name="outproj_router",
    )(x2d, mix, w_out, ffn_norm.reshape(1, d), router_split)


def _rank_kernel(idx_ref, tri_ref, rank_ref, count_ref, carry_ref):
    @pl.when(pl.program_id(0) == 0)
    def _():
        carry_ref[...] = jnp.zeros_like(carry_ref)

    idx = idx_ref[...]
    lane = lax.broadcasted_iota(jnp.int32, idx.shape, 1)
    e0, e1 = idx[:, 0:1], idx[:, 1:2]
    member = ((lane == e0) | (lane == e1)).astype(F32)
    before = _dot(tri_ref[...], member) + carry_ref[...]
    r0 = jnp.sum(jnp.where(lane == e0, before, 0.0), axis=-1, keepdims=True)
    r1 = jnp.sum(jnp.where(lane == e1, before, 0.0), axis=-1, keepdims=True)
    rank_ref[...] = jnp.where(lane == 0, r0, jnp.where(lane == 1, r1, 0.0)).astype(jnp.int32)
    carry_ref[...] += jnp.sum(member, axis=0, keepdims=True)
    count_ref[...] = carry_ref[...].astype(jnp.int32)


def _expert_ranks(ridx, tm=512):
    n = ridx.shape[0]
    tm = min(tm, n)
    i = np.arange(tm)
    tri = jnp.asarray((i[None, :] < i[:, None]).astype(np.float32)).astype(BF16)
    return pl.pallas_call(
        _rank_kernel,
        grid=(n // tm,),
        in_specs=[pl.BlockSpec((tm, GATE_LANES), lambda i: (i, 0)),
                  pl.BlockSpec((tm, tm), lambda i: (0, 0))],
        out_specs=[pl.BlockSpec((tm, GATE_LANES), lambda i: (i, 0)),
                   pl.BlockSpec((1, GATE_LANES), lambda i: (0, 0))],
        out_shape=[jax.ShapeDtypeStruct((n, GATE_LANES), jnp.int32),
                   jax.ShapeDtypeStruct((1, GATE_LANES), jnp.int32)],
        scratch_shapes=[pltpu.VMEM((1, GATE_LANES), F32)],
        compiler_params=_cparams(("arbitrary",)),
        name="expert_ranks",
    )(ridx, tri)


def _row_copy(src_ref, src_row, dst_ref, dst_row, sem):
    return pltpu.make_async_copy(src_ref.at[pl.ds(src_row, 1)], dst_ref.at[pl.ds(dst_row, 1)], sem)


def _dispatch_kernel(dest_ref, hn_ref, slots_in_ref, slots_ref, sems, *, tm, n_tok):
    del slots_in_ref
    i = pl.program_id(0)
    base = i * tm
    slot = i % 2

    def start(r, carry):
        _row_copy(hn_ref, base + r, slots_ref, dest_ref[base + r], sems.at[slot]).start()
        _row_copy(hn_ref, base + r, slots_ref, dest_ref[n_tok + base + r], sems.at[slot]).start(priority=1)
        return carry

    def drain(s):
        for _ in range(2):
            pltpu.make_async_copy(hn_ref.at[pl.ds(0, tm)], slots_ref.at[pl.ds(0, tm)], sems.at[s]).wait()

    lax.fori_loop(0, tm, start, 0, unroll=DMA_UNROLL)

    @pl.when(i > 0)
    def _():
        drain(1 - slot)

    @pl.when(i == pl.num_programs(0) - 1)
    def _():
        drain(slot)


def _dispatch(dest_flat, hn, n_slots, tm=256):
    n, d = hn.shape
    tm = min(tm, n)
    grid_spec = pltpu.PrefetchScalarGridSpec(
        num_scalar_prefetch=1,
        grid=(n // tm,),
        in_specs=[pl.BlockSpec(memory_space=pl.ANY),
                  pl.BlockSpec(memory_space=pl.ANY)],
        out_specs=pl.BlockSpec(memory_space=pl.ANY),
        scratch_shapes=[pltpu.SemaphoreType.DMA((2,))],
    )
    return pl.pallas_call(
        functools.partial(_dispatch_kernel, tm=tm, n_tok=n),
        grid_spec=grid_spec,
        out_shape=jax.ShapeDtypeStruct((n_slots, d), hn.dtype),
        input_output_aliases={2: 0},
        compiler_params=_cparams(("arbitrary",)),
        name="row_dispatch",
    )(dest_flat, hn, jnp.zeros((n_slots, d), hn.dtype))


def _moe_kernel(be_ref, nused_ref, x_ref, w1_ref, w3_ref, w2_ref, out_ref):
    i, j = pl.program_id(0), pl.program_id(1)

    @pl.when(i < nused_ref[0])
    def _():
        x = x_ref[...].astype(BF16)
        a = jnp.dot(x, w1_ref[...], preferred_element_type=F32)
        g = jnp.dot(x, w3_ref[...], preferred_element_type=F32)
        y = jnp.dot((_silu(a) * g).astype(BF16), w2_ref[...], preferred_element_type=F32)

        @pl.when(j == 0)
        def _():
            out_ref[...] = y

        @pl.when(j > 0)
        def _():
            out_ref[...] += y

    @pl.when(i >= nused_ref[0])
    def _():
        out_ref[...] = jnp.zeros_like(out_ref)


def _moe_ffn(block_expert, n_used, slots, w1, w3, w2, n_ff_chunks=2):
    n_slots, d = slots.shape
    n_blocks = n_slots // MOE_BLOCK
    d_ff = w1.shape[2]
    tf = d_ff // n_ff_chunks
    last_j = n_ff_chunks - 1

    def blk(i, nused):
        return jnp.minimum(i, nused[0] - 1)

    def chunk(i, j, nused):
        return jnp.where(i < nused[0], j, last_j)

    grid_spec = pltpu.PrefetchScalarGridSpec(
        num_scalar_prefetch=2,
        grid=(n_blocks, n_ff_chunks),
        in_specs=[pl.BlockSpec((MOE_BLOCK, d), lambda i, j, be, nu: (blk(i, nu), 0)),
                  pl.BlockSpec((None, d, tf), lambda i, j, be, nu: (be[blk(i, nu)], 0, chunk(i, j, nu))),
                  pl.BlockSpec((None, d, tf), lambda i, j, be, nu: (be[blk(i, nu)], 0, chunk(i, j, nu))),
                  pl.BlockSpec((None, tf, d), lambda i, j, be, nu: (be[blk(i, nu)], chunk(i, j, nu), 0))],
        out_specs=pl.BlockSpec((MOE_BLOCK, d), lambda i, j, be, nu: (i, 0)),
    )
    return pl.pallas_call(
        _moe_kernel,
        grid_spec=grid_spec,
        out_shape=jax.ShapeDtypeStruct((n_slots, d), F32),
        compiler_params=_cparams(("arbitrary", "arbitrary")),
        name="moe_ffn",
    )(block_expert, n_used, slots, w1, w3, w2)


def _combine_kernel(dest_ref, x_ref, gate_ref, nw_ref, y_ref, out_ref, ybuf_ref, sems, *, tm, n_tok):
    i = pl.program_id(0)
    slot = i % 2

    def gather(tile, s):
        base = tile * tm

        def start(r, carry):
            _row_copy(y_ref, dest_ref[base + r], ybuf_ref.at[s, 0], r, sems.at[s]).start()
            _row_copy(y_ref, dest_ref[n_tok + base + r], ybuf_ref.at[s, 1], r, sems.at[s]).start(priority=1)
            return carry

        lax.fori_loop(0, tm, start, 0, unroll=DMA_UNROLL)

    @pl.when(i == 0)
    def _():
        gather(0, 0)

    @pl.when(i + 1 < pl.num_programs(0))
    def _():
        gather(i + 1, 1 - slot)

    for k in range(2):
        pltpu.make_async_copy(y_ref.at[pl.ds(0, tm)], ybuf_ref.at[slot, k], sems.at[slot]).wait()
    gate = gate_ref[...]
    y = x_ref[...] + (ybuf_ref[slot, 0] * gate[:, 0:1] + ybuf_ref[slot, 1] * gate[:, 1:2])
    out_ref[...] = _rms(y, nw_ref[...])


def _combine(dest_flat, x2d, rgate, final_norm, yb, tm=256):
    n, d = x2d.shape
    tm = min(tm, n)
    grid_spec = pltpu.PrefetchScalarGridSpec(
        num_scalar_prefetch=1,
        grid=(n // tm,),
        in_specs=[pl.BlockSpec((tm, d), lambda i, dest: (i, 0)),
                  pl.BlockSpec((tm, GATE_LANES), lambda i, dest: (i, 0)),
                  pl.BlockSpec((1, d), lambda i, dest: (0, 0)),
                  pl.BlockSpec(memory_space=pl.ANY)],
        out_specs=pl.BlockSpec((tm, d), lambda i, dest: (i, 0)),
        scratch_shapes=[pltpu.VMEM((2, 2, tm, d), F32), pltpu.SemaphoreType.DMA((2,))],
    )
    return pl.pallas_call(
        functools.partial(_combine_kernel, tm=tm, n_tok=n),
        grid_spec=grid_spec,
        out_shape=jax.ShapeDtypeStruct((n, d), F32),
        compiler_params=_cparams(("arbitrary",)),
        name="combine_norm",
    )(dest_flat, x2d, rgate, final_norm.reshape(1, d), yb)


def _split_in_proj(w_in, lo, hi):
    main = jnp.concatenate([w_in[:, :lo], w_in[:, hi:]], axis=1)
    gate = jnp.pad(w_in[:, lo:hi], ((0, 0), (0, GATE_LANES - (hi - lo))))
    return jnp.concatenate([main, gate], axis=1).astype(BF16), main.shape[1]


def kernel(x, mix_norm_0, w_in_0, conv_w_0, a_log_0, dt_bias_0, gdn_norm_0, ret_norm_0, w_out_0, ffn_norm_0,
           ffn_w1_0, ffn_w3_0, ffn_w2_0, mix_norm_1, w_in_1, gate_bias_1, mlstm_norm_1, w_out_1, ffn_norm_1,
           router_1, exp_w1_1, exp_w3_1, exp_w2_1, final_norm):
    b, t, d = x.shape
    n = b * t
    x2d = x.reshape(n, d)

    gdn_qkv = N_HEADS * (2 * HEAD_DK + GDN_DV)
    w0, n_main0 = _split_in_proj(w_in_0, gdn_qkv, gdn_qkv + 2 * N_HEADS)
    proj0, gates0 = _norm_proj(x2d, mix_norm_0, w0, n_main0)
    o_gdn = _gdn(proj0, gates0, _gates_transposed(gates0, b, t, CHUNK), conv_w_0, a_log_0, dt_bias_0, gdn_norm_0, b, t)
    o_ret = _retention(proj0, ret_norm_0, b, t)
    x2 = _outproj_ffn(x2d, o_gdn.reshape(n, -1), o_ret.reshape(n, -1), w_out_0.astype(BF16), ffn_norm_0,
                      ffn_w1_0.astype(BF16), ffn_w3_0.astype(BF16), ffn_w2_0.astype(BF16))

    n_main1 = w_in_1.shape[1] - 2 * N_HEADS
    w1, _ = _split_in_proj(w_in_1, n_main1, n_main1 + 2 * N_HEADS)
    proj1, gates1 = _norm_proj(x2, mix_norm_1, w1, n_main1)
    h_mix = _mlstm(proj1, gates1, _gates_transposed(gates1, b, t, MLSTM_CHUNK), gate_bias_1, mlstm_norm_1, b, t)
    x3, hn, ridx, rgate = _outproj_router(x2, h_mix.reshape(n, -1), w_out_1.astype(BF16), ffn_norm_1, router_1)

    rank, counts = _expert_ranks(ridx)
    counts = counts[0, :N_EXPERTS]
    padded = (counts + MOE_BLOCK - 1) // MOE_BLOCK * MOE_BLOCK
    pad_end = jnp.cumsum(padded)
    pad_start = pad_end - padded
    dest_flat = jnp.concatenate([pad_start[ridx[:, 0]] + rank[:, 0], pad_start[ridx[:, 1]] + rank[:, 1]])
    n_blocks = -(-(2 * n) // MOE_BLOCK) + N_EXPERTS
    block_row0 = jnp.arange(n_blocks, dtype=jnp.int32) * MOE_BLOCK
    block_expert = jnp.minimum(jnp.sum((pad_end[None, :] <= block_row0[:, None]).astype(jnp.int32), axis=1),
                               N_EXPERTS - 1)
    n_used = (pad_end[-1:] // MOE_BLOCK).astype(jnp.int32)

    slots = _dispatch(dest_flat, hn, n_blocks * MOE_BLOCK)
    yb = _moe_ffn(block_expert, n_used, slots, exp_w1_1.astype(BF16), exp_w3_1.astype(BF16),
                  exp_w2_1.astype(BF16))
    out = _combine(dest_flat, x3, rgate, final_norm, yb)
    return out.reshape(b, t, d)
```

```python
import functools
import math

import numpy as np
import jax
import jax.numpy as jnp
from jax import lax
from jax.experimental import pallas as pl
from jax.experimental.pallas import tpu as pltpu

F32 = jnp.float32
BF16 = jnp.bfloat16
HIGHEST = lax.Precision.HIGHEST

EPS = 1e-6
CHUNK = 128
RET_CHUNK = 128
MLSTM_CHUNK = 128
CONV_WIDTH = 4
N_HEADS = 8
HEAD_DK = 64
GDN_DV = 64
RET_DV = 64
MLSTM_DV = 128
ROPE_BASE = 10000.0
N_EXPERTS = 8
MOE_BLOCK = 512
GATE_LANES = 128
VMEM_LIMIT = 52 * 1024 * 1024
DMA_UNROLL = 8


def _cparams(sem):
    return pltpu.CompilerParams(dimension_semantics=sem, vmem_limit_bytes=VMEM_LIMIT)


def _dot(a, b):
    return jnp.dot(a.astype(BF16), b.astype(BF16), preferred_element_type=F32)


def _dot_nt(a, b):
    return lax.dot_general(a.astype(BF16), b.astype(BF16), (((1,), (1,)), ((), ())),
                           preferred_element_type=F32)


def _dot_tn(a, b):
    return lax.dot_general(a.astype(BF16), b.astype(BF16), (((0,), (0,)), ((), ())),
                           preferred_element_type=F32)


def _dot_f32(a, b):
    return jnp.dot(a, b, precision=HIGHEST, preferred_element_type=F32)


def _rms(x, w):
    return x * lax.rsqrt(jnp.mean(x * x, axis=-1, keepdims=True) + EPS) * w


def _silu(x):
    return x * (1.0 / (1.0 + jnp.exp(-x)))


def _sigmoid(x):
    return 1.0 / (1.0 + jnp.exp(-x))


def _softplus(x):
    return jnp.maximum(x, 0.0) + jnp.log1p(jnp.exp(-jnp.abs(x)))


def _causal(n, strict=False):
    r = lax.broadcasted_iota(jnp.int32, (n, n), 0)
    c = lax.broadcasted_iota(jnp.int32, (n, n), 1)
    return (r > c) if strict else (r >= c)


def _norm_proj_kernel(x_ref, nw_ref, w_ref, main_ref, gate_ref, *, n_main, col_chunk):
    h = _rms(x_ref[...], nw_ref[...]).astype(BF16)
    for c0 in range(0, n_main, col_chunk):
        main_ref[:, c0:c0 + col_chunk] = jnp.dot(
            h, w_ref[:, c0:c0 + col_chunk], preferred_element_type=F32).astype(BF16)
    gate_ref[...] = jnp.dot(h, w_ref[:, n_main:], preferred_element_type=F32)


def _norm_proj(x2d, norm_w, w_cat, n_main, tm=512, col_chunk=512):
    n, d = x2d.shape
    tm = min(tm, n)
    n_cat = w_cat.shape[1]
    return pl.pallas_call(
        functools.partial(_norm_proj_kernel, n_main=n_main, col_chunk=col_chunk),
        grid=(n // tm,),
        in_specs=[pl.BlockSpec((tm, d), lambda i: (i, 0)),
                  pl.BlockSpec((1, d), lambda i: (0, 0)),
                  pl.BlockSpec((d, n_cat), lambda i: (0, 0))],
        out_specs=[pl.BlockSpec((tm, n_main), lambda i: (i, 0)),
                   pl.BlockSpec((tm, GATE_LANES), lambda i: (i, 0))],
        out_shape=[jax.ShapeDtypeStruct((n, n_main), BF16),
                   jax.ShapeDtypeStruct((n, GATE_LANES), F32)],
        compiler_params=_cparams(("parallel",)),
        name="norm_proj",
    )(x2d, norm_w.reshape(1, d), w_cat)


def _gdn_kernel(qkv_ref, z_ref, gate_ref, gate_t_ref, conv_ref, arow_ref, dtrow_ref, acol_ref, dtcol_ref,
                norm_ref, ltri_ref, utri_ref, lvl_ref, ones_ref, out_ref, state_ref, xbuf_ref):
    n_b = qkv_ref.shape[0]
    n_qk = N_HEADS * HEAD_DK

    @pl.when(pl.program_id(1) == 0)
    def _():
        state_ref[...] = jnp.zeros_like(state_ref)
        xbuf_ref[:, 0:8, :] = jnp.zeros((n_b, 8, xbuf_ref.shape[2]), F32)

    causal = _causal(CHUNK)
    strict = _causal(CHUNK, strict=True)
    eye = (lax.broadcasted_iota(jnp.int32, (CHUNK, CHUNK), 0)
           == lax.broadcasted_iota(jnp.int32, (CHUNK, CHUNK), 1)).astype(F32)
    sl = [slice(h * HEAD_DK, (h + 1) * HEAD_DK) for h in range(N_HEADS)]

    q, k, v, bc, gcc, gcr = [], [], [], [], [], []
    for i in range(n_b):
        xbuf_ref[i, 8:8 + CHUNK, :] = qkv_ref[i].astype(F32)
        pre = None
        for j in range(CONV_WIDTH):
            term = xbuf_ref[i, pl.ds(8 - (CONV_WIDTH - 1) + j, CHUNK), :] * conv_ref[j:j + 1, :]
            pre = term if pre is None else pre + term
        xbuf_ref[i, 0:8, :] = xbuf_ref[i, CHUNK:CHUNK + 8, :]
        act = _silu(pre)
        q_all, k_all, v_all = act[:, :n_qk], act[:, n_qk:2 * n_qk], act[:, 2 * n_qk:]
        q_all = q_all * lax.rsqrt(_dot(q_all * q_all, ones_ref[...]) + EPS) * (HEAD_DK ** -0.5)
        k_all = k_all * lax.rsqrt(_dot(k_all * k_all, ones_ref[...]) + EPS)
        gl = gate_ref[i]
        beta_c = _sigmoid(gl)
        g_c = -jnp.exp(arow_ref[...]) * _softplus(gl + dtrow_ref[...])
        gc_c = _dot_f32(ltri_ref[...], g_c)
        g_r = -jnp.exp(acol_ref[...]) * _softplus(gate_t_ref[i] + dtcol_ref[...])
        gc_r = _dot_f32(g_r, utri_ref[...])
        for h in range(N_HEADS):
            q.append(q_all[:, sl[h]])
            k.append(k_all[:, sl[h]])
            v.append(v_all[:, sl[h]])
            bc.append(beta_c[:, h:h + 1])
            gcc.append(gc_c[:, 8 + h:9 + h])
            gcr.append(gc_r[8 + h:9 + h, :])

    chains = range(n_b * N_HEADS)
    decay = [jnp.exp(jnp.where(causal, gcc[c] - gcr[c], -jnp.inf)) for c in chains]
    kb = [k[c] * bc[c] for c in chains]
    scores = [_dot_nt(jnp.concatenate([kb[c], q[c]], axis=0), k[c]) for c in chains]
    lmat = [jnp.where(strict, scores[c][:CHUNK] * decay[c], 0.0) for c in chains]
    qk = [scores[c][CHUNK:] * decay[c] for c in chains]
    tinv = [eye - lmat[c] * lvl_ref[0].astype(F32) for c in chains]
    lmat_mxu = [lmat[c].astype(BF16) for c in chains]
    for lvl in range(1, lvl_ref.shape[0]):
        inner = [_dot(lmat_mxu[c] * lvl_ref[lvl], tinv[c]) for c in chains]
        tinv = [tinv[c] - _dot(tinv[c], inner[c]) for c in chains]
    egc = [jnp.exp(gcc[c]) for c in chains]
    uw = [_dot(tinv[c], jnp.concatenate([v[c] * bc[c], kb[c] * egc[c]], axis=1)) for c in chains]
    state = [state_ref[c] for c in chains]
    v_new = [uw[c][:, :GDN_DV] - _dot(uw[c][:, GDN_DV:], state[c]) for c in chains]
    o = [_dot(q[c] * egc[c], state[c]) + _dot(qk[c], v_new[c]) for c in chains]
    g_last = [gcc[c][CHUNK - 1:CHUNK, :] for c in chains]
    k_dec = [k[c] * jnp.exp(g_last[c] - gcc[c]) for c in chains]
    for c in chains:
        state_ref[c] = state[c] * jnp.exp(g_last[c]) + _dot_tn(k_dec[c], v_new[c])
    for i in range(n_b):
        o_all = jnp.concatenate(o[i * N_HEADS:(i + 1) * N_HEADS], axis=1)
        mean_sq = _dot(o_all * o_all, ones_ref[...]) * (1.0 / GDN_DV)
        out_ref[i] = (o_all * lax.rsqrt(mean_sq + EPS) * norm_ref[...]
                      * _silu(z_ref[i].astype(F32))).astype(out_ref.dtype)


def _level_masks():
    i = np.arange(CHUNK)[:, None]
    j = np.arange(CHUNK)[None, :]
    masks = []
    for lvl in range(CHUNK.bit_length() - 1):
        same_parent = (i >> (lvl + 1)) == (j >> (lvl + 1))
        masks.append((same_parent & (((i >> lvl) & 1) == 1) & (((j >> lvl) & 1) == 0)).astype(np.float32))
    return jnp.asarray(np.stack(masks)).astype(BF16)


def _tri_consts(ck):
    i = np.arange(ck)[:, None]
    j = np.arange(ck)[None, :]
    return jnp.asarray((j <= i).astype(np.float32)), jnp.asarray((i <= j).astype(np.float32))


def _lane_row(vals, offset):
    return jnp.zeros((1, GATE_LANES), F32).at[0, offset:offset + vals.shape[0]].set(vals.astype(F32))


def _sub_col(vals, offset):
    return jnp.zeros((16, 1), F32).at[offset:offset + vals.shape[0], 0].set(vals.astype(F32))


def _head_ones(n_heads, width):
    g = np.arange(n_heads * width) // width
    return jnp.asarray((g[:, None] == g[None, :]).astype(np.float32)).astype(BF16)


def _batch_per_step(b):
    return 2 if b % 2 == 0 else 1


def _gdn(proj, gates, gates_t, conv_w, a_log, dt_bias, gdn_norm, b, t):
    nc = t // CHUNK
    nb = _batch_per_step(b)
    n_qkv = N_HEADS * (2 * HEAD_DK + GDN_DV)
    n_z = N_HEADS * GDN_DV
    proj3 = proj.reshape(b, t, proj.shape[-1])
    ltri, utri = _tri_consts(CHUNK)
    const2 = lambda shape: pl.BlockSpec(shape, lambda i, c: (0,) * len(shape))
    return pl.pallas_call(
        _gdn_kernel,
        grid=(b // nb, nc),
        in_specs=[pl.BlockSpec((nb, CHUNK, n_qkv), lambda i, c: (i, c, 0)),
                  pl.BlockSpec((nb, CHUNK, n_z), lambda i, c: (i, c, n_qkv // n_z)),
                  pl.BlockSpec((nb, CHUNK, GATE_LANES), lambda i, c: (i, c, 0)),
                  pl.BlockSpec((nb, None, 16, CHUNK), lambda i, c: (i, c, 0, 0)),
                  const2((CONV_WIDTH, n_qkv)), const2((1, GATE_LANES)), const2((1, GATE_LANES)),
                  const2((16, 1)), const2((16, 1)), const2((1, n_z)),
                  const2((CHUNK, CHUNK)), const2((CHUNK, CHUNK)),
                  const2((CHUNK.bit_length() - 1, CHUNK, CHUNK)),
                  const2((n_z, n_z))],
        out_specs=pl.BlockSpec((nb, CHUNK, n_z), lambda i, c: (i, c, 0)),
        out_shape=jax.ShapeDtypeStruct((b, t, n_z), BF16),
        scratch_shapes=[pltpu.VMEM((nb * N_HEADS, HEAD_DK, GDN_DV), F32),
                        pltpu.VMEM((nb, CHUNK + 8, n_qkv), F32)],
        compiler_params=_cparams(("parallel", "arbitrary")),
        name="gdn_mixer",
    )(proj3, proj3, gates.reshape(b, t, GATE_LANES), gates_t, conv_w.astype(F32),
      _lane_row(a_log, 8), _lane_row(dt_bias, 8), _sub_col(a_log, 8), _sub_col(dt_bias, 8),
      jnp.tile(gdn_norm.astype(F32), N_HEADS).reshape(1, n_z), ltri, utri, _level_masks(),
      _head_ones(N_HEADS, HEAD_DK))


def _ret_kernel(q_ref, k_ref, v_ref, g_ref, cos_ref, sin_ref, decay_ref, qs_ref, ks_ref, cd_ref, norm_ref,
                ones_ref, out_ref, state_ref):
    @pl.when(pl.program_id(1) == 0)
    def _():
        state_ref[...] = jnp.zeros_like(state_ref)

    width = N_HEADS * HEAD_DK
    half = HEAD_DK // 2
    first_half = (lax.broadcasted_iota(jnp.int32, (q_ref.shape[1], width), 1) % HEAD_DK) < half

    def rot(t_in):
        swapped = jnp.where(first_half, pltpu.roll(t_in, width - half, 1), pltpu.roll(t_in, half, 1))
        return t_in * cos_ref[...] + swapped * sin_ref[...]

    n_b = q_ref.shape[0]
    sl = [slice(h * HEAD_DK, (h + 1) * HEAD_DK) for h in range(N_HEADS)]
    q, k, v, q_in, k_st = [], [], [], [], []
    for i in range(n_b):
        q_all = rot(q_ref[i].astype(F32))
        k_all = rot(k_ref[i].astype(F32)) * (HEAD_DK ** -0.5)
        v_all = v_ref[i].astype(F32)
        qi_all = q_all * qs_ref[...]
        ks_all = k_all * ks_ref[...]
        for h in range(N_HEADS):
            q.append(q_all[:, sl[h]])
            k.append(k_all[:, sl[h]])
            v.append(v_all[:, sl[h]])
            q_in.append(qi_all[:, sl[h]])
            k_st.append(ks_all[:, sl[h]])
    chains = range(n_b * N_HEADS)
    state = [state_ref[c] for c in chains]
    scores = [_dot_nt(q[c], k[c]) * decay_ref[c % N_HEADS] for c in chains]
    inter = [_dot(q_in[c], state[c]) for c in chains]
    o = [_dot(scores[c], v[c]) + inter[c] for c in chains]
    for c in chains:
        state_ref[c] = state[c] * cd_ref[c % N_HEADS] + _dot_tn(k_st[c], v[c])
    inv_dv = 1.0 / RET_DV
    for i in range(n_b):
        o_all = jnp.concatenate(o[i * N_HEADS:(i + 1) * N_HEADS], axis=1)
        xc = o_all - _dot(o_all, ones_ref[...]) * inv_dv
        var = _dot(xc * xc, ones_ref[...]) * inv_dv
        out_ref[i] = (xc * lax.rsqrt(var + EPS) * norm_ref[...]
                      * _silu(g_ref[i].astype(F32))).astype(out_ref.dtype)


def _retention(proj, ret_norm, b, t):
    ck = RET_CHUNK
    nc = t // ck
    width = N_HEADS * HEAD_DK
    half = HEAD_DK // 2
    proj3 = proj.reshape(b, t, proj.shape[-1])
    pos = jnp.arange(t, dtype=F32)
    inv_freq = jnp.power(ROPE_BASE, -jnp.arange(half, dtype=F32) / half)
    ang = pos[:, None] * inv_freq[None, :]
    cos, sin = jnp.cos(ang), jnp.sin(ang)
    cos_t = jnp.tile(jnp.concatenate([cos, cos], axis=1), (1, N_HEADS))
    sin_t = jnp.tile(jnp.concatenate([-sin, sin], axis=1), (1, N_HEADS))
    log_gamma = jnp.log1p(-jnp.exp2(-5.0 - jnp.arange(N_HEADS, dtype=F32)))
    idx = jnp.arange(ck, dtype=F32)
    causal = jnp.tril(jnp.ones((ck, ck), bool))
    decay = jnp.exp(jnp.where(causal, (idx[:, None] - idx[None, :]) * log_gamma[:, None, None], -jnp.inf))
    qs = jnp.repeat(jnp.exp((idx + 1.0) * log_gamma[:, None]).T, HEAD_DK, axis=1)
    ks = jnp.repeat(jnp.exp((ck - 1.0 - idx) * log_gamma[:, None]).T, HEAD_DK, axis=1)
    cd = jnp.broadcast_to(jnp.exp(ck * log_gamma)[:, None, None], (N_HEADS, 1, RET_DV))
    base = (N_HEADS * (2 * HEAD_DK + GDN_DV) + N_HEADS * GDN_DV) // width
    nb = _batch_per_step(b)
    col = lambda k: pl.BlockSpec((nb, ck, width), lambda i, c: (i, c, base + k))
    const2 = lambda shape: pl.BlockSpec(shape, lambda i, c: (0,) * len(shape))
    return pl.pallas_call(
        _ret_kernel,
        grid=(b // nb, nc),
        in_specs=[col(0), col(1), col(2), col(3),
                  pl.BlockSpec((ck, width), lambda i, c: (c, 0)),
                  pl.BlockSpec((ck, width), lambda i, c: (c, 0)),
                  const2((N_HEADS, ck, ck)), const2((ck, width)), const2((ck, width)),
                  const2((N_HEADS, 1, RET_DV)), const2((1, width)), const2((width, width))],
        out_specs=pl.BlockSpec((nb, ck, width), lambda i, c: (i, c, 0)),
        out_shape=jax.ShapeDtypeStruct((b, t, width), BF16),
        scratch_shapes=[pltpu.VMEM((nb * N_HEADS, HEAD_DK, RET_DV), F32)],
        compiler_params=_cparams(("parallel", "arbitrary")),
        name="retention_mixer",
    )(proj3, proj3, proj3, proj3, cos_t, sin_t, decay, qs, ks, cd, ret_norm.reshape(1, width).astype(F32),
      _head_ones(N_HEADS, RET_DV))


def _mlstm_kernel(q_ref, k_ref, v_ref, o_ref, gate_ref, gate_t_ref, brow_ref, bcol_ref, norm_ref,
                  ltri_ref, utri_ref, out_ref, state_ref, m_ref):
    @pl.when(pl.program_id(1) == 0)
    def _():
        state_ref[...] = jnp.zeros_like(state_ref)
        m_ref[...] = jnp.zeros_like(m_ref)

    n_b, ck = q_ref.shape[0], q_ref.shape[1]
    causal = _causal(ck)
    ones_col = (lax.broadcasted_iota(jnp.int32, (ck, MLSTM_DV), 1) == 0).astype(F32)
    sl = [slice(h * HEAD_DK, (h + 1) * HEAD_DK) for h in range(N_HEADS)]
    sv = [slice(h * MLSTM_DV, (h + 1) * MLSTM_DV) for h in range(N_HEADS)]

    q, k, vx, bc, br, ig_c, ig_r = [], [], [], [], [], [], []
    for i in range(n_b):
        q_all = q_ref[i].astype(F32)
        k_all = k_ref[i].astype(F32) * (HEAD_DK ** -0.5)
        v_all = v_ref[i].astype(F32)
        gl = gate_ref[i] + brow_ref[...]
        bcum_c = _dot_f32(ltri_ref[...], -_softplus(-gl))
        gt = gate_t_ref[i] + bcol_ref[...]
        bcum_r = _dot_f32(-_softplus(-gt), utri_ref[...])
        for h in range(N_HEADS):
            q.append(q_all[:, sl[h]])
            k.append(k_all[:, sl[h]])
            vx.append(jnp.concatenate([v_all[:, sv[h]], ones_col], axis=1))
            bc.append(bcum_c[:, 8 + h:9 + h])
            br.append(bcum_r[8 + h:9 + h, :])
            ig_c.append(gl[:, h:h + 1])
            ig_r.append(gt[h:h + 1, :])

    chains = range(n_b * N_HEADS)
    log_w = [jnp.where(causal, bc[c] - br[c] + ig_r[c], -jnp.inf) for c in chains]
    m_intra = [jnp.max(log_w[c], axis=-1, keepdims=True) for c in chains]
    b_end = [bc[c][ck - 1:ck, :] for c in chains]
    lw_end = [b_end[c] - bc[c] + ig_c[c] for c in chains]
    m_end = [jnp.max(lw_end[c], axis=0, keepdims=True) for c in chains]
    m_s = [m_ref[c][:, 0:1] for c in chains]
    cx = [state_ref[c] for c in chains]
    m_t = [jnp.maximum(bc[c] + m_s[c], m_intra[c]) for c in chains]
    inter = [jnp.exp(bc[c] + m_s[c] - m_t[c]) for c in chains]
    qk = [_dot_nt(q[c], k[c]) for c in chains]
    qc = [_dot(q[c], cx[c]) for c in chains]
    s = [qk[c] * jnp.exp(log_w[c] - m_t[c]) for c in chains]
    numx = [inter[c] * qc[c] + _dot(s[c], vx[c]) for c in chains]
    m_new = [jnp.maximum(b_end[c] + m_s[c], m_end[c]) for c in chains]
    wk = [k[c] * jnp.exp(lw_end[c] - m_new[c]) for c in chains]
    for c in chains:
        state_ref[c] = jnp.exp(b_end[c] + m_s[c] - m_new[c]) * cx[c] + _dot_tn(wk[c], vx[c])
        m_ref[c] = jnp.broadcast_to(m_new[c], (1, GATE_LANES))
    hh = [numx[c][:, :MLSTM_DV] / jnp.maximum(jnp.abs(numx[c][:, MLSTM_DV:MLSTM_DV + 1]), jnp.exp(-m_t[c]))
          for c in chains]
    outs = [_rms(hh[c], norm_ref[:, sv[c % N_HEADS]]) for c in chains]
    for i in range(n_b):
        out_ref[i] = (jnp.concatenate(outs[i * N_HEADS:(i + 1) * N_HEADS], axis=1)
                      * _sigmoid(o_ref[i].astype(F32))).astype(out_ref.dtype)


def _mlstm(proj, gates, gates_t, gate_bias, mlstm_norm, b, t):
    ck = MLSTM_CHUNK
    nc = t // ck
    n_qk = N_HEADS * HEAD_DK
    n_v = N_HEADS * MLSTM_DV
    proj3 = proj.reshape(b, t, proj.shape[-1])
    ltri, utri = _tri_consts(ck)
    nb = _batch_per_step(b)
    const2 = lambda shape: pl.BlockSpec(shape, lambda i, c: (0,) * len(shape))
    return pl.pallas_call(
        _mlstm_kernel,
        grid=(b // nb, nc),
        in_specs=[pl.BlockSpec((nb, ck, n_qk), lambda i, c: (i, c, 0)),
                  pl.BlockSpec((nb, ck, n_qk), lambda i, c: (i, c, 1)),
                  pl.BlockSpec((nb, ck, n_v), lambda i, c: (i, c, 2 * n_qk // n_v)),
                  pl.BlockSpec((nb, ck, n_v), lambda i, c: (i, c, 2 * n_qk // n_v + 1)),
                  pl.BlockSpec((nb, ck, GATE_LANES), lambda i, c: (i, c, 0)),
                  pl.BlockSpec((nb, None, 16, ck), lambda i, c: (i, c, 0, 0)),
                  const2((1, GATE_LANES)), const2((16, 1)), const2((1, n_v)),
                  const2((ck, ck)), const2((ck, ck))],
        out_specs=pl.BlockSpec((nb, ck, n_v), lambda i, c: (i, c, 0)),
        out_shape=jax.ShapeDtypeStruct((b, t, n_v), BF16),
        scratch_shapes=[pltpu.VMEM((nb * N_HEADS, HEAD_DK, 2 * MLSTM_DV), F32),
                        pltpu.VMEM((nb * N_HEADS, 1, GATE_LANES), F32)],
        compiler_params=_cparams(("parallel", "arbitrary")),
        name="mlstm_mixer",
    )(proj3, proj3, proj3, proj3, gates.reshape(b, t, GATE_LANES), gates_t,
      _lane_row(gate_bias, 0), _sub_col(gate_bias, 0), mlstm_norm.reshape(1, n_v).astype(F32), ltri, utri)


def _gates_transposed(gates, b, t, ck):
    return jnp.transpose(gates[:, :16].reshape(b, t // ck, ck, 16), (0, 1, 3, 2))


def _outproj_ffn_kernel(x_ref, ma_ref, mb_ref, wo_ref, nw_ref, w1_ref, w3_ref, w2_ref, out_ref,
                        x1_ref, h_ref, acc_ref):
    j = pl.program_id(1)
    n_a = ma_ref.shape[1]

    @pl.when(j == 0)
    def _():
        x1 = (x_ref[...] + jnp.dot(ma_ref[...], wo_ref[:n_a, :], preferred_element_type=F32)
              + jnp.dot(mb_ref[...], wo_ref[n_a:, :], preferred_element_type=F32))
        x1_ref[...] = x1
        h_ref[...] = _rms(x1, nw_ref[...]).astype(BF16)
        acc_ref[...] = jnp.zeros_like(acc_ref)

    h = h_ref[...]
    a = jnp.dot(h, w1_ref[...], preferred_element_type=F32)
    g = jnp.dot(h, w3_ref[...], preferred_element_type=F32)
    acc_ref[...] += jnp.dot((_silu(a) * g).astype(BF16), w2_ref[...], preferred_element_type=F32)

    @pl.when(j == pl.num_programs(1) - 1)
    def _():
        out_ref[...] = x1_ref[...] + acc_ref[...]


def _outproj_ffn(x2d, mix_a, mix_b, w_out, ffn_norm, w1, w3, w2, tm=512, n_ff_chunks=2):
    n, d = x2d.shape
    tm = min(tm, n)
    d_ff = w1.shape[1]
    tf = d_ff // n_ff_chunks
    n_a, n_b = mix_a.shape[1], mix_b.shape[1]
    return pl.pallas_call(
        _outproj_ffn_kernel,
        grid=(n // tm, n_ff_chunks),
        in_specs=[pl.BlockSpec((tm, d), lambda i, j: (i, 0)),
                  pl.BlockSpec((tm, n_a), lambda i, j: (i, 0)),
                  pl.BlockSpec((tm, n_b), lambda i, j: (i, 0)),
                  pl.BlockSpec((n_a + n_b, d), lambda i, j: (0, 0)),
                  pl.BlockSpec((1, d), lambda i, j: (0, 0)),
                  pl.BlockSpec((d, tf), lambda i, j: (0, j)),
                  pl.BlockSpec((d, tf), lambda i, j: (0, j)),
                  pl.BlockSpec((tf, d), lambda i, j: (j, 0))],
        out_specs=pl.BlockSpec((tm, d), lambda i, j: (i, 0)),
        out_shape=jax.ShapeDtypeStruct((n, d), F32),
        scratch_shapes=[pltpu.VMEM((tm, d), F32), pltpu.VMEM((tm, d), BF16), pltpu.VMEM((tm, d), F32)],
        compiler_params=_cparams(("parallel", "arbitrary")),
        name="outproj_ffn",
    )(x2d, mix_a, mix_b, w_out, ffn_norm.reshape(1, d), w1, w3, w2)


def _outproj_router_kernel(x_ref, m_ref, wo_ref, nw_ref, r_ref, x3_ref, hn_ref, idx_ref, gate_ref):
    x3 = x_ref[...] + jnp.dot(m_ref[...], wo_ref[...], preferred_element_type=F32)
    x3_ref[...] = x3
    hn = _rms(x3, nw_ref[...])
    hn_ref[...] = hn
    hn_hi = hn.astype(BF16)
    hn_lo = (hn - hn_hi.astype(F32)).astype(BF16)
    r_hi, r_lo = r_ref[:, :GATE_LANES], r_ref[:, GATE_LANES:]
    logits = (jnp.dot(hn_hi, r_hi, preferred_element_type=F32)
              + (jnp.dot(hn_hi, r_lo, preferred_element_type=F32)
                 + jnp.dot(hn_lo, r_hi, preferred_element_type=F32)))
    lane = lax.broadcasted_iota(jnp.int32, logits.shape, 1)
    logits = jnp.where(lane < N_EXPERTS, logits, -jnp.inf)
    m0 = jnp.max(logits, axis=-1, keepdims=True)
    i0 = jnp.min(jnp.where(logits == m0, lane, GATE_LANES), axis=-1, keepdims=True)
    rest = jnp.where(lane == i0, -jnp.inf, logits)
    m1 = jnp.max(rest, axis=-1, keepdims=True)
    i1 = jnp.min(jnp.where(rest == m1, lane, GATE_LANES), axis=-1, keepdims=True)
    e1 = jnp.exp(m1 - m0)
    inv = 1.0 / (1.0 + e1)
    idx_ref[...] = jnp.where(lane == 0, i0, jnp.where(lane == 1, i1, 0))
    gate_ref[...] = jnp.where(lane == 0, inv, jnp.where(lane == 1, e1 * inv, 0.0))


def _outproj_router(x2d, mix, w_out, ffn_norm, router, tm=512):
    n, d = x2d.shape
    tm = min(tm, n)
    n_m = mix.shape[1]
    router_pad = jnp.zeros((d, GATE_LANES), F32).at[:, :N_EXPERTS].set(router.astype(F32))
    router_hi = router_pad.astype(BF16)
    router_lo = (router_pad - router_hi.astype(F32)).astype(BF16)
    router_split = jnp.concatenate([router_hi, router_lo], axis=1)
    row = lambda w: pl.BlockSpec((tm, w), lambda i: (i, 0))
    return pl.pallas_call(
        _outproj_router_kernel,
        grid=(n // tm,),
        in_specs=[row(d), row(n_m),
                  pl.BlockSpec((n_m, d), lambda i: (0, 0)),
                  pl.BlockSpec((1, d), lambda i: (0, 0)),
                  pl.BlockSpec((d, 2 * GATE_LANES), lambda i: (0, 0))],
        out_specs=[row(d), row(d), row(GATE_LANES), row(GATE_LANES)],
        out_shape=[jax.ShapeDtypeStruct((n, d), F32), jax.ShapeDtypeStruct((n, d), F32),
                   jax.ShapeDtypeStruct((n, GATE_LANES), jnp.int32),
                   jax.ShapeDtypeStruct((n, GATE_LANES), F32)],
        compiler_params=_cparams(("parallel",)),
        name="outproj_router",
    )(x2d, mix, w_out, ffn_norm.reshape(1, d), router_split)


def _rank_kernel(idx_ref, tri_ref, rank_ref, count_ref, carry_ref):
    @pl.when(pl.program_id(0) == 0)
    def _():
        carry_ref[...] = jnp.zeros_like(carry_ref)

    idx = idx_ref[...]
    lane = lax.broadcasted_iota(jnp.int32, idx.shape, 1)
    e0, e1 = idx[:, 0:1], idx[:, 1:2]
    member = ((lane == e0) | (lane == e1)).astype(F32)
    before = _dot(tri_ref[...], member) + carry_ref[...]
    r0 = jnp.sum(jnp.where(lane == e0, before, 0.0), axis=-1, keepdims=True)
    r1 = jnp.sum(jnp.where(lane == e1, before, 0.0), axis=-1, keepdims=True)
    rank_ref[...] = jnp.where(lane == 0, r0, jnp.where(lane == 1, r1, 0.0)).astype(jnp.int32)
    carry_ref[...] += jnp.sum(member, axis=0, keepdims=True)
    count_ref[...] = carry_ref[...].astype(jnp.int32)


def _expert_ranks(ridx, tm=512):
    n = ridx.shape[0]
    tm = min(tm, n)
    i = np.arange(tm)
    tri = jnp.asarray((i[None, :] < i[:, None]).astype(np.float32)).astype(BF16)
    return pl.pallas_call(
        _rank_kernel,
        grid=(n // tm,),
        in_specs=[pl.BlockSpec((tm, GATE_LANES), lambda i: (i, 0)),
                  pl.BlockSpec((tm, tm), lambda i: (0, 0))],
        out_specs=[pl.BlockSpec((tm, GATE_LANES), lambda i: (i, 0)),
                   pl.BlockSpec((1, GATE_LANES), lambda i: (0, 0))],
        out_shape=[jax.ShapeDtypeStruct((n, GATE_LANES), jnp.int32),
                   jax.ShapeDtypeStruct((1, GATE_LANES), jnp.int32)],
        scratch_shapes=[pltpu.VMEM((1, GATE_LANES), F32)],
        compiler_params=_cparams(("arbitrary",)),
        name="expert_ranks",
    )(ridx, tri)


def _row_copy(src_ref, src_row, dst_ref, dst_row, sem):
    return pltpu.make_async_copy(src_ref.at[pl.ds(src_row, 1)], dst_ref.at[pl.ds(dst_row, 1)], sem)


def _dispatch_kernel(dest_ref, hn_ref, slots_in_ref, slots_ref, buf_ref, load_sems, row_sems, *, tm, n_tok):
    del slots_in_ref
    i = pl.program_id(0)
    last = pl.num_programs(0) - 1
    base = i * tm
    cur = i % 3
    rsem = i % 2

    def load(tile, b):
        return pltpu.make_async_copy(hn_ref.at[pl.ds(tile * tm, tm)], buf_ref.at[b], load_sems.at[b])

    def drain(s):
        for _ in range(2):
            pltpu.make_async_copy(buf_ref.at[0], slots_ref.at[pl.ds(0, tm)], row_sems.at[s]).wait()

    @pl.when(i == 0)
    def _():
        load(0, 0).start()

    @pl.when(i < last)
    def _():
        load(i + 1, (i + 1) % 3).start()

    load(i, cur).wait()

    def start(r, carry):
        _row_copy(buf_ref.at[cur], r, slots_ref, dest_ref[base + r], row_sems.at[rsem]).start()
        _row_copy(buf_ref.at[cur], r, slots_ref, dest_ref[n_tok + base + r], row_sems.at[rsem]).start(priority=1)
        return carry

    lax.fori_loop(0, tm, start, 0, unroll=DMA_UNROLL)

    @pl.when(i > 0)
    def _():
        drain(1 - rsem)

    @pl.when(i == last)
    def _():
        drain(rsem)


def _dispatch(dest_flat, hn, n_slots, tm=512):
    n, d = hn.shape
    tm = min(tm, n)
    grid_spec = pltpu.PrefetchScalarGridSpec(
        num_scalar_prefetch=1,
        grid=(n // tm,),
        in_specs=[pl.BlockSpec(memory_space=pl.ANY),
                  pl.BlockSpec(memory_space=pl.ANY)],
        out_specs=pl.BlockSpec(memory_space=pl.ANY),
        scratch_shapes=[pltpu.VMEM((3, tm, d), hn.dtype), pltpu.SemaphoreType.DMA((3,)),
                        pltpu.SemaphoreType.DMA((2,))],
    )
    return pl.pallas_call(
        functools.partial(_dispatch_kernel, tm=tm, n_tok=n),
        grid_spec=grid_spec,
        out_shape=jax.ShapeDtypeStruct((n_slots, d), hn.dtype),
        input_output_aliases={2: 0},
        compiler_params=_cparams(("arbitrary",)),
        name="row_dispatch",
    )(dest_flat, hn, jnp.zeros((n_slots, d), hn.dtype))


def _moe_kernel(be_ref, nused_ref, x_ref, w1_ref, w3_ref, w2_ref, out_ref):
    i, j = pl.program_id(0), pl.program_id(1)

    @pl.when(i < nused_ref[0])
    def _():
        x = x_ref[...].astype(BF16)
        a = jnp.dot(x, w1_ref[...], preferred_element_type=F32)
        g = jnp.dot(x, w3_ref[...], preferred_element_type=F32)
        y = jnp.dot((_silu(a) * g).astype(BF16), w2_ref[...], preferred_element_type=F32)

        @pl.when(j == 0)
        def _():
            out_ref[...] = y

        @pl.when(j > 0)
        def _():
            out_ref[...] += y

    @pl.when(i >= nused_ref[0])
    def _():
        out_ref[...] = jnp.zeros_like(out_ref)


def _moe_ffn(block_expert, n_used, slots, w1, w3, w2, n_ff_chunks=2):
    n_slots, d = slots.shape
    n_blocks = n_slots // MOE_BLOCK
    d_ff = w1.shape[2]
    tf = d_ff // n_ff_chunks
    last_j = n_ff_chunks - 1

    def blk(i, nused):
        return jnp.minimum(i, nused[0] - 1)

    def chunk(i, j, nused):
        return jnp.where(i < nused[0], j, last_j)

    grid_spec = pltpu.PrefetchScalarGridSpec(
        num_scalar_prefetch=2,
        grid=(n_blocks, n_ff_chunks),
        in_specs=[pl.BlockSpec((MOE_BLOCK, d), lambda i, j, be, nu: (blk(i, nu), 0)),
                  pl.BlockSpec((None, d, tf), lambda i, j, be, nu: (be[blk(i, nu)], 0, chunk(i, j, nu))),
                  pl.BlockSpec((None, d, tf), lambda i, j, be, nu: (be[blk(i, nu)], 0, chunk(i, j, nu))),
                  pl.BlockSpec((None, tf, d), lambda i, j, be, nu: (be[blk(i, nu)], chunk(i, j, nu), 0))],
        out_specs=pl.BlockSpec((MOE_BLOCK, d), lambda i, j, be, nu: (i, 0)),
    )
    return pl.pallas_call(
        _moe_kernel,
        grid_spec=grid_spec,
        out_shape=jax.ShapeDtypeStruct((n_slots, d), F32),
        compiler_params=_cparams(("arbitrary", "arbitrary")),
        name="moe_ffn",
    )(block_expert, n_used, slots, w1, w3, w2)


def _combine_kernel(dest_ref, x_ref, gate_ref, nw_ref, y_ref, out_ref, ybuf_ref, sems, *, tm, n_tok):
    i = pl.program_id(0)
    slot = i % 2

    def gather(tile, s):
        base = tile * tm

        def start(r, carry):
            _row_copy(y_ref, dest_ref[base + r], ybuf_ref.at[s, 0], r, sems.at[s]).start()
            _row_copy(y_ref, dest_ref[n_tok + base + r], ybuf_ref.at[s, 1], r, sems.at[s]).start(priority=1)
            return carry

        lax.fori_loop(0, tm, start, 0, unroll=DMA_UNROLL)

    @pl.when(i == 0)
    def _():
        gather(0, 0)

    @pl.when(i + 1 < pl.num_programs(0))
    def _():
        gather(i + 1, 1 - slot)

    for k in range(2):
        pltpu.make_async_copy(y_ref.at[pl.ds(0, tm)], ybuf_ref.at[slot, k], sems.at[slot]).wait()
    gate = gate_ref[...]
    y = x_ref[...] + (ybuf_ref[slot, 0] * gate[:, 0:1] + ybuf_ref[slot, 1] * gate[:, 1:2])
    out_ref[...] = _rms(y, nw_ref[...])


def _combine(dest_flat, x2d, rgate, final_norm, yb, tm=512):
    n, d = x2d.shape
    tm = min(tm, n)
    grid_spec = pltpu.PrefetchScalarGridSpec(
        num_scalar_prefetch=1,
        grid=(n // tm,),
        in_specs=[pl.BlockSpec((tm, d), lambda i, dest: (i, 0)),
                  pl.BlockSpec((tm, GATE_LANES), lambda i, dest: (i, 0)),
                  pl.BlockSpec((1, d), lambda i, dest: (0, 0)),
                  pl.BlockSpec(memory_space=pl.ANY)],
        out_specs=pl.BlockSpec((tm, d), lambda i, dest: (i, 0)),
        scratch_shapes=[pltpu.VMEM((2, 2, tm, d), F32), pltpu.SemaphoreType.DMA((2,))],
    )
    return pl.pallas_call(
        functools.partial(_combine_kernel, tm=tm, n_tok=n),
        grid_spec=grid_spec,
        out_shape=jax.ShapeDtypeStruct((n, d), F32),
        compiler_params=_cparams(("arbitrary",)),
        name="combine_norm",
    )(dest_flat, x2d, rgate, final_norm.reshape(1, d), yb)


def _split_in_proj(w_in, lo, hi):
    main = jnp.concatenate([w_in[:, :lo], w_in[:, hi:]], axis=1)
    gate = jnp.pad(w_in[:, lo:hi], ((0, 0), (0, GATE_LANES - (hi - lo))))
    return jnp.concatenate([main, gate], axis=1).astype(BF16), main.shape[1]


def kernel(x, mix_norm_0, w_in_0, conv_w_0, a_log_0, dt_bias_0, gdn_norm_0, ret_norm_0, w_out_0, ffn_norm_0,
           ffn_w1_0, ffn_w3_0, ffn_w2_0, mix_norm_1, w_in_1, gate_bias_1, mlstm_norm_1, w_out_1, ffn_norm_1,
           router_1, exp_w1_1, exp_w3_1, exp_w2_1, final_norm):
    b, t, d = x.shape
    n = b * t
    x2d = x.reshape(n, d)

    gdn_qkv = N_HEADS * (2 * HEAD_DK + GDN_DV)
    w0, n_main0 = _split_in_proj(w_in_0, gdn_qkv, gdn_qkv + 2 * N_HEADS)
    proj0, gates0 = _norm_proj(x2d, mix_norm_0, w0, n_main0)
    o_gdn = _gdn(proj0, gates0, _gates_transposed(gates0, b, t, CHUNK), conv_w_0, a_log_0, dt_bias_0, gdn_norm_0, b, t)
    o_ret = _retention(proj0, ret_norm_0, b, t)
    x2 = _outproj_ffn(x2d, o_gdn.reshape(n, -1), o_ret.reshape(n, -1), w_out_0.astype(BF16), ffn_norm_0,
                      ffn_w1_0.astype(BF16), ffn_w3_0.astype(BF16), ffn_w2_0.astype(BF16))

    n_main1 = w_in_1.shape[1] - 2 * N_HEADS
    w1, _ = _split_in_proj(w_in_1, n_main1, n_main1 + 2 * N_HEADS)
    proj1, gates1 = _norm_proj(x2, mix_norm_1, w1, n_main1)
    h_mix = _mlstm(proj1, gates1, _gates_transposed(gates1, b, t, MLSTM_CHUNK), gate_bias_1, mlstm_norm_1, b, t)
    x3, hn, ridx, rgate = _outproj_router(x2, h_mix.reshape(n, -1), w_out_1.astype(BF16), ffn_norm_1, router_1)

    rank, counts = _expert_ranks(ridx)
    counts = counts[0, :N_EXPERTS]
    padded = (counts + MOE_BLOCK - 1) // MOE_BLOCK * MOE_BLOCK
    pad_end = jnp.cumsum(padded)
    pad_start = pad_end - padded
    dest_flat = jnp.concatenate([pad_start[ridx[:, 0]] + rank[:, 0], pad_start[ridx[:, 1]] + rank[:, 1]])
    n_blocks = -(-(2 * n) // MOE_BLOCK) + N_EXPERTS
    block_row0 = jnp.arange(n_blocks, dtype=jnp.int32) * MOE_BLOCK
    block_expert = jnp.minimum(jnp.sum((pad_end[None, :] <= block_row0[:, None]).astype(jnp.int32), axis=1),
                               N_EXPERTS - 1)
    n_used = (pad_end[-1:] // MOE_BLOCK).astype(jnp.int32)

    slots = _dispatch(dest_flat, hn, n_blocks * MOE_BLOCK)
    yb = _moe_ffn(block_expert, n_used, slots, exp_w1_1.astype(BF16), exp_w3_1.astype(BF16),
                  exp_w2_1.astype(BF16))
    out = _combine(dest_flat, x3, rgate, final_norm, yb)
    return out.reshape(b, t, d)
```

```python
import functools
import math

import numpy as np
import jax
import jax.numpy as jnp
from jax import lax
from jax.experimental import pallas as pl
from jax.experimental.pallas import tpu as pltpu

F32 = jnp.float32
BF16 = jnp.bfloat16
HIGHEST = lax.Precision.HIGHEST

EPS = 1e-6
CHUNK = 128
RET_CHUNK = 128
MLSTM_CHUNK = 128
CONV_WIDTH = 4
N_HEADS = 8
HEAD_DK = 64
GDN_DV = 64
RET_DV = 64
MLSTM_DV = 128
ROPE_BASE = 10000.0
N_EXPERTS = 8
MOE_BLOCK = 512
GATE_LANES = 128
MXU_WIDTH = 256
VMEM_LIMIT = 52 * 1024 * 1024
DMA_UNROLL = 8


def _cparams(sem):
    return pltpu.CompilerParams(dimension_semantics=sem, vmem_limit_bytes=VMEM_LIMIT)


def _dot(a, b):
    return jnp.dot(a.astype(BF16), b.astype(BF16), preferred_element_type=F32)


def _dot_nt(a, b):
    return lax.dot_general(a.astype(BF16), b.astype(BF16), (((1,), (1,)), ((), ())),
                           preferred_element_type=F32)


def _dot_tn(a, b):
    return lax.dot_general(a.astype(BF16), b.astype(BF16), (((0,), (0,)), ((), ())),
                           preferred_element_type=F32)


def _dot_f32(a, b):
    return jnp.dot(a, b, precision=HIGHEST, preferred_element_type=F32)


def _rms(x, w):
    return x * lax.rsqrt(jnp.mean(x * x, axis=-1, keepdims=True) + EPS) * w


def _silu(x):
    return x * (1.0 / (1.0 + jnp.exp(-x)))


def _sigmoid(x):
    return 1.0 / (1.0 + jnp.exp(-x))


def _softplus(x):
    return jnp.maximum(x, 0.0) + jnp.log1p(jnp.exp(-jnp.abs(x)))


def _causal(n, strict=False):
    r = lax.broadcasted_iota(jnp.int32, (n, n), 0)
    c = lax.broadcasted_iota(jnp.int32, (n, n), 1)
    return (r > c) if strict else (r >= c)


def _norm_proj_kernel(x_ref, nw_ref, w_ref, main_ref, gate_ref, *, n_main, col_chunk):
    h = _rms(x_ref[...], nw_ref[...]).astype(BF16)
    for c0 in range(0, n_main, col_chunk):
        main_ref[:, c0:c0 + col_chunk] = jnp.dot(
            h, w_ref[:, c0:c0 + col_chunk], preferred_element_type=F32).astype(BF16)
    gate_ref[...] = jnp.dot(h, w_ref[:, n_main:], preferred_element_type=F32)


def _norm_proj(x2d, norm_w, w_cat, n_main, tm=512, col_chunk=512):
    n, d = x2d.shape
    tm = min(tm, n)
    n_cat = w_cat.shape[1]
    return pl.pallas_call(
        functools.partial(_norm_proj_kernel, n_main=n_main, col_chunk=col_chunk),
        grid=(n // tm,),
        in_specs=[pl.BlockSpec((tm, d), lambda i: (i, 0)),
                  pl.BlockSpec((1, d), lambda i: (0, 0)),
                  pl.BlockSpec((d, n_cat), lambda i: (0, 0))],
        out_specs=[pl.BlockSpec((tm, n_main), lambda i: (i, 0)),
                   pl.BlockSpec((tm, GATE_LANES), lambda i: (i, 0))],
        out_shape=[jax.ShapeDtypeStruct((n, n_main), BF16),
                   jax.ShapeDtypeStruct((n, GATE_LANES), F32)],
        compiler_params=_cparams(("parallel",)),
        name="norm_proj",
    )(x2d, norm_w.reshape(1, d), w_cat)


def _gdn_kernel(qkv_ref, z_ref, gate_ref, gate_t_ref, conv_ref, arow_ref, dtrow_ref, acol_ref, dtcol_ref,
                norm_ref, ltri_ref, utri_ref, lvl_ref, ones_ref, out_ref, state_ref, xbuf_ref):
    n_b = qkv_ref.shape[0]
    n_qk = N_HEADS * HEAD_DK

    @pl.when(pl.program_id(1) == 0)
    def _():
        state_ref[...] = jnp.zeros_like(state_ref)
        xbuf_ref[:, 0:8, :] = jnp.zeros((n_b, 8, xbuf_ref.shape[2]), F32)

    causal = _causal(CHUNK)
    strict = _causal(CHUNK, strict=True)
    eye = (lax.broadcasted_iota(jnp.int32, (CHUNK, CHUNK), 0)
           == lax.broadcasted_iota(jnp.int32, (CHUNK, CHUNK), 1)).astype(F32)
    sl = [slice(h * HEAD_DK, (h + 1) * HEAD_DK) for h in range(N_HEADS)]

    q, k, v, bc, gcc, gcr = [], [], [], [], [], []
    for i in range(n_b):
        xbuf_ref[i, 8:8 + CHUNK, :] = qkv_ref[i].astype(F32)
        pre = None
        for j in range(CONV_WIDTH):
            term = xbuf_ref[i, pl.ds(8 - (CONV_WIDTH - 1) + j, CHUNK), :] * conv_ref[j:j + 1, :]
            pre = term if pre is None else pre + term
        xbuf_ref[i, 0:8, :] = xbuf_ref[i, CHUNK:CHUNK + 8, :]
        act = _silu(pre)
        q_all, k_all, v_all = act[:, :n_qk], act[:, n_qk:2 * n_qk], act[:, 2 * n_qk:]
        q_all = q_all * lax.rsqrt(_dot(q_all * q_all, ones_ref[...]) + EPS) * (HEAD_DK ** -0.5)
        k_all = k_all * lax.rsqrt(_dot(k_all * k_all, ones_ref[...]) + EPS)
        gl = gate_ref[i]
        beta_c = _sigmoid(gl)
        g_c = -jnp.exp(arow_ref[...]) * _softplus(gl + dtrow_ref[...])
        gc_c = _dot_f32(ltri_ref[...], g_c)
        g_r = -jnp.exp(acol_ref[...]) * _softplus(gate_t_ref[i] + dtcol_ref[...])
        gc_r = _dot_f32(g_r, utri_ref[...])
        for h in range(N_HEADS):
            q.append(q_all[:, sl[h]])
            k.append(k_all[:, sl[h]])
            v.append(v_all[:, sl[h]])
            bc.append(beta_c[:, h:h + 1])
            gcc.append(gc_c[:, 8 + h:9 + h])
            gcr.append(gc_r[8 + h:9 + h, :])

    chains = range(n_b * N_HEADS)
    decay = [jnp.exp(jnp.where(causal, gcc[c] - gcr[c], -jnp.inf)) for c in chains]
    kb = [k[c] * bc[c] for c in chains]
    scores = [_dot_nt(jnp.concatenate([kb[c], q[c]], axis=0), k[c]) for c in chains]
    lmat = [jnp.where(strict, scores[c][:CHUNK] * decay[c], 0.0) for c in chains]
    qk = [scores[c][CHUNK:] * decay[c] for c in chains]
    tinv = [eye - lmat[c] * lvl_ref[0].astype(F32) for c in chains]
    lmat_mxu = [lmat[c].astype(BF16) for c in chains]
    for lvl in range(1, lvl_ref.shape[0]):
        inner = [_dot(lmat_mxu[c] * lvl_ref[lvl], tinv[c]) for c in chains]
        tinv = [tinv[c] - _dot(tinv[c], inner[c]) for c in chains]
    egc = [jnp.exp(gcc[c]) for c in chains]
    uw = [_dot(tinv[c], jnp.concatenate([v[c] * bc[c], kb[c] * egc[c]], axis=1)) for c in chains]
    state = [state_ref[c] for c in chains]
    v_new = [uw[c][:, :GDN_DV] - _dot(uw[c][:, GDN_DV:], state[c]) for c in chains]
    o = [_dot(q[c] * egc[c], state[c]) + _dot(qk[c], v_new[c]) for c in chains]
    g_last = [gcc[c][CHUNK - 1:CHUNK, :] for c in chains]
    k_dec = [k[c] * jnp.exp(g_last[c] - gcc[c]) for c in chains]
    for c in chains:
        state_ref[c] = state[c] * jnp.exp(g_last[c]) + _dot_tn(k_dec[c], v_new[c])
    for i in range(n_b):
        o_all = jnp.concatenate(o[i * N_HEADS:(i + 1) * N_HEADS], axis=1)
        mean_sq = _dot(o_all * o_all, ones_ref[...]) * (1.0 / GDN_DV)
        out_ref[i] = (o_all * lax.rsqrt(mean_sq + EPS) * norm_ref[...]
                      * _silu(z_ref[i].astype(F32))).astype(out_ref.dtype)


def _level_masks():
    i = np.arange(CHUNK)[:, None]
    j = np.arange(CHUNK)[None, :]
    masks = []
    for lvl in range(CHUNK.bit_length() - 1):
        same_parent = (i >> (lvl + 1)) == (j >> (lvl + 1))
        masks.append((same_parent & (((i >> lvl) & 1) == 1) & (((j >> lvl) & 1) == 0)).astype(np.float32))
    return jnp.asarray(np.stack(masks)).astype(BF16)


def _tri_consts(ck):
    i = np.arange(ck)[:, None]
    j = np.arange(ck)[None, :]
    return jnp.asarray((j <= i).astype(np.float32)), jnp.asarray((i <= j).astype(np.float32))


def _lane_row(vals, offset):
    return jnp.zeros((1, GATE_LANES), F32).at[0, offset:offset + vals.shape[0]].set(vals.astype(F32))


def _sub_col(vals, offset):
    return jnp.zeros((16, 1), F32).at[offset:offset + vals.shape[0], 0].set(vals.astype(F32))


def _head_ones(n_heads, width):
    g = np.arange(n_heads * width) // width
    return jnp.asarray((g[:, None] == g[None, :]).astype(np.float32)).astype(BF16)


def _batch_per_step(b):
    return 2 if b % 2 == 0 else 1


def _gdn(proj, gates, gates_t, conv_w, a_log, dt_bias, gdn_norm, b, t):
    nc = t // CHUNK
    nb = _batch_per_step(b)
    n_qkv = N_HEADS * (2 * HEAD_DK + GDN_DV)
    n_z = N_HEADS * GDN_DV
    proj3 = proj.reshape(b, t, proj.shape[-1])
    ltri, utri = _tri_consts(CHUNK)
    const2 = lambda shape: pl.BlockSpec(shape, lambda i, c: (0,) * len(shape))
    return pl.pallas_call(
        _gdn_kernel,
        grid=(b // nb, nc),
        in_specs=[pl.BlockSpec((nb, CHUNK, n_qkv), lambda i, c: (i, c, 0)),
                  pl.BlockSpec((nb, CHUNK, n_z), lambda i, c: (i, c, n_qkv // n_z)),
                  pl.BlockSpec((nb, CHUNK, GATE_LANES), lambda i, c: (i, c, 0)),
                  pl.BlockSpec((nb, None, 16, CHUNK), lambda i, c: (i, c, 0, 0)),
                  const2((CONV_WIDTH, n_qkv)), const2((1, GATE_LANES)), const2((1, GATE_LANES)),
                  const2((16, 1)), const2((16, 1)), const2((1, n_z)),
                  const2((CHUNK, CHUNK)), const2((CHUNK, CHUNK)),
                  const2((CHUNK.bit_length() - 1, CHUNK, CHUNK)),
                  const2((n_z, n_z))],
        out_specs=pl.BlockSpec((nb, CHUNK, n_z), lambda i, c: (i, c, 0)),
        out_shape=jax.ShapeDtypeStruct((b, t, n_z), BF16),
        scratch_shapes=[pltpu.VMEM((nb * N_HEADS, HEAD_DK, GDN_DV), F32),
                        pltpu.VMEM((nb, CHUNK + 8, n_qkv), F32)],
        compiler_params=_cparams(("parallel", "arbitrary")),
        name="gdn_mixer",
    )(proj3, proj3, gates.reshape(b, t, GATE_LANES), gates_t, conv_w.astype(F32),
      _lane_row(a_log, 8), _lane_row(dt_bias, 8), _sub_col(a_log, 8), _sub_col(dt_bias, 8),
      jnp.tile(gdn_norm.astype(F32), N_HEADS).reshape(1, n_z), ltri, utri, _level_masks(),
      _head_ones(N_HEADS, HEAD_DK))


def _ret_kernel(q_ref, k_ref, v_ref, g_ref, cos_ref, sin_ref, decay_ref, qs_ref, ks_ref, cd_ref, norm_ref,
                ones_ref, out_ref, state_ref):
    @pl.when(pl.program_id(1) == 0)
    def _():
        state_ref[...] = jnp.zeros_like(state_ref)

    width = N_HEADS * HEAD_DK
    half = HEAD_DK // 2
    first_half = (lax.broadcasted_iota(jnp.int32, (q_ref.shape[1], width), 1) % HEAD_DK) < half

    def rot(t_in):
        swapped = jnp.where(first_half, pltpu.roll(t_in, width - half, 1), pltpu.roll(t_in, half, 1))
        return t_in * cos_ref[...] + swapped * sin_ref[...]

    n_b = q_ref.shape[0]
    sl = [slice(h * HEAD_DK, (h + 1) * HEAD_DK) for h in range(N_HEADS)]
    q, k, v, q_in, k_st = [], [], [], [], []
    for i in range(n_b):
        q_all = rot(q_ref[i].astype(F32))
        k_all = rot(k_ref[i].astype(F32)) * (HEAD_DK ** -0.5)
        v_all = v_ref[i].astype(F32)
        qi_all = q_all * qs_ref[...]
        ks_all = k_all * ks_ref[...]
        for h in range(N_HEADS):
            q.append(q_all[:, sl[h]])
            k.append(k_all[:, sl[h]])
            v.append(v_all[:, sl[h]])
            q_in.append(qi_all[:, sl[h]])
            k_st.append(ks_all[:, sl[h]])
    chains = range(n_b * N_HEADS)
    state = [state_ref[c] for c in chains]
    scores = [_dot_nt(q[c], k[c]) * decay_ref[c % N_HEADS] for c in chains]
    inter = [_dot(q_in[c], state[c]) for c in chains]
    o = [_dot(scores[c], v[c]) + inter[c] for c in chains]
    for c in chains:
        state_ref[c] = state[c] * cd_ref[c % N_HEADS] + _dot_tn(k_st[c], v[c])
    inv_dv = 1.0 / RET_DV
    for i in range(n_b):
        o_all = jnp.concatenate(o[i * N_HEADS:(i + 1) * N_HEADS], axis=1)
        xc = o_all - _dot(o_all, ones_ref[...]) * inv_dv
        var = _dot(xc * xc, ones_ref[...]) * inv_dv
        out_ref[i] = (xc * lax.rsqrt(var + EPS) * norm_ref[...]
                      * _silu(g_ref[i].astype(F32))).astype(out_ref.dtype)


def _retention(proj, ret_norm, b, t):
    ck = RET_CHUNK
    nc = t // ck
    width = N_HEADS * HEAD_DK
    half = HEAD_DK // 2
    proj3 = proj.reshape(b, t, proj.shape[-1])
    pos = jnp.arange(t, dtype=F32)
    inv_freq = jnp.power(ROPE_BASE, -jnp.arange(half, dtype=F32) / half)
    ang = pos[:, None] * inv_freq[None, :]
    cos, sin = jnp.cos(ang), jnp.sin(ang)
    cos_t = jnp.tile(jnp.concatenate([cos, cos], axis=1), (1, N_HEADS))
    sin_t = jnp.tile(jnp.concatenate([-sin, sin], axis=1), (1, N_HEADS))
    log_gamma = jnp.log1p(-jnp.exp2(-5.0 - jnp.arange(N_HEADS, dtype=F32)))
    idx = jnp.arange(ck, dtype=F32)
    causal = jnp.tril(jnp.ones((ck, ck), bool))
    decay = jnp.exp(jnp.where(causal, (idx[:, None] - idx[None, :]) * log_gamma[:, None, None], -jnp.inf))
    qs = jnp.repeat(jnp.exp((idx + 1.0) * log_gamma[:, None]).T, HEAD_DK, axis=1)
    ks = jnp.repeat(jnp.exp((ck - 1.0 - idx) * log_gamma[:, None]).T, HEAD_DK, axis=1)
    cd = jnp.broadcast_to(jnp.exp(ck * log_gamma)[:, None, None], (N_HEADS, 1, RET_DV))
    base = (N_HEADS * (2 * HEAD_DK + GDN_DV) + N_HEADS * GDN_DV) // width
    nb = _batch_per_step(b)
    col = lambda k: pl.BlockSpec((nb, ck, width), lambda i, c: (i, c, base + k))
    const2 = lambda shape: pl.BlockSpec(shape, lambda i, c: (0,) * len(shape))
    return pl.pallas_call(
        _ret_kernel,
        grid=(b // nb, nc),
        in_specs=[col(0), col(1), col(2), col(3),
                  pl.BlockSpec((ck, width), lambda i, c: (c, 0)),
                  pl.BlockSpec((ck, width), lambda i, c: (c, 0)),
                  const2((N_HEADS, ck, ck)), const2((ck, width)), const2((ck, width)),
                  const2((N_HEADS, 1, RET_DV)), const2((1, width)), const2((width, width))],
        out_specs=pl.BlockSpec((nb, ck, width), lambda i, c: (i, c, 0)),
        out_shape=jax.ShapeDtypeStruct((b, t, width), BF16),
        scratch_shapes=[pltpu.VMEM((nb * N_HEADS, HEAD_DK, RET_DV), F32)],
        compiler_params=_cparams(("parallel", "arbitrary")),
        name="retention_mixer",
    )(proj3, proj3, proj3, proj3, cos_t, sin_t, decay, qs, ks, cd, ret_norm.reshape(1, width).astype(F32),
      _head_ones(N_HEADS, RET_DV))


def _mlstm_kernel(q_ref, k_ref, v_ref, o_ref, gate_ref, gate_t_ref, brow_ref, bcol_ref, norm_ref,
                  ltri_ref, utri_ref, out_ref, state_ref, m_ref):
    @pl.when(pl.program_id(1) == 0)
    def _():
        state_ref[...] = jnp.zeros_like(state_ref)
        m_ref[...] = jnp.zeros_like(m_ref)

    n_b, ck = q_ref.shape[0], q_ref.shape[1]
    causal = _causal(ck)
    ones_col = (lax.broadcasted_iota(jnp.int32, (ck, MLSTM_DV), 1) == 0).astype(F32)
    sl = [slice(h * HEAD_DK, (h + 1) * HEAD_DK) for h in range(N_HEADS)]
    sv = [slice(h * MLSTM_DV, (h + 1) * MLSTM_DV) for h in range(N_HEADS)]

    q, k, vx, bc, br, ig_c, ig_r = [], [], [], [], [], [], []
    for i in range(n_b):
        q_all = q_ref[i].astype(F32)
        k_all = k_ref[i].astype(F32) * (HEAD_DK ** -0.5)
        v_all = v_ref[i].astype(F32)
        gl = gate_ref[i] + brow_ref[...]
        bcum_c = _dot_f32(ltri_ref[...], -_softplus(-gl))
        gt = gate_t_ref[i] + bcol_ref[...]
        bcum_r = _dot_f32(-_softplus(-gt), utri_ref[...])
        for h in range(N_HEADS):
            q.append(q_all[:, sl[h]])
            k.append(k_all[:, sl[h]])
            vx.append(jnp.concatenate([v_all[:, sv[h]], ones_col], axis=1))
            bc.append(bcum_c[:, 8 + h:9 + h])
            br.append(bcum_r[8 + h:9 + h, :])
            ig_c.append(gl[:, h:h + 1])
            ig_r.append(gt[h:h + 1, :])

    chains = range(n_b * N_HEADS)
    log_w = [jnp.where(causal, bc[c] - br[c] + ig_r[c], -jnp.inf) for c in chains]
    m_intra = [jnp.max(log_w[c], axis=-1, keepdims=True) for c in chains]
    b_end = [bc[c][ck - 1:ck, :] for c in chains]
    lw_end = [b_end[c] - bc[c] + ig_c[c] for c in chains]
    m_end = [jnp.max(lw_end[c], axis=0, keepdims=True) for c in chains]
    m_s = [m_ref[c][:, 0:1] for c in chains]
    cx = [state_ref[c] for c in chains]
    m_t = [jnp.maximum(bc[c] + m_s[c], m_intra[c]) for c in chains]
    inter = [jnp.exp(bc[c] + m_s[c] - m_t[c]) for c in chains]
    qk = [_dot_nt(q[c], k[c]) for c in chains]
    qc = [_dot(q[c], cx[c]) for c in chains]
    s = [qk[c] * jnp.exp(log_w[c] - m_t[c]) for c in chains]
    numx = [inter[c] * qc[c] + _dot(s[c], vx[c]) for c in chains]
    m_new = [jnp.maximum(b_end[c] + m_s[c], m_end[c]) for c in chains]
    wk = [k[c] * jnp.exp(lw_end[c] - m_new[c]) for c in chains]
    for c in chains:
        state_ref[c] = jnp.exp(b_end[c] + m_s[c] - m_new[c]) * cx[c] + _dot_tn(wk[c], vx[c])
        m_ref[c] = jnp.broadcast_to(m_new[c], (1, GATE_LANES))
    hh = [numx[c][:, :MLSTM_DV] / jnp.maximum(jnp.abs(numx[c][:, MLSTM_DV:MLSTM_DV + 1]), jnp.exp(-m_t[c]))
          for c in chains]
    outs = [_rms(hh[c], norm_ref[:, sv[c % N_HEADS]]) for c in chains]
    for i in range(n_b):
        out_ref[i] = (jnp.concatenate(outs[i * N_HEADS:(i + 1) * N_HEADS], axis=1)
                      * _sigmoid(o_ref[i].astype(F32))).astype(out_ref.dtype)


def _mlstm(proj, gates, gates_t, gate_bias, mlstm_norm, b, t):
    ck = MLSTM_CHUNK
    nc = t // ck
    n_qk = N_HEADS * HEAD_DK
    n_v = N_HEADS * MLSTM_DV
    proj3 = proj.reshape(b, t, proj.shape[-1])
    ltri, utri = _tri_consts(ck)
    nb = _batch_per_step(b)
    const2 = lambda shape: pl.BlockSpec(shape, lambda i, c: (0,) * len(shape))
    return pl.pallas_call(
        _mlstm_kernel,
        grid=(b // nb, nc),
        in_specs=[pl.BlockSpec((nb, ck, n_qk), lambda i, c: (i, c, 0)),
                  pl.BlockSpec((nb, ck, n_qk), lambda i, c: (i, c, 1)),
                  pl.BlockSpec((nb, ck, n_v), lambda i, c: (i, c, 2 * n_qk // n_v)),
                  pl.BlockSpec((nb, ck, n_v), lambda i, c: (i, c, 2 * n_qk // n_v + 1)),
                  pl.BlockSpec((nb, ck, GATE_LANES), lambda i, c: (i, c, 0)),
                  pl.BlockSpec((nb, None, 16, ck), lambda i, c: (i, c, 0, 0)),
                  const2((1, GATE_LANES)), const2((16, 1)), const2((1, n_v)),
                  const2((ck, ck)), const2((ck, ck))],
        out_specs=pl.BlockSpec((nb, ck, n_v), lambda i, c: (i, c, 0)),
        out_shape=jax.ShapeDtypeStruct((b, t, n_v), BF16),
        scratch_shapes=[pltpu.VMEM((nb * N_HEADS, HEAD_DK, 2 * MLSTM_DV), F32),
                        pltpu.VMEM((nb * N_HEADS, 1, GATE_LANES), F32)],
        compiler_params=_cparams(("parallel", "arbitrary")),
        name="mlstm_mixer",
    )(proj3, proj3, proj3, proj3, gates.reshape(b, t, GATE_LANES), gates_t,
      _lane_row(gate_bias, 0), _sub_col(gate_bias, 0), mlstm_norm.reshape(1, n_v).astype(F32), ltri, utri)


def _gates_transposed(gates, b, t, ck):
    return jnp.transpose(gates[:, :16].reshape(b, t // ck, ck, 16), (0, 1, 3, 2))


def _outproj_ffn_kernel(x_ref, ma_ref, mb_ref, nw_ref, wo_hbm, w1_hbm, w3_hbm, w2_hbm, out_ref,
                        wo_ref, w1_ref, w3_ref, w2_ref, sems, *, ff_chunks):
    @pl.when(pl.program_id(0) == 0)
    def _():
        copies = [pltpu.make_async_copy(src, dst, sems.at[k]) for k, (src, dst) in enumerate(
            ((wo_hbm, wo_ref), (w1_hbm, w1_ref), (w3_hbm, w3_ref), (w2_hbm, w2_ref)))]
        for cp in copies:
            cp.start()
        for cp in copies:
            cp.wait()

    n_a = ma_ref.shape[1]
    x1 = (x_ref[...] + jnp.dot(ma_ref[...], wo_ref[:n_a, :], preferred_element_type=F32)
          + jnp.dot(mb_ref[...], wo_ref[n_a:, :], preferred_element_type=F32))
    h = _rms(x1, nw_ref[...]).astype(BF16)
    out_ref[...] = x1
    for lo, hi in ff_chunks:
        a = jnp.dot(h, w1_ref[:, lo:hi], preferred_element_type=F32)
        g = jnp.dot(h, w3_ref[:, lo:hi], preferred_element_type=F32)
        out_ref[...] += jnp.dot((_silu(a) * g).astype(BF16), w2_ref[lo:hi, :], preferred_element_type=F32)


def _ff_chunks(d_ff, max_chunk=1024):
    assert d_ff % MXU_WIDTH == 0 and max_chunk % MXU_WIDTH == 0
    bounds = list(range(0, d_ff, max_chunk)) + [d_ff]
    return tuple(zip(bounds[:-1], bounds[1:]))


def _outproj_ffn(x2d, mix_a, mix_b, w_out, ffn_norm, w1, w3, w2, tm=512):
    n, d = x2d.shape
    tm = min(tm, n)
    d_ff = w1.shape[1]
    n_a, n_b = mix_a.shape[1], mix_b.shape[1]
    row = lambda w: pl.BlockSpec((tm, w), lambda i: (i, 0))
    hbm = pl.BlockSpec(memory_space=pl.ANY)
    return pl.pallas_call(
        functools.partial(_outproj_ffn_kernel, ff_chunks=_ff_chunks(d_ff)),
        grid=(n // tm,),
        in_specs=[row(d), row(n_a), row(n_b), pl.BlockSpec((1, d), lambda i: (0, 0)), hbm, hbm, hbm, hbm],
        out_specs=row(d),
        out_shape=jax.ShapeDtypeStruct((n, d), F32),
        scratch_shapes=[pltpu.VMEM(w_out.shape, BF16), pltpu.VMEM(w1.shape, BF16), pltpu.VMEM(w3.shape, BF16),
                        pltpu.VMEM(w2.shape, BF16), pltpu.SemaphoreType.DMA((4,))],
        compiler_params=_cparams(("arbitrary",)),
        name="outproj_ffn",
    )(x2d, mix_a, mix_b, ffn_norm.reshape(1, d), w_out, w1, w3, w2)


def _outproj_router_kernel(x_ref, m_ref, wo_ref, nw_ref, r_ref, x3_ref, hn_ref, idx_ref, gate_ref):
    x3 = x_ref[...] + jnp.dot(m_ref[...], wo_ref[...], preferred_element_type=F32)
    x3_ref[...] = x3
    hn = _rms(x3, nw_ref[...])
    hn_ref[...] = hn
    hn_hi = hn.astype(BF16)
    hn_lo = (hn - hn_hi.astype(F32)).astype(BF16)
    r_hi, r_lo = r_ref[:, :GATE_LANES], r_ref[:, GATE_LANES:]
    logits = (jnp.dot(hn_hi, r_hi, preferred_element_type=F32)
              + (jnp.dot(hn_hi, r_lo, preferred_element_type=F32)
                 + jnp.dot(hn_lo, r_hi, preferred_element_type=F32)))
    lane = lax.broadcasted_iota(jnp.int32, logits.shape, 1)
    logits = jnp.where(lane < N_EXPERTS, logits, -jnp.inf)
    m0 = jnp.max(logits, axis=-1, keepdims=True)
    i0 = jnp.min(jnp.where(logits == m0, lane, GATE_LANES), axis=-1, keepdims=True)
    rest = jnp.where(lane == i0, -jnp.inf, logits)
    m1 = jnp.max(rest, axis=-1, keepdims=True)
    i1 = jnp.min(jnp.where(rest == m1, lane, GATE_LANES), axis=-1, keepdims=True)
    e1 = jnp.exp(m1 - m0)
    inv = 1.0 / (1.0 + e1)
    idx_ref[...] = jnp.where(lane == 0, i0, jnp.where(lane == 1, i1, 0))
    gate_ref[...] = jnp.where(lane == 0, inv, jnp.where(lane == 1, e1 * inv, 0.0))


def _outproj_router(x2d, mix, w_out, ffn_norm, router, tm=1024):
    n, d = x2d.shape
    tm = min(tm, n)
    n_m = mix.shape[1]
    router_pad = jnp.zeros((d, GATE_LANES), F32).at[:, :N_EXPERTS].set(router.astype(F32))
    router_hi = router_pad.astype(BF16)
    router_lo = (router_pad - router_hi.astype(F32)).astype(BF16)
    router_split = jnp.concatenate([router_hi, router_lo], axis=1)
    row = lambda w: pl.BlockSpec((tm, w), lambda i: (i, 0))
    return pl.pallas_call(
        _outproj_router_kernel,
        grid=(n // tm,),
        in_specs=[row(d), row(n_m),
                  pl.BlockSpec((n_m, d), lambda i: (0, 0)),
                  pl.BlockSpec((1, d), lambda i: (0, 0)),
                  pl.BlockSpec((d, 2 * GATE_LANES), lambda i: (0, 0))],
        out_specs=[row(d), row(d), row(GATE_LANES), row(GATE_LANES)],
        out_shape=[jax.ShapeDtypeStruct((n, d), F32), jax.ShapeDtypeStruct((n, d), F32),
                   jax.ShapeDtypeStruct((n, GATE_LANES), jnp.int32),
                   jax.ShapeDtypeStruct((n, GATE_LANES), F32)],
        compiler_params=_cparams(("parallel",)),
        name="outproj_router",
    )(x2d, mix, w_out, ffn_norm.reshape(1, d), router_split)


def _rank_kernel(idx_ref, tri_ref, rank_ref, count_ref, carry_ref):
    @pl.when(pl.program_id(0) == 0)
    def _():
        carry_ref[...] = jnp.zeros_like(carry_ref)

    idx = idx_ref[...]
    lane = lax.broadcasted_iota(jnp.int32, idx.shape, 1)
    e0, e1 = idx[:, 0:1], idx[:, 1:2]
    member = ((lane == e0) | (lane == e1)).astype(F32)
    before = _dot(tri_ref[...], member) + carry_ref[...]
    r0 = jnp.sum(jnp.where(lane == e0, before, 0.0), axis=-1, keepdims=True)
    r1 = jnp.sum(jnp.where(lane == e1, before, 0.0), axis=-1, keepdims=True)
    rank_ref[...] = jnp.where(lane == 0, r0, jnp.where(lane == 1, r1, 0.0)).astype(jnp.int32)
    carry_ref[...] += jnp.sum(member, axis=0, keepdims=True)
    count_ref[...] = carry_ref[...].astype(jnp.int32)


def _expert_ranks(ridx, tm=512):
    n = ridx.shape[0]
    tm = min(tm, n)
    i = np.arange(tm)
    tri = jnp.asarray((i[None, :] < i[:, None]).astype(np.float32)).astype(BF16)
    return pl.pallas_call(
        _rank_kernel,
        grid=(n // tm,),
        in_specs=[pl.BlockSpec((tm, GATE_LANES), lambda i: (i, 0)),
                  pl.BlockSpec((tm, tm), lambda i: (0, 0))],
        out_specs=[pl.BlockSpec((tm, GATE_LANES), lambda i: (i, 0)),
                   pl.BlockSpec((1, GATE_LANES), lambda i: (0, 0))],
        out_shape=[jax.ShapeDtypeStruct((n, GATE_LANES), jnp.int32),
                   jax.ShapeDtypeStruct((1, GATE_LANES), jnp.int32)],
        scratch_shapes=[pltpu.VMEM((1, GATE_LANES), F32)],
        compiler_params=_cparams(("arbitrary",)),
        name="expert_ranks",
    )(ridx, tri)


def _row_copy(src_ref, src_row, dst_ref, dst_row, sem):
    return pltpu.make_async_copy(src_ref.at[pl.ds(src_row, 1)], dst_ref.at[pl.ds(dst_row, 1)], sem)


def _dispatch_kernel(dest_ref, hn_ref, slots_in_ref, slots_ref, buf_ref, load_sems, row_sems, *, tm, n_tok):
    del slots_in_ref
    i = pl.program_id(0)
    last = pl.num_programs(0) - 1
    base = i * tm
    cur = i % 3
    rsem = i % 2

    def load(tile, b):
        return pltpu.make_async_copy(hn_ref.at[pl.ds(tile * tm, tm)], buf_ref.at[b], load_sems.at[b])

    def drain(s):
        for _ in range(2):
            pltpu.make_async_copy(buf_ref.at[0], slots_ref.at[pl.ds(0, tm)], row_sems.at[s]).wait()

    @pl.when(i == 0)
    def _():
        load(0, 0).start()

    @pl.when(i < last)
    def _():
        load(i + 1, (i + 1) % 3).start()

    load(i, cur).wait()

    def start(r, carry):
        _row_copy(buf_ref.at[cur], r, slots_ref, dest_ref[base + r], row_sems.at[rsem]).start()
        _row_copy(buf_ref.at[cur], r, slots_ref, dest_ref[n_tok + base + r], row_sems.at[rsem]).start(priority=1)
        return carry

    lax.fori_loop(0, tm, start, 0, unroll=DMA_UNROLL)

    @pl.when(i > 0)
    def _():
        drain(1 - rsem)

    @pl.when(i == last)
    def _():
        drain(rsem)


def _dispatch(dest_flat, hn, n_slots, tm=512):
    n, d = hn.shape
    tm = min(tm, n)
    grid_spec = pltpu.PrefetchScalarGridSpec(
        num_scalar_prefetch=1,
        grid=(n // tm,),
        in_specs=[pl.BlockSpec(memory_space=pl.ANY),
                  pl.BlockSpec(memory_space=pl.ANY)],
        out_specs=pl.BlockSpec(memory_space=pl.ANY),
        scratch_shapes=[pltpu.VMEM((3, tm, d), hn.dtype), pltpu.SemaphoreType.DMA((3,)),
                        pltpu.SemaphoreType.DMA((2,))],
    )
    return pl.pallas_call(
        functools.partial(_dispatch_kernel, tm=tm, n_tok=n),
        grid_spec=grid_spec,
        out_shape=jax.ShapeDtypeStruct((n_slots, d), hn.dtype),
        input_output_aliases={2: 0},
        compiler_params=_cparams(("arbitrary",)),
        name="row_dispatch",
    )(dest_flat, hn, jnp.zeros((n_slots, d), hn.dtype))


def _moe_kernel(be_ref, nused_ref, x_ref, w1_ref, w3_ref, w2_ref, out_ref):
    i, j = pl.program_id(0), pl.program_id(1)

    @pl.when(i < nused_ref[0])
    def _():
        x = x_ref[...].astype(BF16)
        a = jnp.dot(x, w1_ref[...], preferred_element_type=F32)
        g = jnp.dot(x, w3_ref[...], preferred_element_type=F32)
        y = jnp.dot((_silu(a) * g).astype(BF16), w2_ref[...], preferred_element_type=F32)

        @pl.when(j == 0)
        def _():
            out_ref[...] = y

        @pl.when(j > 0)
        def _():
            out_ref[...] += y

    @pl.when(i >= nused_ref[0])
    def _():
        out_ref[...] = jnp.zeros_like(out_ref)


def _moe_ffn(block_expert, n_used, slots, w1, w3, w2, n_ff_chunks=2):
    n_slots, d = slots.shape
    n_blocks = n_slots // MOE_BLOCK
    d_ff = w1.shape[2]
    tf = d_ff // n_ff_chunks
    last_j = n_ff_chunks - 1

    def blk(i, nused):
        return jnp.minimum(i, nused[0] - 1)

    def chunk(i, j, nused):
        return jnp.where(i < nused[0], j, last_j)

    grid_spec = pltpu.PrefetchScalarGridSpec(
        num_scalar_prefetch=2,
        grid=(n_blocks, n_ff_chunks),
        in_specs=[pl.BlockSpec((MOE_BLOCK, d), lambda i, j, be, nu: (blk(i, nu), 0)),
                  pl.BlockSpec((None, d, tf), lambda i, j, be, nu: (be[blk(i, nu)], 0, chunk(i, j, nu))),
                  pl.BlockSpec((None, d, tf), lambda i, j, be, nu: (be[blk(i, nu)], 0, chunk(i, j, nu))),
                  pl.BlockSpec((None, tf, d), lambda i, j, be, nu: (be[blk(i, nu)], chunk(i, j, nu), 0))],
        out_specs=pl.BlockSpec((MOE_BLOCK, d), lambda i, j, be, nu: (i, 0)),
    )
    return pl.pallas_call(
        _moe_kernel,
        grid_spec=grid_spec,
        out_shape=jax.ShapeDtypeStruct((n_slots, d), F32),
        compiler_params=_cparams(("arbitrary", "arbitrary")),
        name="moe_ffn",
    )(block_expert, n_used, slots, w1, w3, w2)


def _combine_kernel(dest_ref, x_ref, gate_ref, nw_ref, y_ref, out_ref, ybuf_ref, sems, *, tm, n_tok):
    i = pl.program_id(0)
    slot = i % 2

    def gather(tile, s):
        base = tile * tm

        def start(r, carry):
            _row_copy(y_ref, dest_ref[base + r], ybuf_ref.at[s, 0], r, sems.at[s]).start()
            _row_copy(y_ref, dest_ref[n_tok + base + r], ybuf_ref.at[s, 1], r, sems.at[s]).start(priority=1)
            return carry

        lax.fori_loop(0, tm, start, 0, unroll=DMA_UNROLL)

    @pl.when(i == 0)
    def _():
        gather(0, 0)

    @pl.when(i + 1 < pl.num_programs(0))
    def _():
        gather(i + 1, 1 - slot)

    for k in range(2):
        pltpu.make_async_copy(y_ref.at[pl.ds(0, tm)], ybuf_ref.at[slot, k], sems.at[slot]).wait()
    gate = gate_ref[...]
    y = x_ref[...] + (ybuf_ref[slot, 0] * gate[:, 0:1] + ybuf_ref[slot, 1] * gate[:, 1:2])
    out_ref[...] = _rms(y, nw_ref[...])


def _combine(dest_flat, x2d, rgate, final_norm, yb, tm=512):
    n, d = x2d.shape
    tm = min(tm, n)
    grid_spec = pltpu.PrefetchScalarGridSpec(
        num_scalar_prefetch=1,
        grid=(n // tm,),
        in_specs=[pl.BlockSpec((tm, d), lambda i, dest: (i, 0)),
                  pl.BlockSpec((tm, GATE_LANES), lambda i, dest: (i, 0)),
                  pl.BlockSpec((1, d), lambda i, dest: (0, 0)),
                  pl.BlockSpec(memory_space=pl.ANY)],
        out_specs=pl.BlockSpec((tm, d), lambda i, dest: (i, 0)),
        scratch_shapes=[pltpu.VMEM((2, 2, tm, d), F32), pltpu.SemaphoreType.DMA((2,))],
    )
    return pl.pallas_call(
        functools.partial(_combine_kernel, tm=tm, n_tok=n),
        grid_spec=grid_spec,
        out_shape=jax.ShapeDtypeStruct((n, d), F32),
        compiler_params=_cparams(("arbitrary",)),
        name="combine_norm",
    )(dest_flat, x2d, rgate, final_norm.reshape(1, d), yb)


def _split_in_proj(w_in, lo, hi):
    main = jnp.concatenate([w_in[:, :lo], w_in[:, hi:]], axis=1)
    gate = jnp.pad(w_in[:, lo:hi], ((0, 0), (0, GATE_LANES - (hi - lo))))
    return jnp.concatenate([main, gate], axis=1).astype(BF16), main.shape[1]


def kernel(x, mix_norm_0, w_in_0, conv_w_0, a_log_0, dt_bias_0, gdn_norm_0, ret_norm_0, w_out_0, ffn_norm_0,
           ffn_w1_0, ffn_w3_0, ffn_w2_0, mix_norm_1, w_in_1, gate_bias_1, mlstm_norm_1, w_out_1, ffn_norm_1,
           router_1, exp_w1_1, exp_w3_1, exp_w2_1, final_norm):
    b, t, d = x.shape
    n = b * t
    x2d = x.reshape(n, d)

    gdn_qkv = N_HEADS * (2 * HEAD_DK + GDN_DV)
    w0, n_main0 = _split_in_proj(w_in_0, gdn_qkv, gdn_qkv + 2 * N_HEADS)
    proj0, gates0 = _norm_proj(x2d, mix_norm_0, w0, n_main0)
    o_gdn = _gdn(proj0, gates0, _gates_transposed(gates0, b, t, CHUNK), conv_w_0, a_log_0, dt_bias_0, gdn_norm_0, b, t)
    o_ret = _retention(proj0, ret_norm_0, b, t)
    x2 = _outproj_ffn(x2d, o_gdn.reshape(n, -1), o_ret.reshape(n, -1), w_out_0.astype(BF16), ffn_norm_0,
                      ffn_w1_0.astype(BF16), ffn_w3_0.astype(BF16), ffn_w2_0.astype(BF16))

    n_main1 = w_in_1.shape[1] - 2 * N_HEADS
    w1, _ = _split_in_proj(w_in_1, n_main1, n_main1 + 2 * N_HEADS)
    proj1, gates1 = _norm_proj(x2, mix_norm_1, w1, n_main1)
    h_mix = _mlstm(proj1, gates1, _gates_transposed(gates1, b, t, MLSTM_CHUNK), gate_bias_1, mlstm_norm_1, b, t)
    x3, hn, ridx, rgate = _outproj_router(x2, h_mix.reshape(n, -1), w_out_1.astype(BF16), ffn_norm_1, router_1)

    rank, counts = _expert_ranks(ridx)
    counts = counts[0, :N_EXPERTS]
    padded = (counts + MOE_BLOCK - 1) // MOE_BLOCK * MOE_BLOCK
    pad_end = jnp.cumsum(padded)
    pad_start = pad_end - padded
    dest_flat = jnp.concatenate([pad_start[ridx[:, 0]] + rank[:, 0], pad_start[ridx[:, 1]] + rank[:, 1]])
    n_blocks = -(-(2 * n) // MOE_BLOCK) + N_EXPERTS
    block_row0 = jnp.arange(n_blocks, dtype=jnp.int32) * MOE_BLOCK
    block_expert = jnp.minimum(jnp.sum((pad_end[None, :] <= block_row0[:, None]).astype(jnp.int32), axis=1),
                               N_EXPERTS - 1)
    n_used = (pad_end[-1:] // MOE_BLOCK).astype(jnp.int32)

    slots = _dispatch(dest_flat, hn, n_blocks * MOE_BLOCK)
    yb = _moe_ffn(block_expert, n_used, slots, exp_w1_1.astype(BF16), exp_w3_1.astype(BF16),
                  exp_w2_1.astype(BF16))
    out = _combine(dest_flat, x3, rgate, final_norm, yb)
    return out.reshape(b, t, d)
```

```python
import functools
import math

import numpy as np
import jax
import jax.numpy as jnp
from jax import lax
from jax.experimental import pallas as pl
from jax.experimental.pallas import tpu as pltpu

F32 = jnp.float32
BF16 = jnp.bfloat16
HIGHEST = lax.Precision.HIGHEST

EPS = 1e-6
CHUNK = 128
RET_CHUNK = 128
MLSTM_CHUNK = 128
CONV_WIDTH = 4
N_HEADS = 8
HEAD_DK = 64
GDN_DV = 64
RET_DV = 64
MLSTM_DV = 128
ROPE_BASE = 10000.0
N_EXPERTS = 8
MOE_BLOCK = 512
GATE_LANES = 128
MXU_WIDTH = 256
VMEM_LIMIT = 52 * 1024 * 1024
DMA_UNROLL = 8


def _cparams(sem):
    return pltpu.CompilerParams(dimension_semantics=sem, vmem_limit_bytes=VMEM_LIMIT)


def _dot(a, b):
    return jnp.dot(a.astype(BF16), b.astype(BF16), preferred_element_type=F32)


def _dot_nt(a, b):
    return lax.dot_general(a.astype(BF16), b.astype(BF16), (((1,), (1,)), ((), ())),
                           preferred_element_type=F32)


def _dot_tn(a, b):
    return lax.dot_general(a.astype(BF16), b.astype(BF16), (((0,), (0,)), ((), ())),
                           preferred_element_type=F32)


def _dot_f32(a, b):
    return jnp.dot(a, b, precision=HIGHEST, preferred_element_type=F32)


def _rms(x, w):
    return x * lax.rsqrt(jnp.mean(x * x, axis=-1, keepdims=True) + EPS) * w


def _silu(x):
    return x * (1.0 / (1.0 + jnp.exp(-x)))


def _sigmoid(x):
    return 1.0 / (1.0 + jnp.exp(-x))


def _softplus(x):
    return jnp.maximum(x, 0.0) + jnp.log1p(jnp.exp(-jnp.abs(x)))


def _causal(n, strict=False):
    r = lax.broadcasted_iota(jnp.int32, (n, n), 0)
    c = lax.broadcasted_iota(jnp.int32, (n, n), 1)
    return (r > c) if strict else (r >= c)


def _norm_proj_kernel(x_ref, nw_ref, w_ref, main_ref, gate_ref, *, n_main, col_chunk):
    h = _rms(x_ref[...], nw_ref[...]).astype(BF16)
    for c0 in range(0, n_main, col_chunk):
        main_ref[:, c0:c0 + col_chunk] = jnp.dot(
            h, w_ref[:, c0:c0 + col_chunk], preferred_element_type=F32).astype(BF16)
    gate_ref[...] = jnp.dot(h, w_ref[:, n_main:], preferred_element_type=F32)


def _norm_proj(x2d, norm_w, w_cat, n_main, tm=512, col_chunk=512):
    n, d = x2d.shape
    tm = min(tm, n)
    n_cat = w_cat.shape[1]
    return pl.pallas_call(
        functools.partial(_norm_proj_kernel, n_main=n_main, col_chunk=col_chunk),
        grid=(n // tm,),
        in_specs=[pl.BlockSpec((tm, d), lambda i: (i, 0)),
                  pl.BlockSpec((1, d), lambda i: (0, 0)),
                  pl.BlockSpec((d, n_cat), lambda i: (0, 0))],
        out_specs=[pl.BlockSpec((tm, n_main), lambda i: (i, 0)),
                   pl.BlockSpec((tm, GATE_LANES), lambda i: (i, 0))],
        out_shape=[jax.ShapeDtypeStruct((n, n_main), BF16),
                   jax.ShapeDtypeStruct((n, GATE_LANES), F32)],
        compiler_params=_cparams(("parallel",)),
        name="norm_proj",
    )(x2d, norm_w.reshape(1, d), w_cat)


def _gdn_kernel(qkv_ref, z_ref, gate_ref, gate_t_ref, conv_ref, arow_ref, dtrow_ref, acol_ref, dtcol_ref,
                norm_ref, ltri_ref, utri_ref, lvl_ref, ones_ref, out_ref, state_ref, xbuf_ref):
    n_b = qkv_ref.shape[0]
    n_qk = N_HEADS * HEAD_DK

    @pl.when(pl.program_id(1) == 0)
    def _():
        state_ref[...] = jnp.zeros_like(state_ref)
        xbuf_ref[:, 0:8, :] = jnp.zeros((n_b, 8, xbuf_ref.shape[2]), F32)

    causal = _causal(CHUNK)
    strict = _causal(CHUNK, strict=True)
    eye = (lax.broadcasted_iota(jnp.int32, (CHUNK, CHUNK), 0)
           == lax.broadcasted_iota(jnp.int32, (CHUNK, CHUNK), 1)).astype(F32)
    sl = [slice(h * HEAD_DK, (h + 1) * HEAD_DK) for h in range(N_HEADS)]

    q, k, v, bc, gcc, gcr = [], [], [], [], [], []
    for i in range(n_b):
        xbuf_ref[i, 8:8 + CHUNK, :] = qkv_ref[i].astype(F32)
        pre = None
        for j in range(CONV_WIDTH):
            term = xbuf_ref[i, pl.ds(8 - (CONV_WIDTH - 1) + j, CHUNK), :] * conv_ref[j:j + 1, :]
            pre = term if pre is None else pre + term
        xbuf_ref[i, 0:8, :] = xbuf_ref[i, CHUNK:CHUNK + 8, :]
        act = _silu(pre)
        q_all, k_all, v_all = act[:, :n_qk], act[:, n_qk:2 * n_qk], act[:, 2 * n_qk:]
        q_all = q_all * lax.rsqrt(_dot(q_all * q_all, ones_ref[...]) + EPS) * (HEAD_DK ** -0.5)
        k_all = k_all * lax.rsqrt(_dot(k_all * k_all, ones_ref[...]) + EPS)
        gl = gate_ref[i]
        beta_c = _sigmoid(gl)
        g_c = -jnp.exp(arow_ref[...]) * _softplus(gl + dtrow_ref[...])
        gc_c = _dot_f32(ltri_ref[...], g_c)
        g_r = -jnp.exp(acol_ref[...]) * _softplus(gate_t_ref[i] + dtcol_ref[...])
        gc_r = _dot_f32(g_r, utri_ref[...])
        for h in range(N_HEADS):
            q.append(q_all[:, sl[h]])
            k.append(k_all[:, sl[h]])
            v.append(v_all[:, sl[h]])
            bc.append(beta_c[:, h:h + 1])
            gcc.append(gc_c[:, 8 + h:9 + h])
            gcr.append(gc_r[8 + h:9 + h, :])

    chains = range(n_b * N_HEADS)
    decay = [jnp.exp(jnp.where(causal, gcc[c] - gcr[c], -jnp.inf)) for c in chains]
    kb = [k[c] * bc[c] for c in chains]
    scores = [_dot_nt(jnp.concatenate([kb[c], q[c]], axis=0), k[c]) for c in chains]
    lmat = [jnp.where(strict, scores[c][:CHUNK] * decay[c], 0.0) for c in chains]
    qk = [scores[c][CHUNK:] * decay[c] for c in chains]
    tinv = [eye - lmat[c] * lvl_ref[0].astype(F32) for c in chains]
    lmat_mxu = [lmat[c].astype(BF16) for c in chains]
    for lvl in range(1, lvl_ref.shape[0]):
        inner = [_dot(lmat_mxu[c] * lvl_ref[lvl], tinv[c]) for c in chains]
        tinv = [tinv[c] - _dot(tinv[c], inner[c]) for c in chains]
    egc = [jnp.exp(gcc[c]) for c in chains]
    uw = [_dot(tinv[c], jnp.concatenate([v[c] * bc[c], kb[c] * egc[c]], axis=1)) for c in chains]
    state = [state_ref[c] for c in chains]
    v_new = [uw[c][:, :GDN_DV] - _dot(uw[c][:, GDN_DV:], state[c]) for c in chains]
    o = [_dot(q[c] * egc[c], state[c]) + _dot(qk[c], v_new[c]) for c in chains]
    g_last = [gcc[c][CHUNK - 1:CHUNK, :] for c in chains]
    k_dec = [k[c] * jnp.exp(g_last[c] - gcc[c]) for c in chains]
    for c in chains:
        state_ref[c] = state[c] * jnp.exp(g_last[c]) + _dot_tn(k_dec[c], v_new[c])
    for i in range(n_b):
        o_all = jnp.concatenate(o[i * N_HEADS:(i + 1) * N_HEADS], axis=1)
        mean_sq = _dot(o_all * o_all, ones_ref[...]) * (1.0 / GDN_DV)
        out_ref[i] = (o_all * lax.rsqrt(mean_sq + EPS) * norm_ref[...]
                      * _silu(z_ref[i].astype(F32))).astype(out_ref.dtype)


def _level_masks():
    i = np.arange(CHUNK)[:, None]
    j = np.arange(CHUNK)[None, :]
    masks = []
    for lvl in range(CHUNK.bit_length() - 1):
        same_parent = (i >> (lvl + 1)) == (j >> (lvl + 1))
        masks.append((same_parent & (((i >> lvl) & 1) == 1) & (((j >> lvl) & 1) == 0)).astype(np.float32))
    return jnp.asarray(np.stack(masks)).astype(BF16)


def _tri_consts(ck):
    i = np.arange(ck)[:, None]
    j = np.arange(ck)[None, :]
    return jnp.asarray((j <= i).astype(np.float32)), jnp.asarray((i <= j).astype(np.float32))


def _lane_row(vals, offset):
    return jnp.zeros((1, GATE_LANES), F32).at[0, offset:offset + vals.shape[0]].set(vals.astype(F32))


def _sub_col(vals, offset):
    return jnp.zeros((16, 1), F32).at[offset:offset + vals.shape[0], 0].set(vals.astype(F32))


def _head_ones(n_heads, width):
    g = np.arange(n_heads * width) // width
    return jnp.asarray((g[:, None] == g[None, :]).astype(np.float32)).astype(BF16)


def _batch_per_step(b):
    return 2 if b % 2 == 0 else 1


def _gdn(proj, gates, gates_t, conv_w, a_log, dt_bias, gdn_norm, b, t):
    nc = t // CHUNK
    nb = _batch_per_step(b)
    n_qkv = N_HEADS * (2 * HEAD_DK + GDN_DV)
    n_z = N_HEADS * GDN_DV
    proj3 = proj.reshape(b, t, proj.shape[-1])
    ltri, utri = _tri_consts(CHUNK)
    const2 = lambda shape: pl.BlockSpec(shape, lambda i, c: (0,) * len(shape))
    return pl.pallas_call(
        _gdn_kernel,
        grid=(b // nb, nc),
        in_specs=[pl.BlockSpec((nb, CHUNK, n_qkv), lambda i, c: (i, c, 0)),
                  pl.BlockSpec((nb, CHUNK, n_z), lambda i, c: (i, c, n_qkv // n_z)),
                  pl.BlockSpec((nb, CHUNK, GATE_LANES), lambda i, c: (i, c, 0)),
                  pl.BlockSpec((nb, None, 16, CHUNK), lambda i, c: (i, c, 0, 0)),
                  const2((CONV_WIDTH, n_qkv)), const2((1, GATE_LANES)), const2((1, GATE_LANES)),
                  const2((16, 1)), const2((16, 1)), const2((1, n_z)),
                  const2((CHUNK, CHUNK)), const2((CHUNK, CHUNK)),
                  const2((CHUNK.bit_length() - 1, CHUNK, CHUNK)),
                  const2((n_z, n_z))],
        out_specs=pl.BlockSpec((nb, CHUNK, n_z), lambda i, c: (i, c, 0)),
        out_shape=jax.ShapeDtypeStruct((b, t, n_z), BF16),
        scratch_shapes=[pltpu.VMEM((nb * N_HEADS, HEAD_DK, GDN_DV), F32),
                        pltpu.VMEM((nb, CHUNK + 8, n_qkv), F32)],
        compiler_params=_cparams(("parallel", "arbitrary")),
        name="gdn_mixer",
    )(proj3, proj3, gates.reshape(b, t, GATE_LANES), gates_t, conv_w.astype(F32),
      _lane_row(a_log, 8), _lane_row(dt_bias, 8), _sub_col(a_log, 8), _sub_col(dt_bias, 8),
      jnp.tile(gdn_norm.astype(F32), N_HEADS).reshape(1, n_z), ltri, utri, _level_masks(),
      _head_ones(N_HEADS, HEAD_DK))


def _ret_kernel(q_ref, k_ref, v_ref, g_ref, cos_ref, sin_ref, decay_ref, qs_ref, ks_ref, cd_ref, norm_ref,
                ones_ref, out_ref, state_ref):
    @pl.when(pl.program_id(1) == 0)
    def _():
        state_ref[...] = jnp.zeros_like(state_ref)

    width = N_HEADS * HEAD_DK
    half = HEAD_DK // 2
    first_half = (lax.broadcasted_iota(jnp.int32, (q_ref.shape[1], width), 1) % HEAD_DK) < half

    def rot(t_in):
        swapped = jnp.where(first_half, pltpu.roll(t_in, width - half, 1), pltpu.roll(t_in, half, 1))
        return t_in * cos_ref[...] + swapped * sin_ref[...]

    n_b = q_ref.shape[0]
    sl = [slice(h * HEAD_DK, (h + 1) * HEAD_DK) for h in range(N_HEADS)]
    q, k, v, q_in, k_st = [], [], [], [], []
    for i in range(n_b):
        q_all = rot(q_ref[i].astype(F32))
        k_all = rot(k_ref[i].astype(F32)) * (HEAD_DK ** -0.5)
        v_all = v_ref[i].astype(F32)
        qi_all = q_all * qs_ref[...]
        ks_all = k_all * ks_ref[...]
        for h in range(N_HEADS):
            q.append(q_all[:, sl[h]])
            k.append(k_all[:, sl[h]])
            v.append(v_all[:, sl[h]])
            q_in.append(qi_all[:, sl[h]])
            k_st.append(ks_all[:, sl[h]])
    chains = range(n_b * N_HEADS)
    state = [state_ref[c] for c in chains]
    scores = [_dot_nt(q[c], k[c]) * decay_ref[c % N_HEADS] for c in chains]
    inter = [_dot(q_in[c], state[c]) for c in chains]
    o = [_dot(scores[c], v[c]) + inter[c] for c in chains]
    for c in chains:
        state_ref[c] = state[c] * cd_ref[c % N_HEADS] + _dot_tn(k_st[c], v[c])
    inv_dv = 1.0 / RET_DV
    for i in range(n_b):
        o_all = jnp.concatenate(o[i * N_HEADS:(i + 1) * N_HEADS], axis=1)
        xc = o_all - _dot(o_all, ones_ref[...]) * inv_dv
        var = _dot(xc * xc, ones_ref[...]) * inv_dv
        out_ref[i] = (xc * lax.rsqrt(var + EPS) * norm_ref[...]
                      * _silu(g_ref[i].astype(F32))).astype(out_ref.dtype)


def _retention(proj, ret_norm, b, t):
    ck = RET_CHUNK
    nc = t // ck
    width = N_HEADS * HEAD_DK
    half = HEAD_DK // 2
    proj3 = proj.reshape(b, t, proj.shape[-1])
    pos = jnp.arange(t, dtype=F32)
    inv_freq = jnp.power(ROPE_BASE, -jnp.arange(half, dtype=F32) / half)
    ang = pos[:, None] * inv_freq[None, :]
    cos, sin = jnp.cos(ang), jnp.sin(ang)
    cos_t = jnp.tile(jnp.concatenate([cos, cos], axis=1), (1, N_HEADS))
    sin_t = jnp.tile(jnp.concatenate([-sin, sin], axis=1), (1, N_HEADS))
    log_gamma = jnp.log1p(-jnp.exp2(-5.0 - jnp.arange(N_HEADS, dtype=F32)))
    idx = jnp.arange(ck, dtype=F32)
    causal = jnp.tril(jnp.ones((ck, ck), bool))
    decay = jnp.exp(jnp.where(causal, (idx[:, None] - idx[None, :]) * log_gamma[:, None, None], -jnp.inf))
    qs = jnp.repeat(jnp.exp((idx + 1.0) * log_gamma[:, None]).T, HEAD_DK, axis=1)
    ks = jnp.repeat(jnp.exp((ck - 1.0 - idx) * log_gamma[:, None]).T, HEAD_DK, axis=1)
    cd = jnp.broadcast_to(jnp.exp(ck * log_gamma)[:, None, None], (N_HEADS, 1, RET_DV))
    base = (N_HEADS * (2 * HEAD_DK + GDN_DV) + N_HEADS * GDN_DV) // width
    nb = _batch_per_step(b)
    col = lambda k: pl.BlockSpec((nb, ck, width), lambda i, c: (i, c, base + k))
    const2 = lambda shape: pl.BlockSpec(shape, lambda i, c: (0,) * len(shape))
    return pl.pallas_call(
        _ret_kernel,
        grid=(b // nb, nc),
        in_specs=[col(0), col(1), col(2), col(3),
                  pl.BlockSpec((ck, width), lambda i, c: (c, 0)),
                  pl.BlockSpec((ck, width), lambda i, c: (c, 0)),
                  const2((N_HEADS, ck, ck)), const2((ck, width)), const2((ck, width)),
                  const2((N_HEADS, 1, RET_DV)), const2((1, width)), const2((width, width))],
        out_specs=pl.BlockSpec((nb, ck, width), lambda i, c: (i, c, 0)),
        out_shape=jax.ShapeDtypeStruct((b, t, width), BF16),
        scratch_shapes=[pltpu.VMEM((nb * N_HEADS, HEAD_DK, RET_DV), F32)],
        compiler_params=_cparams(("parallel", "arbitrary")),
        name="retention_mixer",
    )(proj3, proj3, proj3, proj3, cos_t, sin_t, decay, qs, ks, cd, ret_norm.reshape(1, width).astype(F32),
      _head_ones(N_HEADS, RET_DV))


def _mlstm_kernel(q_ref, k_ref, v_ref, o_ref, gate_ref, gate_t_ref, brow_ref, bcol_ref, norm_ref,
                  ltri_ref, utri_ref, out_ref, state_ref, m_ref):
    @pl.when(pl.program_id(1) == 0)
    def _():
        state_ref[...] = jnp.zeros_like(state_ref)
        m_ref[...] = jnp.zeros_like(m_ref)

    n_b, ck = q_ref.shape[0], q_ref.shape[1]
    causal = _causal(ck)
    ones_col = (lax.broadcasted_iota(jnp.int32, (ck, MLSTM_DV), 1) == 0).astype(F32)
    sl = [slice(h * HEAD_DK, (h + 1) * HEAD_DK) for h in range(N_HEADS)]
    sv = [slice(h * MLSTM_DV, (h + 1) * MLSTM_DV) for h in range(N_HEADS)]

    q, k, vx, bc, br, ig_c, ig_r = [], [], [], [], [], [], []
    for i in range(n_b):
        q_all = q_ref[i].astype(F32)
        k_all = k_ref[i].astype(F32) * (HEAD_DK ** -0.5)
        v_all = v_ref[i].astype(F32)
        gl = gate_ref[i] + brow_ref[...]
        bcum_c = _dot_f32(ltri_ref[...], -_softplus(-gl))
        gt = gate_t_ref[i] + bcol_ref[...]
        bcum_r = _dot_f32(-_softplus(-gt), utri_ref[...])
        for h in range(N_HEADS):
            q.append(q_all[:, sl[h]])
            k.append(k_all[:, sl[h]])
            vx.append(jnp.concatenate([v_all[:, sv[h]], ones_col], axis=1))
            bc.append(bcum_c[:, 8 + h:9 + h])
            br.append(bcum_r[8 + h:9 + h, :])
            ig_c.append(gl[:, h:h + 1])
            ig_r.append(gt[h:h + 1, :])

    chains = range(n_b * N_HEADS)
    log_w = [jnp.where(causal, bc[c] - br[c] + ig_r[c], -jnp.inf) for c in chains]
    m_intra = [jnp.max(log_w[c], axis=-1, keepdims=True) for c in chains]
    b_end = [bc[c][ck - 1:ck, :] for c in chains]
    lw_end = [b_end[c] - bc[c] + ig_c[c] for c in chains]
    m_end = [jnp.max(lw_end[c], axis=0, keepdims=True) for c in chains]
    m_s = [m_ref[c][:, 0:1] for c in chains]
    cx = [state_ref[c] for c in chains]
    m_t = [jnp.maximum(bc[c] + m_s[c], m_intra[c]) for c in chains]
    inter = [jnp.exp(bc[c] + m_s[c] - m_t[c]) for c in chains]
    qk = [_dot_nt(q[c], k[c]) for c in chains]
    qc = [_dot(q[c], cx[c]) for c in chains]
    s = [qk[c] * jnp.exp(log_w[c] - m_t[c]) for c in chains]
    numx = [inter[c] * qc[c] + _dot(s[c], vx[c]) for c in chains]
    m_new = [jnp.maximum(b_end[c] + m_s[c], m_end[c]) for c in chains]
    wk = [k[c] * jnp.exp(lw_end[c] - m_new[c]) for c in chains]
    for c in chains:
        state_ref[c] = jnp.exp(b_end[c] + m_s[c] - m_new[c]) * cx[c] + _dot_tn(wk[c], vx[c])
        m_ref[c] = jnp.broadcast_to(m_new[c], (1, GATE_LANES))
    hh = [numx[c][:, :MLSTM_DV] / jnp.maximum(jnp.abs(numx[c][:, MLSTM_DV:MLSTM_DV + 1]), jnp.exp(-m_t[c]))
          for c in chains]
    outs = [_rms(hh[c], norm_ref[:, sv[c % N_HEADS]]) for c in chains]
    for i in range(n_b):
        out_ref[i] = (jnp.concatenate(outs[i * N_HEADS:(i + 1) * N_HEADS], axis=1)
                      * _sigmoid(o_ref[i].astype(F32))).astype(out_ref.dtype)


def _mlstm(proj, gates, gates_t, gate_bias, mlstm_norm, b, t):
    ck = MLSTM_CHUNK
    nc = t // ck
    n_qk = N_HEADS * HEAD_DK
    n_v = N_HEADS * MLSTM_DV
    proj3 = proj.reshape(b, t, proj.shape[-1])
    ltri, utri = _tri_consts(ck)
    nb = _batch_per_step(b)
    const2 = lambda shape: pl.BlockSpec(shape, lambda i, c: (0,) * len(shape))
    return pl.pallas_call(
        _mlstm_kernel,
        grid=(b // nb, nc),
        in_specs=[pl.BlockSpec((nb, ck, n_qk), lambda i, c: (i, c, 0)),
                  pl.BlockSpec((nb, ck, n_qk), lambda i, c: (i, c, 1)),
                  pl.BlockSpec((nb, ck, n_v), lambda i, c: (i, c, 2 * n_qk // n_v)),
                  pl.BlockSpec((nb, ck, n_v), lambda i, c: (i, c, 2 * n_qk // n_v + 1)),
                  pl.BlockSpec((nb, ck, GATE_LANES), lambda i, c: (i, c, 0)),
                  pl.BlockSpec((nb, None, 16, ck), lambda i, c: (i, c, 0, 0)),
                  const2((1, GATE_LANES)), const2((16, 1)), const2((1, n_v)),
                  const2((ck, ck)), const2((ck, ck))],
        out_specs=pl.BlockSpec((nb, ck, n_v), lambda i, c: (i, c, 0)),
        out_shape=jax.ShapeDtypeStruct((b, t, n_v), BF16),
        scratch_shapes=[pltpu.VMEM((nb * N_HEADS, HEAD_DK, 2 * MLSTM_DV), F32),
                        pltpu.VMEM((nb * N_HEADS, 1, GATE_LANES), F32)],
        compiler_params=_cparams(("parallel", "arbitrary")),
        name="mlstm_mixer",
    )(proj3, proj3, proj3, proj3, gates.reshape(b, t, GATE_LANES), gates_t,
      _lane_row(gate_bias, 0), _sub_col(gate_bias, 0), mlstm_norm.reshape(1, n_v).astype(F32), ltri, utri)


def _gates_transposed(gates, b, t, ck):
    return jnp.transpose(gates[:, :16].reshape(b, t // ck, ck, 16), (0, 1, 3, 2))


def _outproj_ffn_kernel(x_ref, ma_ref, mb_ref, nw_ref, wo_hbm, w1_hbm, w3_hbm, w2_hbm, out_ref,
                        wo_ref, w1_ref, w3_ref, w2_ref, sems, *, ff_chunks):
    @pl.when(pl.program_id(0) == 0)
    def _():
        copies = [pltpu.make_async_copy(src, dst, sems.at[k]) for k, (src, dst) in enumerate(
            ((wo_hbm, wo_ref), (w1_hbm, w1_ref), (w3_hbm, w3_ref), (w2_hbm, w2_ref)))]
        for cp in copies:
            cp.start()
        for cp in copies:
            cp.wait()

    n_a = ma_ref.shape[1]
    x1 = (x_ref[...] + jnp.dot(ma_ref[...], wo_ref[:n_a, :], preferred_element_type=F32)
          + jnp.dot(mb_ref[...], wo_ref[n_a:, :], preferred_element_type=F32))
    h = _rms(x1, nw_ref[...]).astype(BF16)
    out_ref[...] = x1
    for lo, hi in ff_chunks:
        a = jnp.dot(h, w1_ref[:, lo:hi], preferred_element_type=F32)
        g = jnp.dot(h, w3_ref[:, lo:hi], preferred_element_type=F32)
        out_ref[...] += jnp.dot((_silu(a) * g).astype(BF16), w2_ref[lo:hi, :], preferred_element_type=F32)


def _ff_chunks(d_ff, max_chunk=1024):
    assert d_ff % MXU_WIDTH == 0 and max_chunk % MXU_WIDTH == 0
    bounds = list(range(0, d_ff, max_chunk)) + [d_ff]
    return tuple(zip(bounds[:-1], bounds[1:]))


def _outproj_ffn(x2d, mix_a, mix_b, w_out, ffn_norm, w1, w3, w2, tm=512):
    n, d = x2d.shape
    tm = min(tm, n)
    d_ff = w1.shape[1]
    n_a, n_b = mix_a.shape[1], mix_b.shape[1]
    row = lambda w: pl.BlockSpec((tm, w), lambda i: (i, 0))
    hbm = pl.BlockSpec(memory_space=pl.ANY)
    return pl.pallas_call(
        functools.partial(_outproj_ffn_kernel, ff_chunks=_ff_chunks(d_ff)),
        grid=(n // tm,),
        in_specs=[row(d), row(n_a), row(n_b), pl.BlockSpec((1, d), lambda i: (0, 0)), hbm, hbm, hbm, hbm],
        out_specs=row(d),
        out_shape=jax.ShapeDtypeStruct((n, d), F32),
        scratch_shapes=[pltpu.VMEM(w_out.shape, BF16), pltpu.VMEM(w1.shape, BF16), pltpu.VMEM(w3.shape, BF16),
                        pltpu.VMEM(w2.shape, BF16), pltpu.SemaphoreType.DMA((4,))],
        compiler_params=_cparams(("arbitrary",)),
        name="outproj_ffn",
    )(x2d, mix_a, mix_b, ffn_norm.reshape(1, d), w_out, w1, w3, w2)


def _outproj_router_kernel(x_ref, m_ref, wo_ref, nw_ref, r_ref, x3_ref, hn_ref, idx_ref, gate_ref):
    x3 = x_ref[...] + jnp.dot(m_ref[...], wo_ref[...], preferred_element_type=F32)
    x3_ref[...] = x3
    hn = _rms(x3, nw_ref[...])
    hn_ref[...] = hn
    hn_hi = hn.astype(BF16)
    hn_lo = (hn - hn_hi.astype(F32)).astype(BF16)
    r_hi, r_lo = r_ref[:, :GATE_LANES], r_ref[:, GATE_LANES:]
    logits = (jnp.dot(hn_hi, r_hi, preferred_element_type=F32)
              + (jnp.dot(hn_hi, r_lo, preferred_element_type=F32)
                 + jnp.dot(hn_lo, r_hi, preferred_element_type=F32)))
    lane = lax.broadcasted_iota(jnp.int32, logits.shape, 1)
    logits = jnp.where(lane < N_EXPERTS, logits, -jnp.inf)
    m0 = jnp.max(logits, axis=-1, keepdims=True)
    i0 = jnp.min(jnp.where(logits == m0, lane, GATE_LANES), axis=-1, keepdims=True)
    rest = jnp.where(lane == i0, -jnp.inf, logits)
    m1 = jnp.max(rest, axis=-1, keepdims=True)
    i1 = jnp.min(jnp.where(rest == m1, lane, GATE_LANES), axis=-1, keepdims=True)
    e1 = jnp.exp(m1 - m0)
    inv = 1.0 / (1.0 + e1)
    idx_ref[...] = jnp.where(lane == 0, i0, jnp.where(lane == 1, i1, 0))
    gate_ref[...] = jnp.where(lane == 0, inv, jnp.where(lane == 1, e1 * inv, 0.0))


def _outproj_router(x2d, mix, w_out, ffn_norm, router, tm=1024):
    n, d = x2d.shape
    tm = min(tm, n)
    n_m = mix.shape[1]
    router_pad = jnp.zeros((d, GATE_LANES), F32).at[:, :N_EXPERTS].set(router.astype(F32))
    router_hi = router_pad.astype(BF16)
    router_lo = (router_pad - router_hi.astype(F32)).astype(BF16)
    router_split = jnp.concatenate([router_hi, router_lo], axis=1)
    row = lambda w: pl.BlockSpec((tm, w), lambda i: (i, 0))
    return pl.pallas_call(
        _outproj_router_kernel,
        grid=(n // tm,),
        in_specs=[row(d), row(n_m),
                  pl.BlockSpec((n_m, d), lambda i: (0, 0)),
                  pl.BlockSpec((1, d), lambda i: (0, 0)),
                  pl.BlockSpec((d, 2 * GATE_LANES), lambda i: (0, 0))],
        out_specs=[row(d), row(d), row(GATE_LANES), row(GATE_LANES)],
        out_shape=[jax.ShapeDtypeStruct((n, d), F32), jax.ShapeDtypeStruct((n, d), F32),
                   jax.ShapeDtypeStruct((n, GATE_LANES), jnp.int32),
                   jax.ShapeDtypeStruct((n, GATE_LANES), F32)],
        compiler_params=_cparams(("parallel",)),
        name="outproj_router",
    )(x2d, mix, w_out, ffn_norm.reshape(1, d), router_split)


def _rank_kernel(idx_ref, tri_ref, rank_ref, count_ref, carry_ref):
    @pl.when(pl.program_id(0) == 0)
    def _():
        carry_ref[...] = jnp.zeros_like(carry_ref)

    idx = idx_ref[...]
    lane = lax.broadcasted_iota(jnp.int32, idx.shape, 1)
    e0, e1 = idx[:, 0:1], idx[:, 1:2]
    member = ((lane == e0) | (lane == e1)).astype(F32)
    before = _dot(tri_ref[...], member) + carry_ref[...]
    r0 = jnp.sum(jnp.where(lane == e0, before, 0.0), axis=-1, keepdims=True)
    r1 = jnp.sum(jnp.where(lane == e1, before, 0.0), axis=-1, keepdims=True)
    rank_ref[...] = jnp.where(lane == 0, r0, jnp.where(lane == 1, r1, 0.0)).astype(jnp.int32)
    carry_ref[...] += jnp.sum(member, axis=0, keepdims=True)
    count_ref[...] = carry_ref[...].astype(jnp.int32)


def _expert_ranks(ridx, tm=512):
    n = ridx.shape[0]
    tm = min(tm, n)
    i = np.arange(tm)
    tri = jnp.asarray((i[None, :] < i[:, None]).astype(np.float32)).astype(BF16)
    return pl.pallas_call(
        _rank_kernel,
        grid=(n // tm,),
        in_specs=[pl.BlockSpec((tm, GATE_LANES), lambda i: (i, 0)),
                  pl.BlockSpec((tm, tm), lambda i: (0, 0))],
        out_specs=[pl.BlockSpec((tm, GATE_LANES), lambda i: (i, 0)),
                   pl.BlockSpec((1, GATE_LANES), lambda i: (0, 0))],
        out_shape=[jax.ShapeDtypeStruct((n, GATE_LANES), jnp.int32),
                   jax.ShapeDtypeStruct((1, GATE_LANES), jnp.int32)],
        scratch_shapes=[pltpu.VMEM((1, GATE_LANES), F32)],
        compiler_params=_cparams(("arbitrary",)),
        name="expert_ranks",
    )(ridx, tri)


def _row_copy(src_ref, src_row, dst_ref, dst_row, sem):
    return pltpu.make_async_copy(src_ref.at[pl.ds(src_row, 1)], dst_ref.at[pl.ds(dst_row, 1)], sem)


def _dispatch_kernel(dest_ref, fill_ref, hn_ref, slots_ref, buf_ref, zero_ref, load_sems, row_sems, fill_sem,
                     *, tm, n_tok, n_fill):
    i = pl.program_id(0)

    @pl.when(i == 0)
    def _():
        zero_ref[...] = jnp.zeros_like(zero_ref)
        for wait in (False, True):
            for s in range(n_fill):
                def fill(r, carry):
                    cp = _row_copy(zero_ref, 0, slots_ref, r, fill_sem)
                    cp.wait() if wait else cp.start()
                    return carry
                lax.fori_loop(fill_ref[2 * s], fill_ref[2 * s + 1], fill, 0)

    last = pl.num_programs(0) - 1
    base = i * tm
    cur = i % 3
    rsem = i % 2

    def load(tile, b):
        return pltpu.make_async_copy(hn_ref.at[pl.ds(tile * tm, tm)], buf_ref.at[b], load_sems.at[b])

    def drain(s):
        for _ in range(2):
            pltpu.make_async_copy(buf_ref.at[0], slots_ref.at[pl.ds(0, tm)], row_sems.at[s]).wait()

    @pl.when(i == 0)
    def _():
        load(0, 0).start()

    @pl.when(i < last)
    def _():
        load(i + 1, (i + 1) % 3).start()

    load(i, cur).wait()

    def start(r, carry):
        _row_copy(buf_ref.at[cur], r, slots_ref, dest_ref[base + r], row_sems.at[rsem]).start()
        _row_copy(buf_ref.at[cur], r, slots_ref, dest_ref[n_tok + base + r], row_sems.at[rsem]).start(priority=1)
        return carry

    lax.fori_loop(0, tm, start, 0, unroll=DMA_UNROLL)

    @pl.when(i > 0)
    def _():
        drain(1 - rsem)

    @pl.when(i == last)
    def _():
        drain(rsem)


def _dispatch(dest_flat, fill_ranges, hn, n_slots, tm=512):
    n, d = hn.shape
    tm = min(tm, n)
    grid_spec = pltpu.PrefetchScalarGridSpec(
        num_scalar_prefetch=2,
        grid=(n // tm,),
        in_specs=[pl.BlockSpec(memory_space=pl.ANY)],
        out_specs=pl.BlockSpec(memory_space=pl.ANY),
        scratch_shapes=[pltpu.VMEM((3, tm, d), hn.dtype), pltpu.VMEM((8, d), hn.dtype),
                        pltpu.SemaphoreType.DMA((3,)), pltpu.SemaphoreType.DMA((2,)),
                        pltpu.SemaphoreType.DMA(())],
    )
    return pl.pallas_call(
        functools.partial(_dispatch_kernel, tm=tm, n_tok=n, n_fill=fill_ranges.shape[0] // 2),
        grid_spec=grid_spec,
        out_shape=jax.ShapeDtypeStruct((n_slots, d), hn.dtype),
        compiler_params=_cparams(("arbitrary",)),
        name="row_dispatch",
    )(dest_flat, fill_ranges, hn)


def _moe_kernel(be_ref, nused_ref, x_ref, w1_hbm, w3_hbm, w2_hbm, out_ref, w1_ref, w3_ref, w2_ref, sems,
                *, ff_chunks):
    i = pl.program_id(0)
    used = i < nused_ref[0]
    expert = be_ref[i]
    prev_expert = be_ref[jnp.maximum(i - 1, 0)]

    @pl.when(used & ((i == 0) | (expert != prev_expert)))
    def _():
        copies = [pltpu.make_async_copy(src.at[expert], dst, sems.at[k]) for k, (src, dst) in enumerate(
            ((w1_hbm, w1_ref), (w3_hbm, w3_ref), (w2_hbm, w2_ref)))]
        for cp in copies:
            cp.start()
        for cp in copies:
            cp.wait()

    @pl.when(used)
    def _():
        x = x_ref[...].astype(BF16)
        for n_done, (lo, hi) in enumerate(ff_chunks):
            a = jnp.dot(x, w1_ref[:, lo:hi], preferred_element_type=F32)
            g = jnp.dot(x, w3_ref[:, lo:hi], preferred_element_type=F32)
            y = jnp.dot((_silu(a) * g).astype(BF16), w2_ref[lo:hi, :], preferred_element_type=F32)
            if n_done == 0:
                out_ref[...] = y
            else:
                out_ref[...] += y

    @pl.when(jnp.logical_not(used))
    def _():
        out_ref[...] = jnp.zeros_like(out_ref)


def _moe_ffn(block_expert, n_used, slots, w1, w3, w2):
    n_slots, d = slots.shape
    n_blocks = n_slots // MOE_BLOCK
    d_ff = w1.shape[2]
    hbm = pl.BlockSpec(memory_space=pl.ANY)
    grid_spec = pltpu.PrefetchScalarGridSpec(
        num_scalar_prefetch=2,
        grid=(n_blocks,),
        in_specs=[pl.BlockSpec((MOE_BLOCK, d), lambda i, be, nu: (jnp.minimum(i, nu[0] - 1), 0)), hbm, hbm, hbm],
        out_specs=pl.BlockSpec((MOE_BLOCK, d), lambda i, be, nu: (i, 0)),
        scratch_shapes=[pltpu.VMEM((d, d_ff), BF16), pltpu.VMEM((d, d_ff), BF16), pltpu.VMEM((d_ff, d), BF16),
                        pltpu.SemaphoreType.DMA((3,))],
    )
    return pl.pallas_call(
        functools.partial(_moe_kernel, ff_chunks=_ff_chunks(d_ff)),
        grid_spec=grid_spec,
        out_shape=jax.ShapeDtypeStruct((n_slots, d), F32),
        compiler_params=_cparams(("arbitrary",)),
        name="moe_ffn",
    )(block_expert, n_used, slots, w1, w3, w2)


def _combine_kernel(dest_ref, x_ref, gate_ref, nw_ref, y_ref, out_ref, ybuf_ref, sems, *, tm, n_tok):
    i = pl.program_id(0)
    slot = i % 2

    def gather(tile, s):
        base = tile * tm

        def start(r, carry):
            _row_copy(y_ref, dest_ref[base + r], ybuf_ref.at[s, 0], r, sems.at[s]).start()
            _row_copy(y_ref, dest_ref[n_tok + base + r], ybuf_ref.at[s, 1], r, sems.at[s]).start(priority=1)
            return carry

        lax.fori_loop(0, tm, start, 0, unroll=DMA_UNROLL)

    @pl.when(i == 0)
    def _():
        gather(0, 0)

    @pl.when(i + 1 < pl.num_programs(0))
    def _():
        gather(i + 1, 1 - slot)

    for k in range(2):
        pltpu.make_async_copy(y_ref.at[pl.ds(0, tm)], ybuf_ref.at[slot, k], sems.at[slot]).wait()
    gate = gate_ref[...]
    y = x_ref[...] + (ybuf_ref[slot, 0] * gate[:, 0:1] + ybuf_ref[slot, 1] * gate[:, 1:2])
    out_ref[...] = _rms(y, nw_ref[...])


def _combine(dest_flat, x2d, rgate, final_norm, yb, tm=512):
    n, d = x2d.shape
    tm = min(tm, n)
    grid_spec = pltpu.PrefetchScalarGridSpec(
        num_scalar_prefetch=1,
        grid=(n // tm,),
        in_specs=[pl.BlockSpec((tm, d), lambda i, dest: (i, 0)),
                  pl.BlockSpec((tm, GATE_LANES), lambda i, dest: (i, 0)),
                  pl.BlockSpec((1, d), lambda i, dest: (0, 0)),
                  pl.BlockSpec(memory_space=pl.ANY)],
        out_specs=pl.BlockSpec((tm, d), lambda i, dest: (i, 0)),
        scratch_shapes=[pltpu.VMEM((2, 2, tm, d), F32), pltpu.SemaphoreType.DMA((2,))],
    )
    return pl.pallas_call(
        functools.partial(_combine_kernel, tm=tm, n_tok=n),
        grid_spec=grid_spec,
        out_shape=jax.ShapeDtypeStruct((n, d), F32),
        compiler_params=_cparams(("arbitrary",)),
        name="combine_norm",
    )(dest_flat, x2d, rgate, final_norm.reshape(1, d), yb)


def _split_in_proj(w_in, lo, hi):
    main = jnp.concatenate([w_in[:, :lo], w_in[:, hi:]], axis=1)
    gate = jnp.pad(w_in[:, lo:hi], ((0, 0), (0, GATE_LANES - (hi - lo))))
    return jnp.concatenate([main, gate], axis=1).astype(BF16), main.shape[1]


def kernel(x, mix_norm_0, w_in_0, conv_w_0, a_log_0, dt_bias_0, gdn_norm_0, ret_norm_0, w_out_0, ffn_norm_0,
           ffn_w1_0, ffn_w3_0, ffn_w2_0, mix_norm_1, w_in_1, gate_bias_1, mlstm_norm_1, w_out_1, ffn_norm_1,
           router_1, exp_w1_1, exp_w3_1, exp_w2_1, final_norm):
    b, t, d = x.shape
    n = b * t
    x2d = x.reshape(n, d)

    gdn_qkv = N_HEADS * (2 * HEAD_DK + GDN_DV)
    w0, n_main0 = _split_in_proj(w_in_0, gdn_qkv, gdn_qkv + 2 * N_HEADS)
    proj0, gates0 = _norm_proj(x2d, mix_norm_0, w0, n_main0)
    o_gdn = _gdn(proj0, gates0, _gates_transposed(gates0, b, t, CHUNK), conv_w_0, a_log_0, dt_bias_0, gdn_norm_0, b, t)
    o_ret = _retention(proj0, ret_norm_0, b, t)
    x2 = _outproj_ffn(x2d, o_gdn.reshape(n, -1), o_ret.reshape(n, -1), w_out_0.astype(BF16), ffn_norm_0,
                      ffn_w1_0.astype(BF16), ffn_w3_0.astype(BF16), ffn_w2_0.astype(BF16))

    n_main1 = w_in_1.shape[1] - 2 * N_HEADS
    w1, _ = _split_in_proj(w_in_1, n_main1, n_main1 + 2 * N_HEADS)
    proj1, gates1 = _norm_proj(x2, mix_norm_1, w1, n_main1)
    h_mix = _mlstm(proj1, gates1, _gates_transposed(gates1, b, t, MLSTM_CHUNK), gate_bias_1, mlstm_norm_1, b, t)
    x3, hn, ridx, rgate = _outproj_router(x2, h_mix.reshape(n, -1), w_out_1.astype(BF16), ffn_norm_1, router_1)

    rank, counts = _expert_ranks(ridx)
    counts = counts[0, :N_EXPERTS]
    padded = (counts + MOE_BLOCK - 1) // MOE_BLOCK * MOE_BLOCK
    pad_end = jnp.cumsum(padded)
    pad_start = pad_end - padded
    dest_flat = jnp.concatenate([pad_start[ridx[:, 0]] + rank[:, 0], pad_start[ridx[:, 1]] + rank[:, 1]])
    n_blocks = -(-(2 * n) // MOE_BLOCK) + N_EXPERTS
    block_row0 = jnp.arange(n_blocks, dtype=jnp.int32) * MOE_BLOCK
    block_expert = jnp.minimum(jnp.sum((pad_end[None, :] <= block_row0[:, None]).astype(jnp.int32), axis=1),
                               N_EXPERTS - 1)
    n_used = (pad_end[-1:] // MOE_BLOCK).astype(jnp.int32)

    n_slots = n_blocks * MOE_BLOCK
    fill_lo = jnp.concatenate([pad_start + counts, pad_end[-1:]])
    fill_hi = jnp.concatenate([pad_end, jnp.full((1,), n_slots, jnp.int32)])
    fill_ranges = jnp.stack([fill_lo, fill_hi], axis=1).reshape(-1).astype(jnp.int32)
    slots = _dispatch(dest_flat, fill_ranges, hn, n_slots)
    yb = _moe_ffn(block_expert, n_used, slots, exp_w1_1.astype(BF16), exp_w3_1.astype(BF16),
                  exp_w2_1.astype(BF16))
    out = _combine(dest_flat, x3, rgate, final_norm, yb)
    return out.reshape(b, t, d)
```

```python
import functools
import math

import numpy as np
import jax
import jax.numpy as jnp
from jax import lax
from jax.experimental import pallas as pl
from jax.experimental.pallas import tpu as pltpu

F32 = jnp.float32
BF16 = jnp.bfloat16
HIGHEST = lax.Precision.HIGHEST

EPS = 1e-6
CHUNK = 128
RET_CHUNK = 128
MLSTM_CHUNK = 128
CONV_WIDTH = 4
N_HEADS = 8
HEAD_DK = 64
GDN_DV = 64
RET_DV = 64
MLSTM_DV = 128
ROPE_BASE = 10000.0
N_EXPERTS = 8
MOE_BLOCK = 512
GATE_LANES = 128
MXU_WIDTH = 256
VMEM_LIMIT = 52 * 1024 * 1024
DMA_UNROLL = 8


def _cparams(sem):
    return pltpu.CompilerParams(dimension_semantics=sem, vmem_limit_bytes=VMEM_LIMIT)


def _dot(a, b):
    return jnp.dot(a.astype(BF16), b.astype(BF16), preferred_element_type=F32)


def _dot_nt(a, b):
    return lax.dot_general(a.astype(BF16), b.astype(BF16), (((1,), (1,)), ((), ())),
                           preferred_element_type=F32)


def _dot_tn(a, b):
    return lax.dot_general(a.astype(BF16), b.astype(BF16), (((0,), (0,)), ((), ())),
                           preferred_element_type=F32)


def _dot_f32(a, b):
    return jnp.dot(a, b, precision=HIGHEST, preferred_element_type=F32)


def _rms(x, w):
    return x * lax.rsqrt(jnp.mean(x * x, axis=-1, keepdims=True) + EPS) * w


def _silu(x):
    return x * (1.0 / (1.0 + jnp.exp(-x)))


def _sigmoid(x):
    return 1.0 / (1.0 + jnp.exp(-x))


def _softplus(x):
    return jnp.maximum(x, 0.0) + jnp.log1p(jnp.exp(-jnp.abs(x)))


def _causal(n):
    r = lax.broadcasted_iota(jnp.int32, (n, n), 0)
    c = lax.broadcasted_iota(jnp.int32, (n, n), 1)
    return r >= c


def _norm_proj_kernel(x_ref, nw_ref, w_ref, main_ref, gate_ref, *, n_main, col_chunk):
    h = _rms(x_ref[...], nw_ref[...]).astype(BF16)
    for c0 in range(0, n_main, col_chunk):
        main_ref[:, c0:c0 + col_chunk] = jnp.dot(
            h, w_ref[:, c0:c0 + col_chunk], preferred_element_type=F32).astype(BF16)
    gate_ref[...] = jnp.dot(h, w_ref[:, n_main:], preferred_element_type=F32)


def _norm_proj(x2d, norm_w, w_cat, n_main, tm=512, col_chunk=512):
    n, d = x2d.shape
    tm = min(tm, n)
    n_cat = w_cat.shape[1]
    return pl.pallas_call(
        functools.partial(_norm_proj_kernel, n_main=n_main, col_chunk=col_chunk),
        grid=(n // tm,),
        in_specs=[pl.BlockSpec((tm, d), lambda i: (i, 0)),
                  pl.BlockSpec((1, d), lambda i: (0, 0)),
                  pl.BlockSpec((d, n_cat), lambda i: (0, 0))],
        out_specs=[pl.BlockSpec((tm, n_main), lambda i: (i, 0)),
                   pl.BlockSpec((tm, GATE_LANES), lambda i: (i, 0))],
        out_shape=[jax.ShapeDtypeStruct((n, n_main), BF16),
                   jax.ShapeDtypeStruct((n, GATE_LANES), F32)],
        compiler_params=_cparams(("parallel",)),
        name="norm_proj",
    )(x2d, norm_w.reshape(1, d), w_cat)


def _gdn_kernel(qkv_ref, z_ref, gate_ref, gate_t_ref, conv_ref, arow_ref, dtrow_ref, acol_ref, dtcol_ref,
                norm_ref, ltri_ref, utri_ref, lvl_ref, ones_ref, out_ref, state_ref, xbuf_ref):
    n_b = qkv_ref.shape[0]
    n_qk = N_HEADS * HEAD_DK

    @pl.when(pl.program_id(1) == 0)
    def _():
        state_ref[...] = jnp.zeros_like(state_ref)
        xbuf_ref[:, 0:8, :] = jnp.zeros((n_b, 8, xbuf_ref.shape[2]), F32)

    causal = _causal(CHUNK)
    eye = (lax.broadcasted_iota(jnp.int32, (CHUNK, CHUNK), 0)
           == lax.broadcasted_iota(jnp.int32, (CHUNK, CHUNK), 1)).astype(F32)
    sl = [slice(h * HEAD_DK, (h + 1) * HEAD_DK) for h in range(N_HEADS)]

    q, k, v, bc, gcc, gcr = [], [], [], [], [], []
    for i in range(n_b):
        xbuf_ref[i, 8:8 + CHUNK, :] = qkv_ref[i].astype(F32)
        pre = None
        for j in range(CONV_WIDTH):
            term = xbuf_ref[i, pl.ds(8 - (CONV_WIDTH - 1) + j, CHUNK), :] * conv_ref[j:j + 1, :]
            pre = term if pre is None else pre + term
        xbuf_ref[i, 0:8, :] = xbuf_ref[i, CHUNK:CHUNK + 8, :]
        act = _silu(pre)
        q_all, k_all, v_all = act[:, :n_qk], act[:, n_qk:2 * n_qk], act[:, 2 * n_qk:]
        q_all = q_all * lax.rsqrt(_dot(q_all * q_all, ones_ref[...]) + EPS) * (HEAD_DK ** -0.5)
        k_all = k_all * lax.rsqrt(_dot(k_all * k_all, ones_ref[...]) + EPS)
        gl = gate_ref[i]
        beta_c = _sigmoid(gl)
        g_c = -jnp.exp(arow_ref[...]) * _softplus(gl + dtrow_ref[...])
        gc_c = _dot_f32(ltri_ref[...], g_c)
        g_r = -jnp.exp(acol_ref[...]) * _softplus(gate_t_ref[i] + dtcol_ref[...])
        gc_r = _dot_f32(g_r, utri_ref[...])
        for h in range(N_HEADS):
            q.append(q_all[:, sl[h]])
            k.append(k_all[:, sl[h]])
            v.append(v_all[:, sl[h]])
            bc.append(beta_c[:, h:h + 1])
            gcc.append(gc_c[:, 8 + h:9 + h])
            gcr.append(gc_r[8 + h:9 + h, :])

    chains = range(n_b * N_HEADS)
    decay = [jnp.exp(jnp.where(causal, gcc[c] - gcr[c], -jnp.inf)) for c in chains]
    kb = [k[c] * bc[c] for c in chains]
    scores = [_dot_nt(jnp.concatenate([kb[c], q[c]], axis=0), k[c]) for c in chains]
    lmat = [scores[c][:CHUNK] * decay[c] for c in chains]
    qk = [scores[c][CHUNK:] * decay[c] for c in chains]
    tinv = [eye - lmat[c] * lvl_ref[0].astype(F32) for c in chains]
    lmat_mxu = [lmat[c].astype(BF16) for c in chains]
    for lvl in range(1, lvl_ref.shape[0]):
        inner = [_dot(lmat_mxu[c] * lvl_ref[lvl], tinv[c]) for c in chains]
        tinv = [tinv[c] - _dot(tinv[c], inner[c]) for c in chains]
    egc = [jnp.exp(gcc[c]) for c in chains]
    uw = [_dot(tinv[c], jnp.concatenate([v[c] * bc[c], kb[c] * egc[c]], axis=1)) for c in chains]
    state = [state_ref[c] for c in chains]
    v_new = [uw[c][:, :GDN_DV] - _dot(uw[c][:, GDN_DV:], state[c]) for c in chains]
    o = [_dot(q[c] * egc[c], state[c]) + _dot(qk[c], v_new[c]) for c in chains]
    g_last = [gcc[c][CHUNK - 1:CHUNK, :] for c in chains]
    k_dec = [k[c] * jnp.exp(g_last[c] - gcc[c]) for c in chains]
    for c in chains:
        state_ref[c] = state[c] * jnp.exp(g_last[c]) + _dot_tn(k_dec[c], v_new[c])
    for i in range(n_b):
        o_all = jnp.concatenate(o[i * N_HEADS:(i + 1) * N_HEADS], axis=1)
        mean_sq = _dot(o_all * o_all, ones_ref[...]) * (1.0 / GDN_DV)
        out_ref[i] = (o_all * lax.rsqrt(mean_sq + EPS) * norm_ref[...]
                      * _silu(z_ref[i].astype(F32))).astype(out_ref.dtype)


def _level_masks():
    i = np.arange(CHUNK)[:, None]
    j = np.arange(CHUNK)[None, :]
    masks = []
    for lvl in range(CHUNK.bit_length() - 1):
        same_parent = (i >> (lvl + 1)) == (j >> (lvl + 1))
        masks.append((same_parent & (((i >> lvl) & 1) == 1) & (((j >> lvl) & 1) == 0)).astype(np.float32))
    return jnp.asarray(np.stack(masks)).astype(BF16)


def _tri_consts(ck):
    i = np.arange(ck)[:, None]
    j = np.arange(ck)[None, :]
    return jnp.asarray((j <= i).astype(np.float32)), jnp.asarray((i <= j).astype(np.float32))


def _lane_row(vals, offset):
    return jnp.zeros((1, GATE_LANES), F32).at[0, offset:offset + vals.shape[0]].set(vals.astype(F32))


def _sub_col(vals, offset):
    return jnp.zeros((16, 1), F32).at[offset:offset + vals.shape[0], 0].set(vals.astype(F32))


def _head_ones(n_heads, width):
    g = np.arange(n_heads * width) // width
    return jnp.asarray((g[:, None] == g[None, :]).astype(np.float32)).astype(BF16)


def _batch_per_step(b):
    return 2 if b % 2 == 0 else 1


def _gdn(proj, gates, gates_t, conv_w, a_log, dt_bias, gdn_norm, b, t):
    nc = t // CHUNK
    nb = _batch_per_step(b)
    n_qkv = N_HEADS * (2 * HEAD_DK + GDN_DV)
    n_z = N_HEADS * GDN_DV
    proj3 = proj.reshape(b, t, proj.shape[-1])
    ltri, utri = _tri_consts(CHUNK)
    const2 = lambda shape: pl.BlockSpec(shape, lambda i, c: (0,) * len(shape))
    return pl.pallas_call(
        _gdn_kernel,
        grid=(b // nb, nc),
        in_specs=[pl.BlockSpec((nb, CHUNK, n_qkv), lambda i, c: (i, c, 0)),
                  pl.BlockSpec((nb, CHUNK, n_z), lambda i, c: (i, c, n_qkv // n_z)),
                  pl.BlockSpec((nb, CHUNK, GATE_LANES), lambda i, c: (i, c, 0)),
                  pl.BlockSpec((nb, None, 16, CHUNK), lambda i, c: (i, c, 0, 0)),
                  const2((CONV_WIDTH, n_qkv)), const2((1, GATE_LANES)), const2((1, GATE_LANES)),
                  const2((16, 1)), const2((16, 1)), const2((1, n_z)),
                  const2((CHUNK, CHUNK)), const2((CHUNK, CHUNK)),
                  const2((CHUNK.bit_length() - 1, CHUNK, CHUNK)),
                  const2((n_z, n_z))],
        out_specs=pl.BlockSpec((nb, CHUNK, n_z), lambda i, c: (i, c, 0)),
        out_shape=jax.ShapeDtypeStruct((b, t, n_z), BF16),
        scratch_shapes=[pltpu.VMEM((nb * N_HEADS, HEAD_DK, GDN_DV), F32),
                        pltpu.VMEM((nb, CHUNK + 8, n_qkv), F32)],
        compiler_params=_cparams(("parallel", "arbitrary")),
        name="gdn_mixer",
    )(proj3, proj3, gates.reshape(b, t, GATE_LANES), gates_t, conv_w.astype(F32),
      _lane_row(a_log, 8), _lane_row(dt_bias, 8), _sub_col(a_log, 8), _sub_col(dt_bias, 8),
      jnp.tile(gdn_norm.astype(F32), N_HEADS).reshape(1, n_z), ltri, utri, _level_masks(),
      _head_ones(N_HEADS, HEAD_DK))


def _ret_kernel(q_ref, k_ref, v_ref, g_ref, cos_ref, sin_ref, decay_ref, qs_ref, ks_ref, cd_ref, norm_ref,
                ones_ref, out_ref, state_ref):
    @pl.when(pl.program_id(1) == 0)
    def _():
        state_ref[...] = jnp.zeros_like(state_ref)

    width = N_HEADS * HEAD_DK
    half = HEAD_DK // 2
    first_half = (lax.broadcasted_iota(jnp.int32, (q_ref.shape[1], width), 1) % HEAD_DK) < half

    def rot(t_in):
        swapped = jnp.where(first_half, pltpu.roll(t_in, width - half, 1), pltpu.roll(t_in, half, 1))
        return t_in * cos_ref[...] + swapped * sin_ref[...]

    n_b = q_ref.shape[0]
    sl = [slice(h * HEAD_DK, (h + 1) * HEAD_DK) for h in range(N_HEADS)]
    q, k, v, q_in, k_st = [], [], [], [], []
    for i in range(n_b):
        q_all = rot(q_ref[i].astype(F32))
        k_all = rot(k_ref[i].astype(F32)) * (HEAD_DK ** -0.5)
        v_all = v_ref[i].astype(F32)
        qi_all = q_all * qs_ref[...]
        ks_all = k_all * ks_ref[...]
        for h in range(N_HEADS):
            q.append(q_all[:, sl[h]])
            k.append(k_all[:, sl[h]])
            v.append(v_all[:, sl[h]])
            q_in.append(qi_all[:, sl[h]])
            k_st.append(ks_all[:, sl[h]])
    chains = range(n_b * N_HEADS)
    state = [state_ref[c] for c in chains]
    scores = [_dot_nt(q[c], k[c]) * decay_ref[c % N_HEADS] for c in chains]
    inter = [_dot(q_in[c], state[c]) for c in chains]
    o = [_dot(scores[c], v[c]) + inter[c] for c in chains]
    for c in chains:
        state_ref[c] = state[c] * cd_ref[c % N_HEADS] + _dot_tn(k_st[c], v[c])
    inv_dv = 1.0 / RET_DV
    for i in range(n_b):
        o_all = jnp.concatenate(o[i * N_HEADS:(i + 1) * N_HEADS], axis=1)
        xc = o_all - _dot(o_all, ones_ref[...]) * inv_dv
        var = _dot(xc * xc, ones_ref[...]) * inv_dv
        out_ref[i] = (xc * lax.rsqrt(var + EPS) * norm_ref[...]
                      * _silu(g_ref[i].astype(F32))).astype(out_ref.dtype)


def _retention(proj, ret_norm, b, t):
    ck = RET_CHUNK
    nc = t // ck
    width = N_HEADS * HEAD_DK
    half = HEAD_DK // 2
    proj3 = proj.reshape(b, t, proj.shape[-1])
    pos = jnp.arange(t, dtype=F32)
    inv_freq = jnp.power(ROPE_BASE, -jnp.arange(half, dtype=F32) / half)
    ang = pos[:, None] * inv_freq[None, :]
    cos, sin = jnp.cos(ang), jnp.sin(ang)
    cos_t = jnp.tile(jnp.concatenate([cos, cos], axis=1), (1, N_HEADS))
    sin_t = jnp.tile(jnp.concatenate([-sin, sin], axis=1), (1, N_HEADS))
    log_gamma = jnp.log1p(-jnp.exp2(-5.0 - jnp.arange(N_HEADS, dtype=F32)))
    idx = jnp.arange(ck, dtype=F32)
    causal = jnp.tril(jnp.ones((ck, ck), bool))
    decay = jnp.exp(jnp.where(causal, (idx[:, None] - idx[None, :]) * log_gamma[:, None, None], -jnp.inf))
    qs = jnp.repeat(jnp.exp((idx + 1.0) * log_gamma[:, None]).T, HEAD_DK, axis=1)
    ks = jnp.repeat(jnp.exp((ck - 1.0 - idx) * log_gamma[:, None]).T, HEAD_DK, axis=1)
    cd = jnp.broadcast_to(jnp.exp(ck * log_gamma)[:, None, None], (N_HEADS, 1, RET_DV))
    base = (N_HEADS * (2 * HEAD_DK + GDN_DV) + N_HEADS * GDN_DV) // width
    nb = _batch_per_step(b)
    col = lambda k: pl.BlockSpec((nb, ck, width), lambda i, c: (i, c, base + k))
    const2 = lambda shape: pl.BlockSpec(shape, lambda i, c: (0,) * len(shape))
    return pl.pallas_call(
        _ret_kernel,
        grid=(b // nb, nc),
        in_specs=[col(0), col(1), col(2), col(3),
                  pl.BlockSpec((ck, width), lambda i, c: (c, 0)),
                  pl.BlockSpec((ck, width), lambda i, c: (c, 0)),
                  const2((N_HEADS, ck, ck)), const2((ck, width)), const2((ck, width)),
                  const2((N_HEADS, 1, RET_DV)), const2((1, width)), const2((width, width))],
        out_specs=pl.BlockSpec((nb, ck, width), lambda i, c: (i, c, 0)),
        out_shape=jax.ShapeDtypeStruct((b, t, width), BF16),
        scratch_shapes=[pltpu.VMEM((nb * N_HEADS, HEAD_DK, RET_DV), F32)],
        compiler_params=_cparams(("parallel", "arbitrary")),
        name="retention_mixer",
    )(proj3, proj3, proj3, proj3, cos_t, sin_t, decay, qs, ks, cd, ret_norm.reshape(1, width).astype(F32),
      _head_ones(N_HEADS, RET_DV))


def _mlstm_kernel(q_ref, k_ref, v_ref, o_ref, gate_ref, gate_t_ref, brow_ref, bcol_ref, norm_ref,
                  ltri_ref, utri_ref, out_ref, state_ref, m_ref):
    @pl.when(pl.program_id(1) == 0)
    def _():
        state_ref[...] = jnp.zeros_like(state_ref)
        m_ref[...] = jnp.zeros_like(m_ref)

    n_b, ck = q_ref.shape[0], q_ref.shape[1]
    causal = _causal(ck)
    ones_col = (lax.broadcasted_iota(jnp.int32, (ck, MLSTM_DV), 1) == 0).astype(F32)
    sl = [slice(h * HEAD_DK, (h + 1) * HEAD_DK) for h in range(N_HEADS)]
    sv = [slice(h * MLSTM_DV, (h + 1) * MLSTM_DV) for h in range(N_HEADS)]

    q, k, vx, bc, br, ig_c, ig_r = [], [], [], [], [], [], []
    for i in range(n_b):
        q_all = q_ref[i].astype(F32)
        k_all = k_ref[i].astype(F32) * (HEAD_DK ** -0.5)
        v_all = v_ref[i].astype(F32)
        gl = gate_ref[i] + brow_ref[...]
        bcum_c = _dot_f32(ltri_ref[...], -_softplus(-gl))
        gt = gate_t_ref[i] + bcol_ref[...]
        bcum_r = _dot_f32(-_softplus(-gt), utri_ref[...])
        for h in range(N_HEADS):
            q.append(q_all[:, sl[h]])
            k.append(k_all[:, sl[h]])
            vx.append(jnp.concatenate([v_all[:, sv[h]], ones_col], axis=1))
            bc.append(bcum_c[:, 8 + h:9 + h])
            br.append(bcum_r[8 + h:9 + h, :])
            ig_c.append(gl[:, h:h + 1])
            ig_r.append(gt[h:h + 1, :])

    chains = range(n_b * N_HEADS)
    log_w = [jnp.where(causal, bc[c] - br[c] + ig_r[c], -jnp.inf) for c in chains]
    m_intra = [jnp.max(log_w[c], axis=-1, keepdims=True) for c in chains]
    b_end = [bc[c][ck - 1:ck, :] for c in chains]
    lw_end = [b_end[c] - bc[c] + ig_c[c] for c in chains]
    m_end = [jnp.max(lw_end[c], axis=0, keepdims=True) for c in chains]
    m_s = [m_ref[c][:, 0:1] for c in chains]
    cx = [state_ref[c] for c in chains]
    m_t = [jnp.maximum(bc[c] + m_s[c], m_intra[c]) for c in chains]
    inter = [jnp.exp(bc[c] + m_s[c] - m_t[c]) for c in chains]
    qk = [_dot_nt(q[c], k[c]) for c in chains]
    qc = [_dot(q[c], cx[c]) for c in chains]
    s = [qk[c] * jnp.exp(log_w[c] - m_t[c]) for c in chains]
    numx = [inter[c] * qc[c] + _dot(s[c], vx[c]) for c in chains]
    m_new = [jnp.maximum(b_end[c] + m_s[c], m_end[c]) for c in chains]
    wk = [k[c] * jnp.exp(lw_end[c] - m_new[c]) for c in chains]
    for c in chains:
        state_ref[c] = jnp.exp(b_end[c] + m_s[c] - m_new[c]) * cx[c] + _dot_tn(wk[c], vx[c])
        m_ref[c] = jnp.broadcast_to(m_new[c], (1, GATE_LANES))
    hh = [numx[c][:, :MLSTM_DV] / jnp.maximum(jnp.abs(numx[c][:, MLSTM_DV:MLSTM_DV + 1]), jnp.exp(-m_t[c]))
          for c in chains]
    outs = [_rms(hh[c], norm_ref[:, sv[c % N_HEADS]]) for c in chains]
    for i in range(n_b):
        out_ref[i] = (jnp.concatenate(outs[i * N_HEADS:(i + 1) * N_HEADS], axis=1)
                      * _sigmoid(o_ref[i].astype(F32))).astype(out_ref.dtype)


def _mlstm(proj, gates, gates_t, gate_bias, mlstm_norm, b, t):
    ck = MLSTM_CHUNK
    nc = t // ck
    n_qk = N_HEADS * HEAD_DK
    n_v = N_HEADS * MLSTM_DV
    proj3 = proj.reshape(b, t, proj.shape[-1])
    ltri, utri = _tri_consts(ck)
    nb = _batch_per_step(b)
    const2 = lambda shape: pl.BlockSpec(shape, lambda i, c: (0,) * len(shape))
    return pl.pallas_call(
        _mlstm_kernel,
        grid=(b // nb, nc),
        in_specs=[pl.BlockSpec((nb, ck, n_qk), lambda i, c: (i, c, 0)),
                  pl.BlockSpec((nb, ck, n_qk), lambda i, c: (i, c, 1)),
                  pl.BlockSpec((nb, ck, n_v), lambda i, c: (i, c, 2 * n_qk // n_v)),
                  pl.BlockSpec((nb, ck, n_v), lambda i, c: (i, c, 2 * n_qk // n_v + 1)),
                  pl.BlockSpec((nb, ck, GATE_LANES), lambda i, c: (i, c, 0)),
                  pl.BlockSpec((nb, None, 16, ck), lambda i, c: (i, c, 0, 0)),
                  const2((1, GATE_LANES)), const2((16, 1)), const2((1, n_v)),
                  const2((ck, ck)), const2((ck, ck))],
        out_specs=pl.BlockSpec((nb, ck, n_v), lambda i, c: (i, c, 0)),
        out_shape=jax.ShapeDtypeStruct((b, t, n_v), BF16),
        scratch_shapes=[pltpu.VMEM((nb * N_HEADS, HEAD_DK, 2 * MLSTM_DV), F32),
                        pltpu.VMEM((nb * N_HEADS, 1, GATE_LANES), F32)],
        compiler_params=_cparams(("parallel", "arbitrary")),
        name="mlstm_mixer",
    )(proj3, proj3, proj3, proj3, gates.reshape(b, t, GATE_LANES), gates_t,
      _lane_row(gate_bias, 0), _sub_col(gate_bias, 0), mlstm_norm.reshape(1, n_v).astype(F32), ltri, utri)


def _gates_transposed(gates, b, t, ck):
    return jnp.transpose(gates[:, :16].reshape(b, t // ck, ck, 16), (0, 1, 3, 2))


def _outproj_ffn_kernel(x_ref, ma_ref, mb_ref, nw_ref, wo_hbm, w1_hbm, w3_hbm, w2_hbm, out_ref,
                        wo_ref, w1_ref, w3_ref, w2_ref, sems, *, ff_chunks):
    @pl.when(pl.program_id(0) == 0)
    def _():
        copies = [pltpu.make_async_copy(src, dst, sems.at[k]) for k, (src, dst) in enumerate(
            ((wo_hbm, wo_ref), (w1_hbm, w1_ref), (w3_hbm, w3_ref), (w2_hbm, w2_ref)))]
        for cp in copies:
            cp.start()
        for cp in copies:
            cp.wait()

    n_a = ma_ref.shape[1]
    x1 = (x_ref[...] + jnp.dot(ma_ref[...], wo_ref[:n_a, :], preferred_element_type=F32)
          + jnp.dot(mb_ref[...], wo_ref[n_a:, :], preferred_element_type=F32))
    h = _rms(x1, nw_ref[...]).astype(BF16)
    out_ref[...] = x1
    for lo, hi in ff_chunks:
        a = jnp.dot(h, w1_ref[:, lo:hi], preferred_element_type=F32)
        g = jnp.dot(h, w3_ref[:, lo:hi], preferred_element_type=F32)
        out_ref[...] += jnp.dot((_silu(a) * g).astype(BF16), w2_ref[lo:hi, :], preferred_element_type=F32)


def _ff_chunks(d_ff, max_chunk=1024):
    assert d_ff % MXU_WIDTH == 0 and max_chunk % MXU_WIDTH == 0
    bounds = list(range(0, d_ff, max_chunk)) + [d_ff]
    return tuple(zip(bounds[:-1], bounds[1:]))


def _outproj_ffn(x2d, mix_a, mix_b, w_out, ffn_norm, w1, w3, w2, tm=512):
    n, d = x2d.shape
    tm = min(tm, n)
    d_ff = w1.shape[1]
    n_a, n_b = mix_a.shape[1], mix_b.shape[1]
    row = lambda w: pl.BlockSpec((tm, w), lambda i: (i, 0))
    hbm = pl.BlockSpec(memory_space=pl.ANY)
    return pl.pallas_call(
        functools.partial(_outproj_ffn_kernel, ff_chunks=_ff_chunks(d_ff)),
        grid=(n // tm,),
        in_specs=[row(d), row(n_a), row(n_b), pl.BlockSpec((1, d), lambda i: (0, 0)), hbm, hbm, hbm, hbm],
        out_specs=row(d),
        out_shape=jax.ShapeDtypeStruct((n, d), F32),
        scratch_shapes=[pltpu.VMEM(w_out.shape, BF16), pltpu.VMEM(w1.shape, BF16), pltpu.VMEM(w3.shape, BF16),
                        pltpu.VMEM(w2.shape, BF16), pltpu.SemaphoreType.DMA((4,))],
        compiler_params=_cparams(("arbitrary",)),
        name="outproj_ffn",
    )(x2d, mix_a, mix_b, ffn_norm.reshape(1, d), w_out, w1, w3, w2)


def _outproj_router_kernel(x_ref, m_ref, wo_ref, nw_ref, r_ref, x3_ref, hn_ref, idx_ref, gate_ref):
    x3 = x_ref[...] + jnp.dot(m_ref[...], wo_ref[...], preferred_element_type=F32)
    x3_ref[...] = x3
    hn = _rms(x3, nw_ref[...])
    hn_ref[...] = hn
    hn_hi = hn.astype(BF16)
    hn_lo = (hn - hn_hi.astype(F32)).astype(BF16)
    r_hi, r_lo = r_ref[:, :GATE_LANES], r_ref[:, GATE_LANES:]
    logits = (jnp.dot(hn_hi, r_hi, preferred_element_type=F32)
              + (jnp.dot(hn_hi, r_lo, preferred_element_type=F32)
                 + jnp.dot(hn_lo, r_hi, preferred_element_type=F32)))
    lane = lax.broadcasted_iota(jnp.int32, logits.shape, 1)
    logits = jnp.where(lane < N_EXPERTS, logits, -jnp.inf)
    m0 = jnp.max(logits, axis=-1, keepdims=True)
    i0 = jnp.min(jnp.where(logits == m0, lane, GATE_LANES), axis=-1, keepdims=True)
    rest = jnp.where(lane == i0, -jnp.inf, logits)
    m1 = jnp.max(rest, axis=-1, keepdims=True)
    i1 = jnp.min(jnp.where(rest == m1, lane, GATE_LANES), axis=-1, keepdims=True)
    e1 = jnp.exp(m1 - m0)
    inv = 1.0 / (1.0 + e1)
    idx_ref[...] = jnp.where(lane == 0, i0, jnp.where(lane == 1, i1, 0))
    gate_ref[...] = jnp.where(lane == 0, inv, jnp.where(lane == 1, e1 * inv, 0.0))


def _outproj_router(x2d, mix, w_out, ffn_norm, router, tm=1024):
    n, d = x2d.shape
    tm = min(tm, n)
    n_m = mix.shape[1]
    router_pad = jnp.zeros((d, GATE_LANES), F32).at[:, :N_EXPERTS].set(router.astype(F32))
    router_hi = router_pad.astype(BF16)
    router_lo = (router_pad - router_hi.astype(F32)).astype(BF16)
    router_split = jnp.concatenate([router_hi, router_lo], axis=1)
    row = lambda w: pl.BlockSpec((tm, w), lambda i: (i, 0))
    return pl.pallas_call(
        _outproj_router_kernel,
        grid=(n // tm,),
        in_specs=[row(d), row(n_m),
                  pl.BlockSpec((n_m, d), lambda i: (0, 0)),
                  pl.BlockSpec((1, d), lambda i: (0, 0)),
                  pl.BlockSpec((d, 2 * GATE_LANES), lambda i: (0, 0))],
        out_specs=[row(d), row(d), row(GATE_LANES), row(GATE_LANES)],
        out_shape=[jax.ShapeDtypeStruct((n, d), F32), jax.ShapeDtypeStruct((n, d), F32),
                   jax.ShapeDtypeStruct((n, GATE_LANES), jnp.int32),
                   jax.ShapeDtypeStruct((n, GATE_LANES), F32)],
        compiler_params=_cparams(("parallel",)),
        name="outproj_router",
    )(x2d, mix, w_out, ffn_norm.reshape(1, d), router_split)


def _rank_kernel(idx_ref, tri_ref, rank_ref, count_ref, carry_ref):
    @pl.when(pl.program_id(0) == 0)
    def _():
        carry_ref[...] = jnp.zeros_like(carry_ref)

    idx = idx_ref[...]
    lane = lax.broadcasted_iota(jnp.int32, idx.shape, 1)
    e0, e1 = idx[:, 0:1], idx[:, 1:2]
    member = ((lane == e0) | (lane == e1)).astype(F32)
    before = _dot(tri_ref[...], member) + carry_ref[...]
    r0 = jnp.sum(jnp.where(lane == e0, before, 0.0), axis=-1, keepdims=True)
    r1 = jnp.sum(jnp.where(lane == e1, before, 0.0), axis=-1, keepdims=True)
    packed = jnp.where(lane == 0, r0, jnp.where(lane == 1, r1, jnp.where(
        lane == 2, e0.astype(F32), jnp.where(lane == 3, e1.astype(F32), 0.0))))
    rank_ref[...] = jnp.transpose(packed)[0:8, :].astype(jnp.int32)
    carry_ref[...] += jnp.sum(member, axis=0, keepdims=True)
    count_ref[...] = carry_ref[...].astype(jnp.int32)


def _expert_ranks(ridx, tm=512):
    n = ridx.shape[0]
    tm = min(tm, n)
    i = np.arange(tm)
    tri = jnp.asarray((i[None, :] < i[:, None]).astype(np.float32)).astype(BF16)
    return pl.pallas_call(
        _rank_kernel,
        grid=(n // tm,),
        in_specs=[pl.BlockSpec((tm, GATE_LANES), lambda i: (i, 0)),
                  pl.BlockSpec((tm, tm), lambda i: (0, 0))],
        out_specs=[pl.BlockSpec((8, tm), lambda i: (0, i)),
                   pl.BlockSpec((1, GATE_LANES), lambda i: (0, 0))],
        out_shape=[jax.ShapeDtypeStruct((8, n), jnp.int32),
                   jax.ShapeDtypeStruct((1, GATE_LANES), jnp.int32)],
        scratch_shapes=[pltpu.VMEM((1, GATE_LANES), F32)],
        compiler_params=_cparams(("arbitrary",)),
        name="expert_ranks",
    )(ridx, tri)


def _row_copy(src_ref, src_row, dst_ref, dst_row, sem):
    return pltpu.make_async_copy(src_ref.at[pl.ds(src_row, 1)], dst_ref.at[pl.ds(dst_row, 1)], sem)


def _dispatch_kernel(dest_ref, fill_ref, hn_ref, slots_ref, buf_ref, zero_ref, load_sems, row_sems, fill_sem,
                     *, tm, n_tok, n_fill):
    i = pl.program_id(0)

    @pl.when(i == 0)
    def _():
        zero_ref[...] = jnp.zeros_like(zero_ref)
        for wait in (False, True):
            for s in range(n_fill):
                def fill(r, carry):
                    cp = _row_copy(zero_ref, 0, slots_ref, r, fill_sem)
                    cp.wait() if wait else cp.start()
                    return carry
                lax.fori_loop(fill_ref[2 * s], fill_ref[2 * s + 1], fill, 0)

    last = pl.num_programs(0) - 1
    base = i * tm
    cur = i % 3
    rsem = i % 2

    def load(tile, b):
        return pltpu.make_async_copy(hn_ref.at[pl.ds(tile * tm, tm)], buf_ref.at[b], load_sems.at[b])

    def drain(s):
        for _ in range(2):
            pltpu.make_async_copy(buf_ref.at[0], slots_ref.at[pl.ds(0, tm)], row_sems.at[s]).wait()

    @pl.when(i == 0)
    def _():
        load(0, 0).start()

    @pl.when(i < last)
    def _():
        load(i + 1, (i + 1) % 3).start()

    load(i, cur).wait()

    def start(r, carry):
        _row_copy(buf_ref.at[cur], r, slots_ref, dest_ref[base + r], row_sems.at[rsem]).start()
        _row_copy(buf_ref.at[cur], r, slots_ref, dest_ref[n_tok + base + r], row_sems.at[rsem]).start(priority=1)
        return carry

    lax.fori_loop(0, tm, start, 0, unroll=DMA_UNROLL)

    @pl.when(i > 0)
    def _():
        drain(1 - rsem)

    @pl.when(i == last)
    def _():
        drain(rsem)


def _dispatch(dest_flat, fill_ranges, hn, n_slots, tm=512):
    n, d = hn.shape
    tm = min(tm, n)
    grid_spec = pltpu.PrefetchScalarGridSpec(
        num_scalar_prefetch=2,
        grid=(n // tm,),
        in_specs=[pl.BlockSpec(memory_space=pl.ANY)],
        out_specs=pl.BlockSpec(memory_space=pl.ANY),
        scratch_shapes=[pltpu.VMEM((3, tm, d), hn.dtype), pltpu.VMEM((8, d), hn.dtype),
                        pltpu.SemaphoreType.DMA((3,)), pltpu.SemaphoreType.DMA((2,)),
                        pltpu.SemaphoreType.DMA(())],
    )
    return pl.pallas_call(
        functools.partial(_dispatch_kernel, tm=tm, n_tok=n, n_fill=fill_ranges.shape[0] // 2),
        grid_spec=grid_spec,
        out_shape=jax.ShapeDtypeStruct((n_slots, d), hn.dtype),
        compiler_params=_cparams(("arbitrary",)),
        name="row_dispatch",
    )(dest_flat, fill_ranges, hn)


def _moe_kernel(be_ref, nused_ref, x_ref, w1_hbm, w3_hbm, w2_hbm, out_ref, w1_ref, w3_ref, w2_ref,
                stage_cols_ref, stage_rows_ref, sems, *, ff_chunks, load_chunk):
    i = pl.program_id(0)
    used = i < nused_ref[0]
    expert = be_ref[i]
    prev_expert = be_ref[jnp.maximum(i - 1, 0)]
    d_ff = w1_ref.shape[1]

    @pl.when(used & ((i == 0) | (expert != prev_expert)))
    def _():
        pieces = []
        for lo in range(0, d_ff, load_chunk):
            cols = pl.ds(lo, load_chunk)
            pieces.append((w1_hbm.at[expert, :, cols], stage_cols_ref, w1_ref.at[:, cols]))
            pieces.append((w3_hbm.at[expert, :, cols], stage_cols_ref, w3_ref.at[:, cols]))
        for lo in range(0, d_ff, load_chunk):
            pieces.append((w2_hbm.at[expert, pl.ds(lo, load_chunk), :], stage_rows_ref,
                           w2_ref.at[pl.ds(lo, load_chunk), :]))
        copies = [pltpu.make_async_copy(src, ring.at[k % 2], sems.at[k % 2])
                  for k, (src, ring, _) in enumerate(pieces)]
        copies[0].start()
        for k, (_, ring, dst) in enumerate(pieces):
            if k + 1 < len(pieces):
                copies[k + 1].start()
            copies[k].wait()
            dst[...] = ring[k % 2].astype(BF16)

    @pl.when(used)
    def _():
        x = x_ref[...].astype(BF16)
        for n_done, (lo, hi) in enumerate(ff_chunks):
            a = jnp.dot(x, w1_ref[:, lo:hi], preferred_element_type=F32)
            g = jnp.dot(x, w3_ref[:, lo:hi], preferred_element_type=F32)
            y = jnp.dot((_silu(a) * g).astype(BF16), w2_ref[lo:hi, :], preferred_element_type=F32)
            if n_done == 0:
                out_ref[...] = y
            else:
                out_ref[...] += y

    @pl.when(jnp.logical_not(used))
    def _():
        out_ref[...] = jnp.zeros_like(out_ref)


def _moe_ffn(block_expert, n_used, slots, w1, w3, w2, load_chunk=512):
    n_slots, d = slots.shape
    n_blocks = n_slots // MOE_BLOCK
    d_ff = w1.shape[2]
    assert d_ff % load_chunk == 0
    hbm = pl.BlockSpec(memory_space=pl.ANY)
    grid_spec = pltpu.PrefetchScalarGridSpec(
        num_scalar_prefetch=2,
        grid=(n_blocks,),
        in_specs=[pl.BlockSpec((MOE_BLOCK, d), lambda i, be, nu: (jnp.minimum(i, nu[0] - 1), 0)), hbm, hbm, hbm],
        out_specs=pl.BlockSpec((MOE_BLOCK, d), lambda i, be, nu: (i, 0)),
        scratch_shapes=[pltpu.VMEM((d, d_ff), BF16), pltpu.VMEM((d, d_ff), BF16), pltpu.VMEM((d_ff, d), BF16),
                        pltpu.VMEM((2, d, load_chunk), w1.dtype), pltpu.VMEM((2, load_chunk, d), w2.dtype),
                        pltpu.SemaphoreType.DMA((2,))],
    )
    return pl.pallas_call(
        functools.partial(_moe_kernel, ff_chunks=_ff_chunks(d_ff), load_chunk=load_chunk),
        grid_spec=grid_spec,
        out_shape=jax.ShapeDtypeStruct((n_slots, d), F32),
        compiler_params=_cparams(("arbitrary",)),
        name="moe_ffn",
    )(block_expert, n_used, slots, w1, w3, w2)


def _combine_kernel(dest_ref, x_ref, gate_ref, nw_ref, y_ref, out_ref, ybuf_ref, sems, *, tm, n_tok):
    i = pl.program_id(0)
    slot = i % 2

    def gather(tile, s):
        base = tile * tm

        def start(r, carry):
            _row_copy(y_ref, dest_ref[base + r], ybuf_ref.at[s, 0], r, sems.at[s]).start()
            _row_copy(y_ref, dest_ref[n_tok + base + r], ybuf_ref.at[s, 1], r, sems.at[s]).start(priority=1)
            return carry

        lax.fori_loop(0, tm, start, 0, unroll=DMA_UNROLL)

    @pl.when(i == 0)
    def _():
        gather(0, 0)

    @pl.when(i + 1 < pl.num_programs(0))
    def _():
        gather(i + 1, 1 - slot)

    for k in range(2):
        pltpu.make_async_copy(y_ref.at[pl.ds(0, tm)], ybuf_ref.at[slot, k], sems.at[slot]).wait()
    gate = gate_ref[...]
    y = x_ref[...] + (ybuf_ref[slot, 0] * gate[:, 0:1] + ybuf_ref[slot, 1] * gate[:, 1:2])
    out_ref[...] = _rms(y, nw_ref[...])


def _combine(dest_flat, x2d, rgate, final_norm, yb, tm=512):
    n, d = x2d.shape
    tm = min(tm, n)
    grid_spec = pltpu.PrefetchScalarGridSpec(
        num_scalar_prefetch=1,
        grid=(n // tm,),
        in_specs=[pl.BlockSpec((tm, d), lambda i, dest: (i, 0)),
                  pl.BlockSpec((tm, GATE_LANES), lambda i, dest: (i, 0)),
                  pl.BlockSpec((1, d), lambda i, dest: (0, 0)),
                  pl.BlockSpec(memory_space=pl.ANY)],
        out_specs=pl.BlockSpec((tm, d), lambda i, dest: (i, 0)),
        scratch_shapes=[pltpu.VMEM((2, 2, tm, d), F32), pltpu.SemaphoreType.DMA((2,))],
    )
    return pl.pallas_call(
        functools.partial(_combine_kernel, tm=tm, n_tok=n),
        grid_spec=grid_spec,
        out_shape=jax.ShapeDtypeStruct((n, d), F32),
        compiler_params=_cparams(("arbitrary",)),
        name="combine_norm",
    )(dest_flat, x2d, rgate, final_norm.reshape(1, d), yb)


def _split_in_proj(w_in, lo, hi):
    main = jnp.concatenate([w_in[:, :lo], w_in[:, hi:]], axis=1)
    gate = jnp.pad(w_in[:, lo:hi], ((0, 0), (0, GATE_LANES - (hi - lo))))
    return jnp.concatenate([main, gate], axis=1).astype(BF16), main.shape[1]


def kernel(x, mix_norm_0, w_in_0, conv_w_0, a_log_0, dt_bias_0, gdn_norm_0, ret_norm_0, w_out_0, ffn_norm_0,
           ffn_w1_0, ffn_w3_0, ffn_w2_0, mix_norm_1, w_in_1, gate_bias_1, mlstm_norm_1, w_out_1, ffn_norm_1,
           router_1, exp_w1_1, exp_w3_1, exp_w2_1, final_norm):
    b, t, d = x.shape
    n = b * t
    x2d = x.reshape(n, d)

    gdn_qkv = N_HEADS * (2 * HEAD_DK + GDN_DV)
    w0, n_main0 = _split_in_proj(w_in_0, gdn_qkv, gdn_qkv + 2 * N_HEADS)
    proj0, gates0 = _norm_proj(x2d, mix_norm_0, w0, n_main0)
    o_gdn = _gdn(proj0, gates0, _gates_transposed(gates0, b, t, CHUNK), conv_w_0, a_log_0, dt_bias_0, gdn_norm_0, b, t)
    o_ret = _retention(proj0, ret_norm_0, b, t)
    x2 = _outproj_ffn(x2d, o_gdn.reshape(n, -1), o_ret.reshape(n, -1), w_out_0.astype(BF16), ffn_norm_0,
                      ffn_w1_0.astype(BF16), ffn_w3_0.astype(BF16), ffn_w2_0.astype(BF16))

    n_main1 = w_in_1.shape[1] - 2 * N_HEADS
    w1, _ = _split_in_proj(w_in_1, n_main1, n_main1 + 2 * N_HEADS)
    proj1, gates1 = _norm_proj(x2, mix_norm_1, w1, n_main1)
    h_mix = _mlstm(proj1, gates1, _gates_transposed(gates1, b, t, MLSTM_CHUNK), gate_bias_1, mlstm_norm_1, b, t)
    x3, hn, ridx, rgate = _outproj_router(x2, h_mix.reshape(n, -1), w_out_1.astype(BF16), ffn_norm_1, router_1)

    rank, counts = _expert_ranks(ridx)
    counts = counts[0, :N_EXPERTS]
    padded = (counts + MOE_BLOCK - 1) // MOE_BLOCK * MOE_BLOCK
    pad_end = jnp.cumsum(padded)
    pad_start = pad_end - padded
    dest_flat = (pad_start[rank[2:4]] + rank[0:2]).reshape(-1)
    n_blocks = -(-(2 * n) // MOE_BLOCK) + N_EXPERTS
    block_row0 = jnp.arange(n_blocks, dtype=jnp.int32) * MOE_BLOCK
    block_expert = jnp.minimum(jnp.sum((pad_end[None, :] <= block_row0[:, None]).astype(jnp.int32), axis=1),
                               N_EXPERTS - 1)
    n_used = (pad_end[-1:] // MOE_BLOCK).astype(jnp.int32)

    n_slots = n_blocks * MOE_BLOCK
    fill_lo = jnp.concatenate([pad_start + counts, pad_end[-1:]])
    fill_hi = jnp.concatenate([pad_end, jnp.full((1,), n_slots, jnp.int32)])
    fill_ranges = jnp.stack([fill_lo, fill_hi], axis=1).reshape(-1).astype(jnp.int32)
    slots = _dispatch(dest_flat, fill_ranges, hn, n_slots)
    yb = _moe_ffn(block_expert, n_used, slots, exp_w1_1, exp_w3_1, exp_w2_1)
    out = _combine(dest_flat, x3, rgate, final_norm, yb)
    return out.reshape(b, t, d)
```

```python
import functools
import math

import numpy as np
import jax
import jax.numpy as jnp
from jax import lax
from jax.experimental import pallas as pl
from jax.experimental.pallas import tpu as pltpu

F32 = jnp.float32
BF16 = jnp.bfloat16
HIGHEST = lax.Precision.HIGHEST

EPS = 1e-6
CHUNK = 128
RET_CHUNK = 128
MLSTM_CHUNK = 128
CONV_WIDTH = 4
N_HEADS = 8
HEAD_DK = 64
GDN_DV = 64
RET_DV = 64
MLSTM_DV = 128
ROPE_BASE = 10000.0
N_EXPERTS = 8
MOE_BLOCK = 512
GATE_LANES = 128
MXU_WIDTH = 256
VMEM_LIMIT = 52 * 1024 * 1024
SUBLANES = 8
DMA_UNROLL = 16


def _cparams(sem):
    return pltpu.CompilerParams(dimension_semantics=sem, vmem_limit_bytes=VMEM_LIMIT)


def _dot(a, b):
    return jnp.dot(a.astype(BF16), b.astype(BF16), preferred_element_type=F32)


def _dot_nt(a, b):
    return lax.dot_general(a.astype(BF16), b.astype(BF16), (((1,), (1,)), ((), ())),
                           preferred_element_type=F32)


def _dot_tn(a, b):
    return lax.dot_general(a.astype(BF16), b.astype(BF16), (((0,), (0,)), ((), ())),
                           preferred_element_type=F32)


def _dot_f32(a, b):
    return jnp.dot(a, b, precision=HIGHEST, preferred_element_type=F32)


def _rms(x, w):
    return x * lax.rsqrt(jnp.mean(x * x, axis=-1, keepdims=True) + EPS) * w


def _silu(x):
    return x * (1.0 / (1.0 + jnp.exp(-x)))


def _sigmoid(x):
    return 1.0 / (1.0 + jnp.exp(-x))


def _softplus(x):
    return jnp.maximum(x, 0.0) + jnp.log1p(jnp.exp(-jnp.abs(x)))


def _causal(n):
    r = lax.broadcasted_iota(jnp.int32, (n, n), 0)
    c = lax.broadcasted_iota(jnp.int32, (n, n), 1)
    return r >= c


def _norm_proj_kernel(x_ref, nw_ref, w_ref, main_ref, gate_ref, *, n_main, col_chunk):
    h = _rms(x_ref[...], nw_ref[...]).astype(BF16)
    for c0 in range(0, n_main, col_chunk):
        main_ref[:, c0:c0 + col_chunk] = jnp.dot(
            h, w_ref[:, c0:c0 + col_chunk], preferred_element_type=F32).astype(BF16)
    gate_ref[...] = jnp.dot(h, w_ref[:, n_main:], preferred_element_type=F32)


def _norm_proj(x2d, norm_w, w_cat, n_main, tm=512, col_chunk=512):
    n, d = x2d.shape
    tm = min(tm, n)
    n_cat = w_cat.shape[1]
    return pl.pallas_call(
        functools.partial(_norm_proj_kernel, n_main=n_main, col_chunk=col_chunk),
        grid=(n // tm,),
        in_specs=[pl.BlockSpec((tm, d), lambda i: (i, 0)),
                  pl.BlockSpec((1, d), lambda i: (0, 0)),
                  pl.BlockSpec((d, n_cat), lambda i: (0, 0))],
        out_specs=[pl.BlockSpec((tm, n_main), lambda i: (i, 0)),
                   pl.BlockSpec((tm, GATE_LANES), lambda i: (i, 0))],
        out_shape=[jax.ShapeDtypeStruct((n, n_main), BF16),
                   jax.ShapeDtypeStruct((n, GATE_LANES), F32)],
        compiler_params=_cparams(("parallel",)),
        name="norm_proj",
    )(x2d, norm_w.reshape(1, d), w_cat)


def _gdn_kernel(qkv_ref, z_ref, gate_ref, gate_t_ref, conv_ref, arow_ref, dtrow_ref, acol_ref, dtcol_ref,
                norm_ref, ltri_ref, utri_ref, lvl_ref, ones_ref, out_ref, state_ref, xbuf_ref):
    n_b = qkv_ref.shape[0]
    n_qk = N_HEADS * HEAD_DK

    @pl.when(pl.program_id(1) == 0)
    def _():
        state_ref[...] = jnp.zeros_like(state_ref)
        xbuf_ref[:, 0:8, :] = jnp.zeros((n_b, 8, xbuf_ref.shape[2]), F32)

    causal = _causal(CHUNK)
    eye = (lax.broadcasted_iota(jnp.int32, (CHUNK, CHUNK), 0)
           == lax.broadcasted_iota(jnp.int32, (CHUNK, CHUNK), 1)).astype(F32)
    sl = [slice(h * HEAD_DK, (h + 1) * HEAD_DK) for h in range(N_HEADS)]

    q, k, v, bc, gcc, gcr = [], [], [], [], [], []
    for i in range(n_b):
        xbuf_ref[i, 8:8 + CHUNK, :] = qkv_ref[i].astype(F32)
        pre = None
        for j in range(CONV_WIDTH):
            term = xbuf_ref[i, pl.ds(8 - (CONV_WIDTH - 1) + j, CHUNK), :] * conv_ref[j:j + 1, :]
            pre = term if pre is None else pre + term
        xbuf_ref[i, 0:8, :] = xbuf_ref[i, CHUNK:CHUNK + 8, :]
        act = _silu(pre)
        q_all, k_all, v_all = act[:, :n_qk], act[:, n_qk:2 * n_qk], act[:, 2 * n_qk:]
        q_all = q_all * lax.rsqrt(_dot(q_all * q_all, ones_ref[...]) + EPS) * (HEAD_DK ** -0.5)
        k_all = k_all * lax.rsqrt(_dot(k_all * k_all, ones_ref[...]) + EPS)
        gl = gate_ref[i]
        beta_c = _sigmoid(gl)
        g_c = -jnp.exp(arow_ref[...]) * _softplus(gl + dtrow_ref[...])
        gc_c = _dot_f32(ltri_ref[...], g_c)
        g_r = -jnp.exp(acol_ref[...]) * _softplus(gate_t_ref[i] + dtcol_ref[...])
        gc_r = _dot_f32(g_r, utri_ref[...])
        for h in range(N_HEADS):
            q.append(q_all[:, sl[h]])
            k.append(k_all[:, sl[h]])
            v.append(v_all[:, sl[h]])
            bc.append(beta_c[:, h:h + 1])
            gcc.append(gc_c[:, 8 + h:9 + h])
            gcr.append(gc_r[8 + h:9 + h, :])

    chains = range(n_b * N_HEADS)
    decay = [jnp.exp(jnp.where(causal, gcc[c] - gcr[c], -jnp.inf)) for c in chains]
    kb = [k[c] * bc[c] for c in chains]
    scores = [_dot_nt(jnp.concatenate([kb[c], q[c]], axis=0), k[c]) for c in chains]
    lmat = [scores[c][:CHUNK] * decay[c] for c in chains]
    qk = [scores[c][CHUNK:] * decay[c] for c in chains]
    tinv = [eye - lmat[c] * lvl_ref[0].astype(F32) for c in chains]
    lmat_mxu = [lmat[c].astype(BF16) for c in chains]
    for lvl in range(1, lvl_ref.shape[0]):
        inner = [_dot(lmat_mxu[c] * lvl_ref[lvl], tinv[c]) for c in chains]
        tinv = [tinv[c] - _dot(tinv[c], inner[c]) for c in chains]
    egc = [jnp.exp(gcc[c]) for c in chains]
    uw = [_dot(tinv[c], jnp.concatenate([v[c] * bc[c], kb[c] * egc[c]], axis=1)) for c in chains]
    state = [state_ref[c] for c in chains]
    v_new = [uw[c][:, :GDN_DV] - _dot(uw[c][:, GDN_DV:], state[c]) for c in chains]
    o = [_dot(q[c] * egc[c], state[c]) + _dot(qk[c], v_new[c]) for c in chains]
    g_last = [gcc[c][CHUNK - 1:CHUNK, :] for c in chains]
    k_dec = [k[c] * jnp.exp(g_last[c] - gcc[c]) for c in chains]
    for c in chains:
        state_ref[c] = state[c] * jnp.exp(g_last[c]) + _dot_tn(k_dec[c], v_new[c])
    for i in range(n_b):
        o_all = jnp.concatenate(o[i * N_HEADS:(i + 1) * N_HEADS], axis=1)
        mean_sq = _dot(o_all * o_all, ones_ref[...]) * (1.0 / GDN_DV)
        out_ref[i] = (o_all * lax.rsqrt(mean_sq + EPS) * norm_ref[...]
                      * _silu(z_ref[i].astype(F32))).astype(out_ref.dtype)


def _level_masks():
    i = np.arange(CHUNK)[:, None]
    j = np.arange(CHUNK)[None, :]
    masks = []
    for lvl in range(CHUNK.bit_length() - 1):
        same_parent = (i >> (lvl + 1)) == (j >> (lvl + 1))
        masks.append((same_parent & (((i >> lvl) & 1) == 1) & (((j >> lvl) & 1) == 0)).astype(np.float32))
    return jnp.asarray(np.stack(masks)).astype(BF16)


def _tri_consts(ck):
    i = np.arange(ck)[:, None]
    j = np.arange(ck)[None, :]
    return jnp.asarray((j <= i).astype(np.float32)), jnp.asarray((i <= j).astype(np.float32))


def _lane_row(vals, offset):
    return jnp.zeros((1, GATE_LANES), F32).at[0, offset:offset + vals.shape[0]].set(vals.astype(F32))


def _sub_col(vals, offset):
    return jnp.zeros((16, 1), F32).at[offset:offset + vals.shape[0], 0].set(vals.astype(F32))


def _head_ones(n_heads, width):
    g = np.arange(n_heads * width) // width
    return jnp.asarray((g[:, None] == g[None, :]).astype(np.float32)).astype(BF16)


def _batch_per_step(b):
    return 2 if b % 2 == 0 else 1


def _gdn(proj, gates, gates_t, conv_w, a_log, dt_bias, gdn_norm, b, t):
    nc = t // CHUNK
    nb = _batch_per_step(b)
    n_qkv = N_HEADS * (2 * HEAD_DK + GDN_DV)
    n_z = N_HEADS * GDN_DV
    proj3 = proj.reshape(b, t, proj.shape[-1])
    ltri, utri = _tri_consts(CHUNK)
    const2 = lambda shape: pl.BlockSpec(shape, lambda i, c: (0,) * len(shape))
    return pl.pallas_call(
        _gdn_kernel,
        grid=(b // nb, nc),
        in_specs=[pl.BlockSpec((nb, CHUNK, n_qkv), lambda i, c: (i, c, 0)),
                  pl.BlockSpec((nb, CHUNK, n_z), lambda i, c: (i, c, n_qkv // n_z)),
                  pl.BlockSpec((nb, CHUNK, GATE_LANES), lambda i, c: (i, c, 0)),
                  pl.BlockSpec((nb, None, 16, CHUNK), lambda i, c: (i, c, 0, 0)),
                  const2((CONV_WIDTH, n_qkv)), const2((1, GATE_LANES)), const2((1, GATE_LANES)),
                  const2((16, 1)), const2((16, 1)), const2((1, n_z)),
                  const2((CHUNK, CHUNK)), const2((CHUNK, CHUNK)),
                  const2((CHUNK.bit_length() - 1, CHUNK, CHUNK)),
                  const2((n_z, n_z))],
        out_specs=pl.BlockSpec((nb, CHUNK, n_z), lambda i, c: (i, c, 0)),
        out_shape=jax.ShapeDtypeStruct((b, t, n_z), BF16),
        scratch_shapes=[pltpu.VMEM((nb * N_HEADS, HEAD_DK, GDN_DV), F32),
                        pltpu.VMEM((nb, CHUNK + 8, n_qkv), F32)],
        compiler_params=_cparams(("parallel", "arbitrary")),
        name="gdn_mixer",
    )(proj3, proj3, gates.reshape(b, t, GATE_LANES), gates_t, conv_w.astype(F32),
      _lane_row(a_log, 8), _lane_row(dt_bias, 8), _sub_col(a_log, 8), _sub_col(dt_bias, 8),
      jnp.tile(gdn_norm.astype(F32), N_HEADS).reshape(1, n_z), ltri, utri, _level_masks(),
      _head_ones(N_HEADS, HEAD_DK))


def _ret_kernel(q_ref, k_ref, v_ref, g_ref, cos_ref, sin_ref, decay_ref, qs_ref, ks_ref, cd_ref, norm_ref,
                ones_ref, out_ref, state_ref):
    @pl.when(pl.program_id(1) == 0)
    def _():
        state_ref[...] = jnp.zeros_like(state_ref)

    width = N_HEADS * HEAD_DK
    half = HEAD_DK // 2
    first_half = (lax.broadcasted_iota(jnp.int32, (q_ref.shape[1], width), 1) % HEAD_DK) < half

    def rot(t_in):
        swapped = jnp.where(first_half, pltpu.roll(t_in, width - half, 1), pltpu.roll(t_in, half, 1))
        return t_in * cos_ref[...] + swapped * sin_ref[...]

    n_b = q_ref.shape[0]
    sl = [slice(h * HEAD_DK, (h + 1) * HEAD_DK) for h in range(N_HEADS)]
    q, k, v, q_in, k_st = [], [], [], [], []
    for i in range(n_b):
        q_all = rot(q_ref[i].astype(F32))
        k_all = rot(k_ref[i].astype(F32)) * (HEAD_DK ** -0.5)
        v_all = v_ref[i].astype(F32)
        qi_all = q_all * qs_ref[...]
        ks_all = k_all * ks_ref[...]
        for h in range(N_HEADS):
            q.append(q_all[:, sl[h]])
            k.append(k_all[:, sl[h]])
            v.append(v_all[:, sl[h]])
            q_in.append(qi_all[:, sl[h]])
            k_st.append(ks_all[:, sl[h]])
    chains = range(n_b * N_HEADS)
    state = [state_ref[c] for c in chains]
    scores = [_dot_nt(q[c], k[c]) * decay_ref[c % N_HEADS] for c in chains]
    inter = [_dot(q_in[c], state[c]) for c in chains]
    o = [_dot(scores[c], v[c]) + inter[c] for c in chains]
    for c in chains:
        state_ref[c] = state[c] * cd_ref[c % N_HEADS] + _dot_tn(k_st[c], v[c])
    inv_dv = 1.0 / RET_DV
    for i in range(n_b):
        o_all = jnp.concatenate(o[i * N_HEADS:(i + 1) * N_HEADS], axis=1)
        xc = o_all - _dot(o_all, ones_ref[...]) * inv_dv
        var = _dot(xc * xc, ones_ref[...]) * inv_dv
        out_ref[i] = (xc * lax.rsqrt(var + EPS) * norm_ref[...]
                      * _silu(g_ref[i].astype(F32))).astype(out_ref.dtype)


def _retention(proj, ret_norm, b, t):
    ck = RET_CHUNK
    nc = t // ck
    width = N_HEADS * HEAD_DK
    half = HEAD_DK // 2
    proj3 = proj.reshape(b, t, proj.shape[-1])
    pos = jnp.arange(t, dtype=F32)
    inv_freq = jnp.power(ROPE_BASE, -jnp.arange(half, dtype=F32) / half)
    ang = pos[:, None] * inv_freq[None, :]
    cos, sin = jnp.cos(ang), jnp.sin(ang)
    cos_t = jnp.tile(jnp.concatenate([cos, cos], axis=1), (1, N_HEADS))
    sin_t = jnp.tile(jnp.concatenate([-sin, sin], axis=1), (1, N_HEADS))
    log_gamma = jnp.log1p(-jnp.exp2(-5.0 - jnp.arange(N_HEADS, dtype=F32)))
    idx = jnp.arange(ck, dtype=F32)
    causal = jnp.tril(jnp.ones((ck, ck), bool))
    decay = jnp.exp(jnp.where(causal, (idx[:, None] - idx[None, :]) * log_gamma[:, None, None], -jnp.inf))
    qs = jnp.repeat(jnp.exp((idx + 1.0) * log_gamma[:, None]).T, HEAD_DK, axis=1)
    ks = jnp.repeat(jnp.exp((ck - 1.0 - idx) * log_gamma[:, None]).T, HEAD_DK, axis=1)
    cd = jnp.broadcast_to(jnp.exp(ck * log_gamma)[:, None, None], (N_HEADS, 1, RET_DV))
    base = (N_HEADS * (2 * HEAD_DK + GDN_DV) + N_HEADS * GDN_DV) // width
    nb = _batch_per_step(b)
    col = lambda k: pl.BlockSpec((nb, ck, width), lambda i, c: (i, c, base + k))
    const2 = lambda shape: pl.BlockSpec(shape, lambda i, c: (0,) * len(shape))
    return pl.pallas_call(
        _ret_kernel,
        grid=(b // nb, nc),
        in_specs=[col(0), col(1), col(2), col(3),
                  pl.BlockSpec((ck, width), lambda i, c: (c, 0)),
                  pl.BlockSpec((ck, width), lambda i, c: (c, 0)),
                  const2((N_HEADS, ck, ck)), const2((ck, width)), const2((ck, width)),
                  const2((N_HEADS, 1, RET_DV)), const2((1, width)), const2((width, width))],
        out_specs=pl.BlockSpec((nb, ck, width), lambda i, c: (i, c, 0)),
        out_shape=jax.ShapeDtypeStruct((b, t, width), BF16),
        scratch_shapes=[pltpu.VMEM((nb * N_HEADS, HEAD_DK, RET_DV), F32)],
        compiler_params=_cparams(("parallel", "arbitrary")),
        name="retention_mixer",
    )(proj3, proj3, proj3, proj3, cos_t, sin_t, decay, qs, ks, cd, ret_norm.reshape(1, width).astype(F32),
      _head_ones(N_HEADS, RET_DV))


def _mlstm_kernel(q_ref, k_ref, v_ref, o_ref, gate_ref, gate_t_ref, brow_ref, bcol_ref, norm_ref,
                  ltri_ref, utri_ref, out_ref, state_ref, m_ref):
    @pl.when(pl.program_id(1) == 0)
    def _():
        state_ref[...] = jnp.zeros_like(state_ref)
        m_ref[...] = jnp.zeros_like(m_ref)

    n_b, ck = q_ref.shape[0], q_ref.shape[1]
    causal = _causal(ck)
    ones_col = (lax.broadcasted_iota(jnp.int32, (ck, MLSTM_DV), 1) == 0).astype(F32)
    sl = [slice(h * HEAD_DK, (h + 1) * HEAD_DK) for h in range(N_HEADS)]
    sv = [slice(h * MLSTM_DV, (h + 1) * MLSTM_DV) for h in range(N_HEADS)]

    q, k, vx, bc, br, ig_c, ig_r = [], [], [], [], [], [], []
    for i in range(n_b):
        q_all = q_ref[i].astype(F32)
        k_all = k_ref[i].astype(F32) * (HEAD_DK ** -0.5)
        v_all = v_ref[i].astype(F32)
        gl = gate_ref[i] + brow_ref[...]
        bcum_c = _dot_f32(ltri_ref[...], -_softplus(-gl))
        gt = gate_t_ref[i] + bcol_ref[...]
        bcum_r = _dot_f32(-_softplus(-gt), utri_ref[...])
        for h in range(N_HEADS):
            q.append(q_all[:, sl[h]])
            k.append(k_all[:, sl[h]])
            vx.append(jnp.concatenate([v_all[:, sv[h]], ones_col], axis=1))
            bc.append(bcum_c[:, 8 + h:9 + h])
            br.append(bcum_r[8 + h:9 + h, :])
            ig_c.append(gl[:, h:h + 1])
            ig_r.append(gt[h:h + 1, :])

    chains = range(n_b * N_HEADS)
    log_w = [jnp.where(causal, bc[c] - br[c] + ig_r[c], -jnp.inf) for c in chains]
    m_intra = [jnp.max(log_w[c], axis=-1, keepdims=True) for c in chains]
    b_end = [bc[c][ck - 1:ck, :] for c in chains]
    lw_end = [b_end[c] - bc[c] + ig_c[c] for c in chains]
    m_end = [jnp.max(lw_end[c], axis=0, keepdims=True) for c in chains]
    m_s = [m_ref[c][:, 0:1] for c in chains]
    cx = [state_ref[c] for c in chains]
    m_t = [jnp.maximum(bc[c] + m_s[c], m_intra[c]) for c in chains]
    inter = [jnp.exp(bc[c] + m_s[c] - m_t[c]) for c in chains]
    qk = [_dot_nt(q[c], k[c]) for c in chains]
    qc = [_dot(q[c], cx[c]) for c in chains]
    s = [qk[c] * jnp.exp(log_w[c] - m_t[c]) for c in chains]
    numx = [inter[c] * qc[c] + _dot(s[c], vx[c]) for c in chains]
    m_new = [jnp.maximum(b_end[c] + m_s[c], m_end[c]) for c in chains]
    wk = [k[c] * jnp.exp(lw_end[c] - m_new[c]) for c in chains]
    for c in chains:
        state_ref[c] = jnp.exp(b_end[c] + m_s[c] - m_new[c]) * cx[c] + _dot_tn(wk[c], vx[c])
        m_ref[c] = jnp.broadcast_to(m_new[c], (1, GATE_LANES))
    hh = [numx[c][:, :MLSTM_DV] / jnp.maximum(jnp.abs(numx[c][:, MLSTM_DV:MLSTM_DV + 1]), jnp.exp(-m_t[c]))
          for c in chains]
    outs = [_rms(hh[c], norm_ref[:, sv[c % N_HEADS]]) for c in chains]
    for i in range(n_b):
        out_ref[i] = (jnp.concatenate(outs[i * N_HEADS:(i + 1) * N_HEADS], axis=1)
                      * _sigmoid(o_ref[i].astype(F32))).astype(out_ref.dtype)


def _mlstm(proj, gates, gates_t, gate_bias, mlstm_norm, b, t):
    ck = MLSTM_CHUNK
    nc = t // ck
    n_qk = N_HEADS * HEAD_DK
    n_v = N_HEADS * MLSTM_DV
    proj3 = proj.reshape(b, t, proj.shape[-1])
    ltri, utri = _tri_consts(ck)
    nb = _batch_per_step(b)
    const2 = lambda shape: pl.BlockSpec(shape, lambda i, c: (0,) * len(shape))
    return pl.pallas_call(
        _mlstm_kernel,
        grid=(b // nb, nc),
        in_specs=[pl.BlockSpec((nb, ck, n_qk), lambda i, c: (i, c, 0)),
                  pl.BlockSpec((nb, ck, n_qk), lambda i, c: (i, c, 1)),
                  pl.BlockSpec((nb, ck, n_v), lambda i, c: (i, c, 2 * n_qk // n_v)),
                  pl.BlockSpec((nb, ck, n_v), lambda i, c: (i, c, 2 * n_qk // n_v + 1)),
                  pl.BlockSpec((nb, ck, GATE_LANES), lambda i, c: (i, c, 0)),
                  pl.BlockSpec((nb, None, 16, ck), lambda i, c: (i, c, 0, 0)),
                  const2((1, GATE_LANES)), const2((16, 1)), const2((1, n_v)),
                  const2((ck, ck)), const2((ck, ck))],
        out_specs=pl.BlockSpec((nb, ck, n_v), lambda i, c: (i, c, 0)),
        out_shape=jax.ShapeDtypeStruct((b, t, n_v), BF16),
        scratch_shapes=[pltpu.VMEM((nb * N_HEADS, HEAD_DK, 2 * MLSTM_DV), F32),
                        pltpu.VMEM((nb * N_HEADS, 1, GATE_LANES), F32)],
        compiler_params=_cparams(("parallel", "arbitrary")),
        name="mlstm_mixer",
    )(proj3, proj3, proj3, proj3, gates.reshape(b, t, GATE_LANES), gates_t,
      _lane_row(gate_bias, 0), _sub_col(gate_bias, 0), mlstm_norm.reshape(1, n_v).astype(F32), ltri, utri)


def _gates_transposed(gates, b, t, ck):
    return jnp.transpose(gates[:, :16].reshape(b, t // ck, ck, 16), (0, 1, 3, 2))


def _outproj_ffn_kernel(x_ref, ma_ref, mb_ref, nw_ref, wo_hbm, w1_hbm, w3_hbm, w2_hbm, out_ref,
                        wo_ref, w1_ref, w3_ref, w2_ref, sems, *, ff_chunks):
    @pl.when(pl.program_id(0) == 0)
    def _():
        copies = [pltpu.make_async_copy(src, dst, sems.at[k]) for k, (src, dst) in enumerate(
            ((wo_hbm, wo_ref), (w1_hbm, w1_ref), (w3_hbm, w3_ref), (w2_hbm, w2_ref)))]
        for cp in copies:
            cp.start()
        for cp in copies:
            cp.wait()

    n_a = ma_ref.shape[1]
    x1 = (x_ref[...] + jnp.dot(ma_ref[...], wo_ref[:n_a, :], preferred_element_type=F32)
          + jnp.dot(mb_ref[...], wo_ref[n_a:, :], preferred_element_type=F32))
    h = _rms(x1, nw_ref[...]).astype(BF16)
    out_ref[...] = x1
    for lo, hi in ff_chunks:
        a = jnp.dot(h, w1_ref[:, lo:hi], preferred_element_type=F32)
        g = jnp.dot(h, w3_ref[:, lo:hi], preferred_element_type=F32)
        out_ref[...] += jnp.dot((_silu(a) * g).astype(BF16), w2_ref[lo:hi, :], preferred_element_type=F32)


def _ff_chunks(d_ff, max_chunk=1024):
    assert d_ff % MXU_WIDTH == 0 and max_chunk % MXU_WIDTH == 0
    bounds = list(range(0, d_ff, max_chunk)) + [d_ff]
    return tuple(zip(bounds[:-1], bounds[1:]))


def _outproj_ffn(x2d, mix_a, mix_b, w_out, ffn_norm, w1, w3, w2, tm=512):
    n, d = x2d.shape
    tm = min(tm, n)
    d_ff = w1.shape[1]
    n_a, n_b = mix_a.shape[1], mix_b.shape[1]
    row = lambda w: pl.BlockSpec((tm, w), lambda i: (i, 0))
    hbm = pl.BlockSpec(memory_space=pl.ANY)
    return pl.pallas_call(
        functools.partial(_outproj_ffn_kernel, ff_chunks=_ff_chunks(d_ff)),
        grid=(n // tm,),
        in_specs=[row(d), row(n_a), row(n_b), pl.BlockSpec((1, d), lambda i: (0, 0)), hbm, hbm, hbm, hbm],
        out_specs=row(d),
        out_shape=jax.ShapeDtypeStruct((n, d), F32),
        scratch_shapes=[pltpu.VMEM(w_out.shape, BF16), pltpu.VMEM(w1.shape, BF16), pltpu.VMEM(w3.shape, BF16),
                        pltpu.VMEM(w2.shape, BF16), pltpu.SemaphoreType.DMA((4,))],
        compiler_params=_cparams(("arbitrary",)),
        name="outproj_ffn",
    )(x2d, mix_a, mix_b, ffn_norm.reshape(1, d), w_out, w1, w3, w2)


def _outproj_router_kernel(x_ref, m_ref, wo_ref, nw_ref, r_ref, x3_ref, hn_ref, idx_ref, gate_ref):
    x3 = x_ref[...] + jnp.dot(m_ref[...], wo_ref[...], preferred_element_type=F32)
    x3_ref[...] = x3
    hn = _rms(x3, nw_ref[...])
    hn_ref[...] = hn
    hn_hi = hn.astype(BF16)
    hn_lo = (hn - hn_hi.astype(F32)).astype(BF16)
    r_hi, r_lo = r_ref[:, :GATE_LANES], r_ref[:, GATE_LANES:]
    logits = (jnp.dot(hn_hi, r_hi, preferred_element_type=F32)
              + (jnp.dot(hn_hi, r_lo, preferred_element_type=F32)
                 + jnp.dot(hn_lo, r_hi, preferred_element_type=F32)))
    lane = lax.broadcasted_iota(jnp.int32, logits.shape, 1)
    logits = jnp.where(lane < N_EXPERTS, logits, -jnp.inf)
    m0 = jnp.max(logits, axis=-1, keepdims=True)
    i0 = jnp.min(jnp.where(logits == m0, lane, GATE_LANES), axis=-1, keepdims=True)
    rest = jnp.where(lane == i0, -jnp.inf, logits)
    m1 = jnp.max(rest, axis=-1, keepdims=True)
    i1 = jnp.min(jnp.where(rest == m1, lane, GATE_LANES), axis=-1, keepdims=True)
    e1 = jnp.exp(m1 - m0)
    inv = 1.0 / (1.0 + e1)
    idx_ref[...] = jnp.where(lane == 0, i0, jnp.where(lane == 1, i1, 0))
    gate_ref[...] = jnp.where(lane == 0, inv, jnp.where(lane == 1, e1 * inv, 0.0))


def _outproj_router(x2d, mix, w_out, ffn_norm, router, tm=1024):
    n, d = x2d.shape
    tm = min(tm, n)
    n_m = mix.shape[1]
    router_pad = jnp.zeros((d, GATE_LANES), F32).at[:, :N_EXPERTS].set(router.astype(F32))
    router_hi = router_pad.astype(BF16)
    router_lo = (router_pad - router_hi.astype(F32)).astype(BF16)
    router_split = jnp.concatenate([router_hi, router_lo], axis=1)
    row = lambda w: pl.BlockSpec((tm, w), lambda i: (i, 0))
    return pl.pallas_call(
        _outproj_router_kernel,
        grid=(n // tm,),
        in_specs=[row(d), row(n_m),
                  pl.BlockSpec((n_m, d), lambda i: (0, 0)),
                  pl.BlockSpec((1, d), lambda i: (0, 0)),
                  pl.BlockSpec((d, 2 * GATE_LANES), lambda i: (0, 0))],
        out_specs=[row(d), row(d), row(GATE_LANES), row(GATE_LANES)],
        out_shape=[jax.ShapeDtypeStruct((n, d), F32), jax.ShapeDtypeStruct((n, d), F32),
                   jax.ShapeDtypeStruct((n, GATE_LANES), jnp.int32),
                   jax.ShapeDtypeStruct((n, GATE_LANES), F32)],
        compiler_params=_cparams(("parallel",)),
        name="outproj_router",
    )(x2d, mix, w_out, ffn_norm.reshape(1, d), router_split)


def _rank_kernel(idx_ref, tri_ref, rank_ref, count_ref, carry_ref):
    @pl.when(pl.program_id(0) == 0)
    def _():
        carry_ref[...] = jnp.zeros_like(carry_ref)

    idx = idx_ref[...]
    lane = lax.broadcasted_iota(jnp.int32, idx.shape, 1)
    e0, e1 = idx[:, 0:1], idx[:, 1:2]
    member = ((lane == e0) | (lane == e1)).astype(F32)
    before = _dot(tri_ref[...], member) + carry_ref[...]
    r0 = jnp.sum(jnp.where(lane == e0, before, 0.0), axis=-1, keepdims=True)
    r1 = jnp.sum(jnp.where(lane == e1, before, 0.0), axis=-1, keepdims=True)
    packed = jnp.where(lane == 0, r0, jnp.where(lane == 1, r1, jnp.where(
        lane == 2, e0.astype(F32), jnp.where(lane == 3, e1.astype(F32), 0.0))))
    rank_ref[...] = jnp.transpose(packed)[0:8, :].astype(jnp.int32)
    carry_ref[...] += jnp.sum(member, axis=0, keepdims=True)
    count_ref[...] = carry_ref[...].astype(jnp.int32)


def _expert_ranks(ridx, tm=512):
    n = ridx.shape[0]
    tm = min(tm, n)
    i = np.arange(tm)
    tri = jnp.asarray((i[None, :] < i[:, None]).astype(np.float32)).astype(BF16)
    return pl.pallas_call(
        _rank_kernel,
        grid=(n // tm,),
        in_specs=[pl.BlockSpec((tm, GATE_LANES), lambda i: (i, 0)),
                  pl.BlockSpec((tm, tm), lambda i: (0, 0))],
        out_specs=[pl.BlockSpec((8, tm), lambda i: (0, i)),
                   pl.BlockSpec((1, GATE_LANES), lambda i: (0, 0))],
        out_shape=[jax.ShapeDtypeStruct((8, n), jnp.int32),
                   jax.ShapeDtypeStruct((1, GATE_LANES), jnp.int32)],
        scratch_shapes=[pltpu.VMEM((1, GATE_LANES), F32)],
        compiler_params=_cparams(("arbitrary",)),
        name="expert_ranks",
    )(ridx, tri)


def _row_copy(src_ref, src_row, dst_ref, dst_row, sem):
    return pltpu.make_async_copy(src_ref.at[pl.ds(src_row, 1)], dst_ref.at[pl.ds(dst_row, 1)], sem)


def _dispatch_kernel(dest_ref, fill_ref, hn_ref, slots_ref, buf_ref, zero_ref, load_sems, row_sems, fill_sem,
                     *, tm, n_tok, n_fill):
    i = pl.program_id(0)

    @pl.when(i == 0)
    def _():
        zero_ref[...] = jnp.zeros_like(zero_ref)
        for wait in (False, True):
            for s in range(n_fill):
                def fill(r, carry):
                    cp = _row_copy(zero_ref, 0, slots_ref, r, fill_sem)
                    cp.wait() if wait else cp.start()
                    return carry
                lax.fori_loop(fill_ref[2 * s], fill_ref[2 * s + 1], fill, 0)

    last = pl.num_programs(0) - 1
    base = i * tm
    cur = i % 3
    rsem = i % 2

    groups = tm // SUBLANES

    def load(tile, b):
        return pltpu.make_async_copy(hn_ref.at[pl.ds(tile * groups, groups)], buf_ref.at[b], load_sems.at[b])

    def drain(s):
        for _ in range(2):
            pltpu.make_async_copy(hn_ref.at[pl.ds(0, groups)], buf_ref.at[0], row_sems.at[s]).wait()

    @pl.when(i == 0)
    def _():
        load(0, 0).start()

    @pl.when(i < last)
    def _():
        load(i + 1, (i + 1) % 3).start()

    load(i, cur).wait()

    def start(g, carry):
        for u in range(SUBLANES):
            r = base + g * SUBLANES + u
            for k in range(2):
                pltpu.make_async_copy(buf_ref.at[cur, g, pl.ds(u, 1)], slots_ref.at[pl.ds(dest_ref[k * n_tok + r], 1)],
                                      row_sems.at[rsem]).start(priority=k)
        return carry

    lax.fori_loop(0, groups, start, 0, unroll=DMA_UNROLL // SUBLANES)

    @pl.when(i > 0)
    def _():
        drain(1 - rsem)

    @pl.when(i == last)
    def _():
        drain(rsem)


def _dispatch(dest_flat, fill_ranges, hn, n_slots, tm=512):
    n, d = hn.shape
    tm = min(tm, n)
    grid_spec = pltpu.PrefetchScalarGridSpec(
        num_scalar_prefetch=2,
        grid=(n // tm,),
        in_specs=[pl.BlockSpec(memory_space=pl.ANY)],
        out_specs=pl.BlockSpec(memory_space=pl.ANY),
        scratch_shapes=[pltpu.VMEM((3, tm // SUBLANES, SUBLANES, d), hn.dtype), pltpu.VMEM((SUBLANES, d), hn.dtype),
                        pltpu.SemaphoreType.DMA((3,)), pltpu.SemaphoreType.DMA((2,)),
                        pltpu.SemaphoreType.DMA(())],
    )
    return pl.pallas_call(
        functools.partial(_dispatch_kernel, tm=tm, n_tok=n, n_fill=fill_ranges.shape[0] // 2),
        grid_spec=grid_spec,
        out_shape=jax.ShapeDtypeStruct((n_slots, d), hn.dtype),
        compiler_params=_cparams(("arbitrary",)),
        name="row_dispatch",
    )(dest_flat, fill_ranges, hn.reshape(n // SUBLANES, SUBLANES, d))


def _moe_kernel(be_ref, nused_ref, x_ref, w1_hbm, w3_hbm, w2_hbm, out_ref, w1_ref, w3_ref, w2_ref,
                stage_cols_ref, stage_rows_ref, sems, *, ff_chunks, load_chunk):
    i = pl.program_id(0)
    used = i < nused_ref[0]
    expert = be_ref[i]
    prev_expert = be_ref[jnp.maximum(i - 1, 0)]
    d_ff = w1_ref.shape[1]

    @pl.when(used & ((i == 0) | (expert != prev_expert)))
    def _():
        pieces = []
        for lo in range(0, d_ff, load_chunk):
            cols = pl.ds(lo, load_chunk)
            pieces.append((w1_hbm.at[expert, :, cols], stage_cols_ref, w1_ref.at[:, cols]))
            pieces.append((w3_hbm.at[expert, :, cols], stage_cols_ref, w3_ref.at[:, cols]))
        for lo in range(0, d_ff, load_chunk):
            pieces.append((w2_hbm.at[expert, pl.ds(lo, load_chunk), :], stage_rows_ref,
                           w2_ref.at[pl.ds(lo, load_chunk), :]))
        copies = [pltpu.make_async_copy(src, ring.at[k % 2], sems.at[k % 2])
                  for k, (src, ring, _) in enumerate(pieces)]
        copies[0].start()
        for k, (_, ring, dst) in enumerate(pieces):
            if k + 1 < len(pieces):
                copies[k + 1].start()
            copies[k].wait()
            dst[...] = ring[k % 2].astype(BF16)

    @pl.when(used)
    def _():
        x = x_ref[...].astype(BF16)
        for n_done, (lo, hi) in enumerate(ff_chunks):
            a = jnp.dot(x, w1_ref[:, lo:hi], preferred_element_type=F32)
            g = jnp.dot(x, w3_ref[:, lo:hi], preferred_element_type=F32)
            y = jnp.dot((_silu(a) * g).astype(BF16), w2_ref[lo:hi, :], preferred_element_type=F32)
            if n_done == 0:
                out_ref[...] = y
            else:
                out_ref[...] += y

    @pl.when(jnp.logical_not(used))
    def _():
        out_ref[...] = jnp.zeros_like(out_ref)


def _moe_ffn(block_expert, n_used, slots, w1, w3, w2, load_chunk=512):
    n_slots, d = slots.shape
    n_blocks = n_slots // MOE_BLOCK
    d_ff = w1.shape[2]
    assert d_ff % load_chunk == 0
    hbm = pl.BlockSpec(memory_space=pl.ANY)
    grid_spec = pltpu.PrefetchScalarGridSpec(
        num_scalar_prefetch=2,
        grid=(n_blocks,),
        in_specs=[pl.BlockSpec((MOE_BLOCK, d), lambda i, be, nu: (jnp.minimum(i, nu[0] - 1), 0)), hbm, hbm, hbm],
        out_specs=pl.BlockSpec((MOE_BLOCK, d), lambda i, be, nu: (i, 0)),
        scratch_shapes=[pltpu.VMEM((d, d_ff), BF16), pltpu.VMEM((d, d_ff), BF16), pltpu.VMEM((d_ff, d), BF16),
                        pltpu.VMEM((2, d, load_chunk), w1.dtype), pltpu.VMEM((2, load_chunk, d), w2.dtype),
                        pltpu.SemaphoreType.DMA((2,))],
    )
    return pl.pallas_call(
        functools.partial(_moe_kernel, ff_chunks=_ff_chunks(d_ff), load_chunk=load_chunk),
        grid_spec=grid_spec,
        out_shape=jax.ShapeDtypeStruct((n_slots, d), F32),
        compiler_params=_cparams(("arbitrary",)),
        name="moe_ffn",
    )(block_expert, n_used, slots, w1, w3, w2)


def _combine_kernel(dest_ref, x_ref, gate_ref, nw_ref, y_ref, y_tiles_ref, out_ref, ybuf_ref, sems, *, tm, n_tok):
    i = pl.program_id(0)
    slot = i % 2
    d = x_ref.shape[1]

    def gather(tile, s):
        base = tile * tm

        def start(g, carry):
            for u in range(SUBLANES):
                r = base + g * SUBLANES + u
                for k in range(2):
                    pltpu.make_async_copy(y_ref.at[pl.ds(dest_ref[k * n_tok + r], 1)],
                                          ybuf_ref.at[s, k, g, pl.ds(u, 1)], sems.at[s]).start(priority=k)
            return carry

        lax.fori_loop(0, tm // SUBLANES, start, 0, unroll=DMA_UNROLL // SUBLANES)

    @pl.when(i == 0)
    def _():
        gather(0, 0)

    @pl.when(i + 1 < pl.num_programs(0))
    def _():
        gather(i + 1, 1 - slot)

    for k in range(2):
        pltpu.make_async_copy(y_tiles_ref.at[pl.ds(0, tm // SUBLANES)], ybuf_ref.at[slot, k], sems.at[slot]).wait()
    gate = gate_ref[...]
    y0 = ybuf_ref[slot, 0].reshape(tm, d)
    y1 = ybuf_ref[slot, 1].reshape(tm, d)
    out_ref[...] = _rms(x_ref[...] + (y0 * gate[:, 0:1] + y1 * gate[:, 1:2]), nw_ref[...])


def _combine(dest_flat, x2d, rgate, final_norm, yb, tm=512):
    n, d = x2d.shape
    tm = min(tm, n)
    grid_spec = pltpu.PrefetchScalarGridSpec(
        num_scalar_prefetch=1,
        grid=(n // tm,),
        in_specs=[pl.BlockSpec((tm, d), lambda i, dest: (i, 0)),
                  pl.BlockSpec((tm, GATE_LANES), lambda i, dest: (i, 0)),
                  pl.BlockSpec((1, d), lambda i, dest: (0, 0)),
                  pl.BlockSpec(memory_space=pl.ANY), pl.BlockSpec(memory_space=pl.ANY)],
        out_specs=pl.BlockSpec((tm, d), lambda i, dest: (i, 0)),
        scratch_shapes=[pltpu.VMEM((2, 2, tm // SUBLANES, SUBLANES, d), F32), pltpu.SemaphoreType.DMA((2,))],
    )
    return pl.pallas_call(
        functools.partial(_combine_kernel, tm=tm, n_tok=n),
        grid_spec=grid_spec,
        out_shape=jax.ShapeDtypeStruct((n, d), F32),
        compiler_params=_cparams(("arbitrary",)),
        name="combine_norm",
    )(dest_flat, x2d, rgate, final_norm.reshape(1, d), yb, yb.reshape(-1, SUBLANES, d))


def _split_in_proj(w_in, lo, hi):
    main = jnp.concatenate([w_in[:, :lo], w_in[:, hi:]], axis=1)
    gate = jnp.pad(w_in[:, lo:hi], ((0, 0), (0, GATE_LANES - (hi - lo))))
    return jnp.concatenate([main, gate], axis=1).astype(BF16), main.shape[1]


def kernel(x, mix_norm_0, w_in_0, conv_w_0, a_log_0, dt_bias_0, gdn_norm_0, ret_norm_0, w_out_0, ffn_norm_0,
           ffn_w1_0, ffn_w3_0, ffn_w2_0, mix_norm_1, w_in_1, gate_bias_1, mlstm_norm_1, w_out_1, ffn_norm_1,
           router_1, exp_w1_1, exp_w3_1, exp_w2_1, final_norm):
    b, t, d = x.shape
    n = b * t
    x2d = x.reshape(n, d)

    gdn_qkv = N_HEADS * (2 * HEAD_DK + GDN_DV)
    w0, n_main0 = _split_in_proj(w_in_0, gdn_qkv, gdn_qkv + 2 * N_HEADS)
    proj0, gates0 = _norm_proj(x2d, mix_norm_0, w0, n_main0)
    o_gdn = _gdn(proj0, gates0, _gates_transposed(gates0, b, t, CHUNK), conv_w_0, a_log_0, dt_bias_0, gdn_norm_0, b, t)
    o_ret = _retention(proj0, ret_norm_0, b, t)
    x2 = _outproj_ffn(x2d, o_gdn.reshape(n, -1), o_ret.reshape(n, -1), w_out_0.astype(BF16), ffn_norm_0,
                      ffn_w1_0.astype(BF16), ffn_w3_0.astype(BF16), ffn_w2_0.astype(BF16))

    n_main1 = w_in_1.shape[1] - 2 * N_HEADS
    w1, _ = _split_in_proj(w_in_1, n_main1, n_main1 + 2 * N_HEADS)
    proj1, gates1 = _norm_proj(x2, mix_norm_1, w1, n_main1)
    h_mix = _mlstm(proj1, gates1, _gates_transposed(gates1, b, t, MLSTM_CHUNK), gate_bias_1, mlstm_norm_1, b, t)
    x3, hn, ridx, rgate = _outproj_router(x2, h_mix.reshape(n, -1), w_out_1.astype(BF16), ffn_norm_1, router_1)

    rank, counts = _expert_ranks(ridx)
    counts = counts[0, :N_EXPERTS]
    padded = (counts + MOE_BLOCK - 1) // MOE_BLOCK * MOE_BLOCK
    pad_end = jnp.cumsum(padded)
    pad_start = pad_end - padded
    pair_expert = rank[2:4].reshape(-1)
    pair_start = jnp.sum(jnp.where(pair_expert[:, None] == jnp.arange(N_EXPERTS, dtype=jnp.int32)[None, :],
                                   pad_start[None, :], 0), axis=1)
    dest_flat = pair_start + rank[0:2].reshape(-1)
    n_blocks = -(-(2 * n) // MOE_BLOCK) + N_EXPERTS
    block_row0 = jnp.arange(n_blocks, dtype=jnp.int32) * MOE_BLOCK
    block_expert = jnp.minimum(jnp.sum((pad_end[None, :] <= block_row0[:, None]).astype(jnp.int32), axis=1),
                               N_EXPERTS - 1)
    n_used = (pad_end[-1:] // MOE_BLOCK).astype(jnp.int32)

    n_slots = n_blocks * MOE_BLOCK
    fill_lo = jnp.concatenate([pad_start + counts, pad_end[-1:]])
    fill_hi = jnp.concatenate([pad_end, jnp.full((1,), n_slots, jnp.int32)])
    fill_ranges = jnp.stack([fill_lo, fill_hi], axis=1).reshape(-1).astype(jnp.int32)
    slots = _dispatch(dest_flat, fill_ranges, hn, n_slots)
    yb = _moe_ffn(block_expert, n_used, slots, exp_w1_1, exp_w3_1, exp_w2_1)
    out = _combine(dest_flat, x3, rgate, final_norm, yb)
    return out.reshape(b, t, d)
```

```python
import functools
import math

import numpy as np
import jax
import jax.numpy as jnp
from jax import lax
from jax.experimental import pallas as pl
from jax.experimental.pallas import tpu as pltpu

F32 = jnp.float32
BF16 = jnp.bfloat16
HIGHEST = lax.Precision.HIGHEST

EPS = 1e-6
CHUNK = 128
RET_CHUNK = 128
MLSTM_CHUNK = 128
CONV_WIDTH = 4
N_HEADS = 8
HEAD_DK = 64
GDN_DV = 64
RET_DV = 64
MLSTM_DV = 128
ROPE_BASE = 10000.0
N_EXPERTS = 8
MOE_BLOCK = 512
GATE_LANES = 128
MXU_WIDTH = 256
VMEM_LIMIT = 52 * 1024 * 1024
SUBLANES = 8
DMA_UNROLL = 16


def _cparams(sem):
    return pltpu.CompilerParams(dimension_semantics=sem, vmem_limit_bytes=VMEM_LIMIT)


def _dot(a, b):
    return jnp.dot(a.astype(BF16), b.astype(BF16), preferred_element_type=F32)


def _dot_nt(a, b):
    return lax.dot_general(a.astype(BF16), b.astype(BF16), (((1,), (1,)), ((), ())),
                           preferred_element_type=F32)


def _dot_tn(a, b):
    return lax.dot_general(a.astype(BF16), b.astype(BF16), (((0,), (0,)), ((), ())),
                           preferred_element_type=F32)


def _dot_f32(a, b):
    return jnp.dot(a, b, precision=HIGHEST, preferred_element_type=F32)


def _rms(x, w):
    return x * lax.rsqrt(jnp.mean(x * x, axis=-1, keepdims=True) + EPS) * w


def _silu(x):
    return x * (1.0 / (1.0 + jnp.exp(-x)))


def _sigmoid(x):
    return 1.0 / (1.0 + jnp.exp(-x))


def _softplus(x):
    return jnp.maximum(x, 0.0) + jnp.log1p(jnp.exp(-jnp.abs(x)))


def _causal(n):
    r = lax.broadcasted_iota(jnp.int32, (n, n), 0)
    c = lax.broadcasted_iota(jnp.int32, (n, n), 1)
    return r >= c


def _norm_proj_kernel(x_ref, nw_ref, w_ref, conv_ref, main_ref, gate_ref, cbuf_ref, *, n_main, col_chunk,
                      conv_cols, tiles_per_seq):
    tm = x_ref.shape[0]
    h = _rms(x_ref[...], nw_ref[...]).astype(BF16)
    if conv_cols:
        @pl.when(pl.program_id(0) % tiles_per_seq == 0)
        def _():
            cbuf_ref[0:SUBLANES, :] = jnp.zeros((SUBLANES, conv_cols), F32)
    for c0 in range(0, n_main, col_chunk):
        cols = slice(c0, c0 + col_chunk)
        y = jnp.dot(h, w_ref[:, cols], preferred_element_type=F32)
        if c0 < conv_cols:
            cbuf_ref[SUBLANES:SUBLANES + tm, cols] = y
            pre = None
            for j in range(CONV_WIDTH):
                term = cbuf_ref[pl.ds(SUBLANES - (CONV_WIDTH - 1) + j, tm), cols] * conv_ref[j:j + 1, cols]
                pre = term if pre is None else pre + term
            cbuf_ref[0:SUBLANES, cols] = cbuf_ref[tm:tm + SUBLANES, cols]
            y = _silu(pre)
        main_ref[:, cols] = y.astype(BF16)
    gate_ref[...] = jnp.dot(h, w_ref[:, n_main:], preferred_element_type=F32)


def _norm_proj(x2d, norm_w, w_cat, n_main, seq_len, conv_w=None, tm=512, col_chunk=256):
    n, d = x2d.shape
    tm = min(tm, n, seq_len)
    n_cat = w_cat.shape[1]
    conv_cols = 0 if conv_w is None else conv_w.shape[1]
    assert seq_len % tm == 0 and conv_cols % col_chunk == 0
    if conv_w is None:
        conv_w = jnp.zeros((CONV_WIDTH, GATE_LANES), F32)
    return pl.pallas_call(
        functools.partial(_norm_proj_kernel, n_main=n_main, col_chunk=col_chunk, conv_cols=conv_cols,
                          tiles_per_seq=seq_len // tm),
        grid=(n // tm,),
        in_specs=[pl.BlockSpec((tm, d), lambda i: (i, 0)),
                  pl.BlockSpec((1, d), lambda i: (0, 0)),
                  pl.BlockSpec((d, n_cat), lambda i: (0, 0)),
                  pl.BlockSpec(conv_w.shape, lambda i: (0, 0))],
        out_specs=[pl.BlockSpec((tm, n_main), lambda i: (i, 0)),
                   pl.BlockSpec((tm, GATE_LANES), lambda i: (i, 0))],
        out_shape=[jax.ShapeDtypeStruct((n, n_main), BF16),
                   jax.ShapeDtypeStruct((n, GATE_LANES), F32)],
        scratch_shapes=[pltpu.VMEM((tm + SUBLANES, max(conv_cols, GATE_LANES)), F32)],
        compiler_params=_cparams(("arbitrary",)),
        name="norm_proj",
    )(x2d, norm_w.reshape(1, d), w_cat, conv_w.astype(F32))


def _gdn_kernel(qkv_ref, z_ref, gate_ref, gate_t_ref, arow_ref, dtrow_ref, acol_ref, dtcol_ref,
                norm_ref, ltri_ref, utri_ref, lvl_ref, ones_ref, out_ref, state_ref):
    n_b = qkv_ref.shape[0]
    n_qk = N_HEADS * HEAD_DK

    @pl.when(pl.program_id(1) == 0)
    def _():
        state_ref[...] = jnp.zeros_like(state_ref)

    causal = _causal(CHUNK)
    eye = (lax.broadcasted_iota(jnp.int32, (CHUNK, CHUNK), 0)
           == lax.broadcasted_iota(jnp.int32, (CHUNK, CHUNK), 1)).astype(F32)
    sl = [slice(h * HEAD_DK, (h + 1) * HEAD_DK) for h in range(N_HEADS)]

    q, k, v, bc, gcc, gcr = [], [], [], [], [], []
    for i in range(n_b):
        act = qkv_ref[i].astype(F32)
        q_all, k_all, v_all = act[:, :n_qk], act[:, n_qk:2 * n_qk], act[:, 2 * n_qk:]
        q_all = q_all * lax.rsqrt(_dot(q_all * q_all, ones_ref[...]) + EPS) * (HEAD_DK ** -0.5)
        k_all = k_all * lax.rsqrt(_dot(k_all * k_all, ones_ref[...]) + EPS)
        gl = gate_ref[i]
        beta_c = _sigmoid(gl)
        g_c = -jnp.exp(arow_ref[...]) * _softplus(gl + dtrow_ref[...])
        gc_c = _dot_f32(ltri_ref[...], g_c)
        g_r = -jnp.exp(acol_ref[...]) * _softplus(gate_t_ref[i] + dtcol_ref[...])
        gc_r = _dot_f32(g_r, utri_ref[...])
        for h in range(N_HEADS):
            q.append(q_all[:, sl[h]])
            k.append(k_all[:, sl[h]])
            v.append(v_all[:, sl[h]])
            bc.append(beta_c[:, h:h + 1])
            gcc.append(gc_c[:, 8 + h:9 + h])
            gcr.append(gc_r[8 + h:9 + h, :])

    chains = range(n_b * N_HEADS)
    decay = [jnp.exp(jnp.where(causal, gcc[c] - gcr[c], -jnp.inf)) for c in chains]
    kb = [k[c] * bc[c] for c in chains]
    scores = [_dot_nt(jnp.concatenate([kb[c], q[c]], axis=0), k[c]) for c in chains]
    lmat = [scores[c][:CHUNK] * decay[c] for c in chains]
    qk = [scores[c][CHUNK:] * decay[c] for c in chains]
    tinv = [eye - lmat[c] * lvl_ref[0].astype(F32) for c in chains]
    lmat_mxu = [lmat[c].astype(BF16) for c in chains]
    for lvl in range(1, lvl_ref.shape[0]):
        inner = [_dot(lmat_mxu[c] * lvl_ref[lvl], tinv[c]) for c in chains]
        tinv = [tinv[c] - _dot(tinv[c], inner[c]) for c in chains]
    egc = [jnp.exp(gcc[c]) for c in chains]
    uw = [_dot(tinv[c], jnp.concatenate([v[c] * bc[c], kb[c] * egc[c]], axis=1)) for c in chains]
    state = [state_ref[c] for c in chains]
    v_new = [uw[c][:, :GDN_DV] - _dot(uw[c][:, GDN_DV:], state[c]) for c in chains]
    o = [_dot(q[c] * egc[c], state[c]) + _dot(qk[c], v_new[c]) for c in chains]
    g_last = [gcc[c][CHUNK - 1:CHUNK, :] for c in chains]
    k_dec = [k[c] * jnp.exp(g_last[c] - gcc[c]) for c in chains]
    for c in chains:
        state_ref[c] = state[c] * jnp.exp(g_last[c]) + _dot_tn(k_dec[c], v_new[c])
    for i in range(n_b):
        o_all = jnp.concatenate(o[i * N_HEADS:(i + 1) * N_HEADS], axis=1)
        mean_sq = _dot(o_all * o_all, ones_ref[...]) * (1.0 / GDN_DV)
        out_ref[i] = (o_all * lax.rsqrt(mean_sq + EPS) * norm_ref[...]
                      * _silu(z_ref[i].astype(F32))).astype(out_ref.dtype)


def _level_masks():
    i = np.arange(CHUNK)[:, None]
    j = np.arange(CHUNK)[None, :]
    masks = []
    for lvl in range(CHUNK.bit_length() - 1):
        same_parent = (i >> (lvl + 1)) == (j >> (lvl + 1))
        masks.append((same_parent & (((i >> lvl) & 1) == 1) & (((j >> lvl) & 1) == 0)).astype(np.float32))
    return jnp.asarray(np.stack(masks)).astype(BF16)


def _tri_consts(ck):
    i = np.arange(ck)[:, None]
    j = np.arange(ck)[None, :]
    return jnp.asarray((j <= i).astype(np.float32)), jnp.asarray((i <= j).astype(np.float32))


def _lane_row(vals, offset):
    return jnp.zeros((1, GATE_LANES), F32).at[0, offset:offset + vals.shape[0]].set(vals.astype(F32))


def _sub_col(vals, offset):
    return jnp.zeros((16, 1), F32).at[offset:offset + vals.shape[0], 0].set(vals.astype(F32))


def _head_ones(n_heads, width):
    g = np.arange(n_heads * width) // width
    return jnp.asarray((g[:, None] == g[None, :]).astype(np.float32)).astype(BF16)


def _batch_per_step(b, want=2):
    return max(k for k in range(1, want + 1) if b % k == 0)


def _gdn(proj, gates, gates_t, a_log, dt_bias, gdn_norm, b, t):
    nc = t // CHUNK
    nb = _batch_per_step(b, want=2)
    n_qkv = N_HEADS * (2 * HEAD_DK + GDN_DV)
    n_z = N_HEADS * GDN_DV
    proj3 = proj.reshape(b, t, proj.shape[-1])
    ltri, utri = _tri_consts(CHUNK)
    const2 = lambda shape: pl.BlockSpec(shape, lambda i, c: (0,) * len(shape))
    return pl.pallas_call(
        _gdn_kernel,
        grid=(b // nb, nc),
        in_specs=[pl.BlockSpec((nb, CHUNK, n_qkv), lambda i, c: (i, c, 0)),
                  pl.BlockSpec((nb, CHUNK, n_z), lambda i, c: (i, c, n_qkv // n_z)),
                  pl.BlockSpec((nb, CHUNK, GATE_LANES), lambda i, c: (i, c, 0)),
                  pl.BlockSpec((nb, None, 16, CHUNK), lambda i, c: (i, c, 0, 0)),
                  const2((1, GATE_LANES)), const2((1, GATE_LANES)),
                  const2((16, 1)), const2((16, 1)), const2((1, n_z)),
                  const2((CHUNK, CHUNK)), const2((CHUNK, CHUNK)),
                  const2((CHUNK.bit_length() - 1, CHUNK, CHUNK)),
                  const2((n_z, n_z))],
        out_specs=pl.BlockSpec((nb, CHUNK, n_z), lambda i, c: (i, c, 0)),
        out_shape=jax.ShapeDtypeStruct((b, t, n_z), BF16),
        scratch_shapes=[pltpu.VMEM((nb * N_HEADS, HEAD_DK, GDN_DV), F32)],
        compiler_params=_cparams(("parallel", "arbitrary")),
        name="gdn_mixer",
    )(proj3, proj3, gates.reshape(b, t, GATE_LANES), gates_t,
      _lane_row(a_log, 8), _lane_row(dt_bias, 8), _sub_col(a_log, 8), _sub_col(dt_bias, 8),
      jnp.tile(gdn_norm.astype(F32), N_HEADS).reshape(1, n_z), ltri, utri, _level_masks(),
      _head_ones(N_HEADS, HEAD_DK))


def _ret_kernel(q_ref, k_ref, v_ref, g_ref, cos_ref, sin_ref, decay_ref, qs_ref, ks_ref, cd_ref, norm_ref,
                ones_ref, out_ref, state_ref):
    @pl.when(pl.program_id(1) == 0)
    def _():
        state_ref[...] = jnp.zeros_like(state_ref)

    width = N_HEADS * HEAD_DK
    half = HEAD_DK // 2
    first_half = (lax.broadcasted_iota(jnp.int32, (q_ref.shape[1], width), 1) % HEAD_DK) < half

    def rot(t_in):
        swapped = jnp.where(first_half, pltpu.roll(t_in, width - half, 1), pltpu.roll(t_in, half, 1))
        return t_in * cos_ref[...] + swapped * sin_ref[...]

    n_b = q_ref.shape[0]
    sl = [slice(h * HEAD_DK, (h + 1) * HEAD_DK) for h in range(N_HEADS)]
    q, k, v, q_in, k_st = [], [], [], [], []
    for i in range(n_b):
        q_all = rot(q_ref[i].astype(F32))
        k_all = rot(k_ref[i].astype(F32)) * (HEAD_DK ** -0.5)
        v_all = v_ref[i].astype(F32)
        qi_all = q_all * qs_ref[...]
        ks_all = k_all * ks_ref[...]
        for h in range(N_HEADS):
            q.append(q_all[:, sl[h]])
            k.append(k_all[:, sl[h]])
            v.append(v_all[:, sl[h]])
            q_in.append(qi_all[:, sl[h]])
            k_st.append(ks_all[:, sl[h]])
    chains = range(n_b * N_HEADS)
    state = [state_ref[c] for c in chains]
    scores = [_dot_nt(q[c], k[c]) * decay_ref[c % N_HEADS] for c in chains]
    inter = [_dot(q_in[c], state[c]) for c in chains]
    o = [_dot(scores[c], v[c]) + inter[c] for c in chains]
    for c in chains:
        state_ref[c] = state[c] * cd_ref[c % N_HEADS] + _dot_tn(k_st[c], v[c])
    inv_dv = 1.0 / RET_DV
    for i in range(n_b):
        o_all = jnp.concatenate(o[i * N_HEADS:(i + 1) * N_HEADS], axis=1)
        xc = o_all - _dot(o_all, ones_ref[...]) * inv_dv
        var = _dot(xc * xc, ones_ref[...]) * inv_dv
        out_ref[i] = (xc * lax.rsqrt(var + EPS) * norm_ref[...]
                      * _silu(g_ref[i].astype(F32))).astype(out_ref.dtype)


def _retention(proj, ret_norm, b, t):
    ck = RET_CHUNK
    nc = t // ck
    width = N_HEADS * HEAD_DK
    half = HEAD_DK // 2
    proj3 = proj.reshape(b, t, proj.shape[-1])
    pos = jnp.arange(t, dtype=F32)
    inv_freq = jnp.power(ROPE_BASE, -jnp.arange(half, dtype=F32) / half)
    ang = pos[:, None] * inv_freq[None, :]
    cos, sin = jnp.cos(ang), jnp.sin(ang)
    cos_t = jnp.tile(jnp.concatenate([cos, cos], axis=1), (1, N_HEADS))
    sin_t = jnp.tile(jnp.concatenate([-sin, sin], axis=1), (1, N_HEADS))
    log_gamma = jnp.log1p(-jnp.exp2(-5.0 - jnp.arange(N_HEADS, dtype=F32)))
    idx = jnp.arange(ck, dtype=F32)
    causal = jnp.tril(jnp.ones((ck, ck), bool))
    decay = jnp.exp(jnp.where(causal, (idx[:, None] - idx[None, :]) * log_gamma[:, None, None], -jnp.inf))
    qs = jnp.repeat(jnp.exp((idx + 1.0) * log_gamma[:, None]).T, HEAD_DK, axis=1)
    ks = jnp.repeat(jnp.exp((ck - 1.0 - idx) * log_gamma[:, None]).T, HEAD_DK, axis=1)
    cd = jnp.broadcast_to(jnp.exp(ck * log_gamma)[:, None, None], (N_HEADS, 1, RET_DV))
    base = (N_HEADS * (2 * HEAD_DK + GDN_DV) + N_HEADS * GDN_DV) // width
    nb = _batch_per_step(b, want=4)
    col = lambda k: pl.BlockSpec((nb, ck, width), lambda i, c: (i, c, base + k))
    const2 = lambda shape: pl.BlockSpec(shape, lambda i, c: (0,) * len(shape))
    return pl.pallas_call(
        _ret_kernel,
        grid=(b // nb, nc),
        in_specs=[col(0), col(1), col(2), col(3),
                  pl.BlockSpec((ck, width), lambda i, c: (c, 0)),
                  pl.BlockSpec((ck, width), lambda i, c: (c, 0)),
                  const2((N_HEADS, ck, ck)), const2((ck, width)), const2((ck, width)),
                  const2((N_HEADS, 1, RET_DV)), const2((1, width)), const2((width, width))],
        out_specs=pl.BlockSpec((nb, ck, width), lambda i, c: (i, c, 0)),
        out_shape=jax.ShapeDtypeStruct((b, t, width), BF16),
        scratch_shapes=[pltpu.VMEM((nb * N_HEADS, HEAD_DK, RET_DV), F32)],
        compiler_params=_cparams(("parallel", "arbitrary")),
        name="retention_mixer",
    )(proj3, proj3, proj3, proj3, cos_t, sin_t, decay, qs, ks, cd, ret_norm.reshape(1, width).astype(F32),
      _head_ones(N_HEADS, RET_DV))


def _mlstm_kernel(q_ref, k_ref, v_ref, o_ref, gate_ref, gate_t_ref, brow_ref, bcol_ref, norm_ref,
                  ltri_ref, utri_ref, out_ref, state_ref, m_ref):
    @pl.when(pl.program_id(1) == 0)
    def _():
        state_ref[...] = jnp.zeros_like(state_ref)
        m_ref[...] = jnp.zeros_like(m_ref)

    n_b, ck = q_ref.shape[0], q_ref.shape[1]
    causal = _causal(ck)
    ones_col = (lax.broadcasted_iota(jnp.int32, (ck, MLSTM_DV), 1) == 0).astype(F32)
    sl = [slice(h * HEAD_DK, (h + 1) * HEAD_DK) for h in range(N_HEADS)]
    sv = [slice(h * MLSTM_DV, (h + 1) * MLSTM_DV) for h in range(N_HEADS)]

    q, k, vx, bc, br, ig_c, ig_r = [], [], [], [], [], [], []
    for i in range(n_b):
        q_all = q_ref[i].astype(F32)
        k_all = k_ref[i].astype(F32) * (HEAD_DK ** -0.5)
        v_all = v_ref[i].astype(F32)
        gl = gate_ref[i] + brow_ref[...]
        bcum_c = _dot_f32(ltri_ref[...], -_softplus(-gl))
        gt = gate_t_ref[i] + bcol_ref[...]
        bcum_r = _dot_f32(-_softplus(-gt), utri_ref[...])
        for h in range(N_HEADS):
            q.append(q_all[:, sl[h]])
            k.append(k_all[:, sl[h]])
            vx.append(jnp.concatenate([v_all[:, sv[h]], ones_col], axis=1))
            bc.append(bcum_c[:, 8 + h:9 + h])
            br.append(bcum_r[8 + h:9 + h, :])
            ig_c.append(gl[:, h:h + 1])
            ig_r.append(gt[h:h + 1, :])

    chains = range(n_b * N_HEADS)
    log_w = [jnp.where(causal, bc[c] - br[c] + ig_r[c], -jnp.inf) for c in chains]
    m_intra = [jnp.max(log_w[c], axis=-1, keepdims=True) for c in chains]
    b_end = [bc[c][ck - 1:ck, :] for c in chains]
    lw_end = [b_end[c] - bc[c] + ig_c[c] for c in chains]
    m_end = [jnp.max(lw_end[c], axis=0, keepdims=True) for c in chains]
    m_s = [m_ref[c][:, 0:1] for c in chains]
    cx = [state_ref[c] for c in chains]
    m_t = [jnp.maximum(bc[c] + m_s[c], m_intra[c]) for c in chains]
    inter = [jnp.exp(bc[c] + m_s[c] - m_t[c]) for c in chains]
    qk = [_dot_nt(q[c], k[c]) for c in chains]
    qc = [_dot(q[c], cx[c]) for c in chains]
    s = [qk[c] * jnp.exp(log_w[c] - m_t[c]) for c in chains]
    numx = [inter[c] * qc[c] + _dot(s[c], vx[c]) for c in chains]
    m_new = [jnp.maximum(b_end[c] + m_s[c], m_end[c]) for c in chains]
    wk = [k[c] * jnp.exp(lw_end[c] - m_new[c]) for c in chains]
    for c in chains:
        state_ref[c] = jnp.exp(b_end[c] + m_s[c] - m_new[c]) * cx[c] + _dot_tn(wk[c], vx[c])
        m_ref[c] = jnp.broadcast_to(m_new[c], (1, GATE_LANES))
    hh = [numx[c][:, :MLSTM_DV] / jnp.maximum(jnp.abs(numx[c][:, MLSTM_DV:MLSTM_DV + 1]), jnp.exp(-m_t[c]))
          for c in chains]
    outs = [_rms(hh[c], norm_ref[:, sv[c % N_HEADS]]) for c in chains]
    for i in range(n_b):
        out_ref[i] = (jnp.concatenate(outs[i * N_HEADS:(i + 1) * N_HEADS], axis=1)
                      * _sigmoid(o_ref[i].astype(F32))).astype(out_ref.dtype)


def _mlstm(proj, gates, gates_t, gate_bias, mlstm_norm, b, t):
    ck = MLSTM_CHUNK
    nc = t // ck
    n_qk = N_HEADS * HEAD_DK
    n_v = N_HEADS * MLSTM_DV
    proj3 = proj.reshape(b, t, proj.shape[-1])
    ltri, utri = _tri_consts(ck)
    nb = _batch_per_step(b, want=4)
    const2 = lambda shape: pl.BlockSpec(shape, lambda i, c: (0,) * len(shape))
    return pl.pallas_call(
        _mlstm_kernel,
        grid=(b // nb, nc),
        in_specs=[pl.BlockSpec((nb, ck, n_qk), lambda i, c: (i, c, 0)),
                  pl.BlockSpec((nb, ck, n_qk), lambda i, c: (i, c, 1)),
                  pl.BlockSpec((nb, ck, n_v), lambda i, c: (i, c, 2 * n_qk // n_v)),
                  pl.BlockSpec((nb, ck, n_v), lambda i, c: (i, c, 2 * n_qk // n_v + 1)),
                  pl.BlockSpec((nb, ck, GATE_LANES), lambda i, c: (i, c, 0)),
                  pl.BlockSpec((nb, None, 16, ck), lambda i, c: (i, c, 0, 0)),
                  const2((1, GATE_LANES)), const2((16, 1)), const2((1, n_v)),
                  const2((ck, ck)), const2((ck, ck))],
        out_specs=pl.BlockSpec((nb, ck, n_v), lambda i, c: (i, c, 0)),
        out_shape=jax.ShapeDtypeStruct((b, t, n_v), BF16),
        scratch_shapes=[pltpu.VMEM((nb * N_HEADS, HEAD_DK, 2 * MLSTM_DV), F32),
                        pltpu.VMEM((nb * N_HEADS, 1, GATE_LANES), F32)],
        compiler_params=_cparams(("parallel", "arbitrary")),
        name="mlstm_mixer",
    )(proj3, proj3, proj3, proj3, gates.reshape(b, t, GATE_LANES), gates_t,
      _lane_row(gate_bias, 0), _sub_col(gate_bias, 0), mlstm_norm.reshape(1, n_v).astype(F32), ltri, utri)


def _gates_transposed(gates, b, t, ck):
    return jnp.transpose(gates[:, :16].reshape(b, t // ck, ck, 16), (0, 1, 3, 2))


def _outproj_ffn_kernel(x_ref, ma_ref, mb_ref, nw_ref, wo_hbm, w1_hbm, w3_hbm, w2_hbm, out_ref,
                        wo_ref, w1_ref, w3_ref, w2_ref, sems, *, ff_chunks):
    @pl.when(pl.program_id(0) == 0)
    def _():
        copies = [pltpu.make_async_copy(src, dst, sems.at[k]) for k, (src, dst) in enumerate(
            ((wo_hbm, wo_ref), (w1_hbm, w1_ref), (w3_hbm, w3_ref), (w2_hbm, w2_ref)))]
        for cp in copies:
            cp.start()
        for cp in copies:
            cp.wait()

    n_a = ma_ref.shape[1]
    x1 = (x_ref[...] + jnp.dot(ma_ref[...], wo_ref[:n_a, :], preferred_element_type=F32)
          + jnp.dot(mb_ref[...], wo_ref[n_a:, :], preferred_element_type=F32))
    h = _rms(x1, nw_ref[...]).astype(BF16)
    out_ref[...] = x1
    for lo, hi in ff_chunks:
        a = jnp.dot(h, w1_ref[:, lo:hi], preferred_element_type=F32)
        g = jnp.dot(h, w3_ref[:, lo:hi], preferred_element_type=F32)
        out_ref[...] += jnp.dot((_silu(a) * g).astype(BF16), w2_ref[lo:hi, :], preferred_element_type=F32)


def _ff_chunks(d_ff, max_chunk=1024):
    assert d_ff % MXU_WIDTH == 0 and max_chunk % MXU_WIDTH == 0
    bounds = list(range(0, d_ff, max_chunk)) + [d_ff]
    return tuple(zip(bounds[:-1], bounds[1:]))


def _outproj_ffn(x2d, mix_a, mix_b, w_out, ffn_norm, w1, w3, w2, tm=512):
    n, d = x2d.shape
    tm = min(tm, n)
    d_ff = w1.shape[1]
    n_a, n_b = mix_a.shape[1], mix_b.shape[1]
    row = lambda w: pl.BlockSpec((tm, w), lambda i: (i, 0))
    hbm = pl.BlockSpec(memory_space=pl.ANY)
    return pl.pallas_call(
        functools.partial(_outproj_ffn_kernel, ff_chunks=_ff_chunks(d_ff)),
        grid=(n // tm,),
        in_specs=[row(d), row(n_a), row(n_b), pl.BlockSpec((1, d), lambda i: (0, 0)), hbm, hbm, hbm, hbm],
        out_specs=row(d),
        out_shape=jax.ShapeDtypeStruct((n, d), F32),
        scratch_shapes=[pltpu.VMEM(w_out.shape, BF16), pltpu.VMEM(w1.shape, BF16), pltpu.VMEM(w3.shape, BF16),
                        pltpu.VMEM(w2.shape, BF16), pltpu.SemaphoreType.DMA((4,))],
        compiler_params=_cparams(("arbitrary",)),
        name="outproj_ffn",
    )(x2d, mix_a, mix_b, ffn_norm.reshape(1, d), w_out, w1, w3, w2)


def _outproj_router_kernel(x_ref, m_ref, wo_ref, nw_ref, r_ref, x3_ref, hn_ref, idx_ref, gate_ref):
    x3 = x_ref[...] + jnp.dot(m_ref[...], wo_ref[...], preferred_element_type=F32)
    x3_ref[...] = x3
    hn = _rms(x3, nw_ref[...])
    hn_ref[...] = hn
    hn_hi = hn.astype(BF16)
    hn_lo = (hn - hn_hi.astype(F32)).astype(BF16)
    r_hi, r_lo = r_ref[:, :GATE_LANES], r_ref[:, GATE_LANES:]
    logits = (jnp.dot(hn_hi, r_hi, preferred_element_type=F32)
              + (jnp.dot(hn_hi, r_lo, preferred_element_type=F32)
                 + jnp.dot(hn_lo, r_hi, preferred_element_type=F32)))
    lane = lax.broadcasted_iota(jnp.int32, logits.shape, 1)
    logits = jnp.where(lane < N_EXPERTS, logits, -jnp.inf)
    m0 = jnp.max(logits, axis=-1, keepdims=True)
    i0 = jnp.min(jnp.where(logits == m0, lane, GATE_LANES), axis=-1, keepdims=True)
    rest = jnp.where(lane == i0, -jnp.inf, logits)
    m1 = jnp.max(rest, axis=-1, keepdims=True)
    i1 = jnp.min(jnp.where(rest == m1, lane, GATE_LANES), axis=-1, keepdims=True)
    e1 = jnp.exp(m1 - m0)
    inv = 1.0 / (1.0 + e1)
    idx_ref[...] = jnp.where(lane == 0, i0, jnp.where(lane == 1, i1, 0))
    gate_ref[...] = jnp.where(lane == 0, inv, jnp.where(lane == 1, e1 * inv, 0.0))


def _outproj_router(x2d, mix, w_out, ffn_norm, router, tm=1024):
    n, d = x2d.shape
    tm = min(tm, n)
    n_m = mix.shape[1]
    router_pad = jnp.zeros((d, GATE_LANES), F32).at[:, :N_EXPERTS].set(router.astype(F32))
    router_hi = router_pad.astype(BF16)
    router_lo = (router_pad - router_hi.astype(F32)).astype(BF16)
    router_split = jnp.concatenate([router_hi, router_lo], axis=1)
    row = lambda w: pl.BlockSpec((tm, w), lambda i: (i, 0))
    return pl.pallas_call(
        _outproj_router_kernel,
        grid=(n // tm,),
        in_specs=[row(d), row(n_m),
                  pl.BlockSpec((n_m, d), lambda i: (0, 0)),
                  pl.BlockSpec((1, d), lambda i: (0, 0)),
                  pl.BlockSpec((d, 2 * GATE_LANES), lambda i: (0, 0))],
        out_specs=[row(d), row(d), row(GATE_LANES), row(GATE_LANES)],
        out_shape=[jax.ShapeDtypeStruct((n, d), F32), jax.ShapeDtypeStruct((n, d), F32),
                   jax.ShapeDtypeStruct((n, GATE_LANES), jnp.int32),
                   jax.ShapeDtypeStruct((n, GATE_LANES), F32)],
        compiler_params=_cparams(("parallel",)),
        name="outproj_router",
    )(x2d, mix, w_out, ffn_norm.reshape(1, d), router_split)


def _rank_kernel(idx_ref, tri_ref, rank_ref, count_ref, carry_ref):
    @pl.when(pl.program_id(0) == 0)
    def _():
        carry_ref[...] = jnp.zeros_like(carry_ref)

    idx = idx_ref[...]
    lane = lax.broadcasted_iota(jnp.int32, idx.shape, 1)
    e0, e1 = idx[:, 0:1], idx[:, 1:2]
    member = ((lane == e0) | (lane == e1)).astype(F32)
    before = _dot(tri_ref[...], member) + carry_ref[...]
    r0 = jnp.sum(jnp.where(lane == e0, before, 0.0), axis=-1, keepdims=True)
    r1 = jnp.sum(jnp.where(lane == e1, before, 0.0), axis=-1, keepdims=True)
    packed = jnp.where(lane == 0, r0, jnp.where(lane == 1, r1, jnp.where(
        lane == 2, e0.astype(F32), jnp.where(lane == 3, e1.astype(F32), 0.0))))
    rank_ref[...] = jnp.transpose(packed)[0:8, :].astype(jnp.int32)
    carry_ref[...] += jnp.sum(member, axis=0, keepdims=True)
    count_ref[...] = carry_ref[...].astype(jnp.int32)


def _expert_ranks(ridx, tm=512):
    n = ridx.shape[0]
    tm = min(tm, n)
    i = np.arange(tm)
    tri = jnp.asarray((i[None, :] < i[:, None]).astype(np.float32)).astype(BF16)
    return pl.pallas_call(
        _rank_kernel,
        grid=(n // tm,),
        in_specs=[pl.BlockSpec((tm, GATE_LANES), lambda i: (i, 0)),
                  pl.BlockSpec((tm, tm), lambda i: (0, 0))],
        out_specs=[pl.BlockSpec((8, tm), lambda i: (0, i)),
                   pl.BlockSpec((1, GATE_LANES), lambda i: (0, 0))],
        out_shape=[jax.ShapeDtypeStruct((8, n), jnp.int32),
                   jax.ShapeDtypeStruct((1, GATE_LANES), jnp.int32)],
        scratch_shapes=[pltpu.VMEM((1, GATE_LANES), F32)],
        compiler_params=_cparams(("arbitrary",)),
        name="expert_ranks",
    )(ridx, tri)


def _row_copy(src_ref, src_row, dst_ref, dst_row, sem):
    return pltpu.make_async_copy(src_ref.at[pl.ds(src_row, 1)], dst_ref.at[pl.ds(dst_row, 1)], sem)


def _dispatch_kernel(dest_ref, fill_ref, hn_ref, slots_ref, buf_ref, zero_ref, load_sems, row_sems, fill_sem,
                     *, tm, n_tok, n_fill):
    i = pl.program_id(0)

    @pl.when(i == 0)
    def _():
        zero_ref[...] = jnp.zeros_like(zero_ref)
        for wait in (False, True):
            for s in range(n_fill):
                def fill(r, carry):
                    cp = _row_copy(zero_ref, 0, slots_ref, r, fill_sem)
                    cp.wait() if wait else cp.start()
                    return carry
                lax.fori_loop(fill_ref[2 * s], fill_ref[2 * s + 1], fill, 0)

    last = pl.num_programs(0) - 1
    base = i * tm
    cur = i % 3
    rsem = i % 2

    groups = tm // SUBLANES

    def load(tile, b):
        return pltpu.make_async_copy(hn_ref.at[pl.ds(tile * groups, groups)], buf_ref.at[b], load_sems.at[b])

    def drain(s):
        for _ in range(2):
            pltpu.make_async_copy(hn_ref.at[pl.ds(0, groups)], buf_ref.at[0], row_sems.at[s]).wait()

    @pl.when(i == 0)
    def _():
        load(0, 0).start()

    @pl.when(i < last)
    def _():
        load(i + 1, (i + 1) % 3).start()

    load(i, cur).wait()

    def start(g, carry):
        for u in range(SUBLANES):
            r = base + g * SUBLANES + u
            for k in range(2):
                pltpu.make_async_copy(buf_ref.at[cur, g, pl.ds(u, 1)], slots_ref.at[pl.ds(dest_ref[k * n_tok + r], 1)],
                                      row_sems.at[rsem]).start(priority=k)
        return carry

    lax.fori_loop(0, groups, start, 0, unroll=DMA_UNROLL // SUBLANES)

    @pl.when(i > 0)
    def _():
        drain(1 - rsem)

    @pl.when(i == last)
    def _():
        drain(rsem)


def _dispatch(dest_flat, fill_ranges, hn, n_slots, tm=512):
    n, d = hn.shape
    tm = min(tm, n)
    grid_spec = pltpu.PrefetchScalarGridSpec(
        num_scalar_prefetch=2,
        grid=(n // tm,),
        in_specs=[pl.BlockSpec(memory_space=pl.ANY)],
        out_specs=pl.BlockSpec(memory_space=pl.ANY),
        scratch_shapes=[pltpu.VMEM((3, tm // SUBLANES, SUBLANES, d), hn.dtype), pltpu.VMEM((SUBLANES, d), hn.dtype),
                        pltpu.SemaphoreType.DMA((3,)), pltpu.SemaphoreType.DMA((2,)),
                        pltpu.SemaphoreType.DMA(())],
    )
    return pl.pallas_call(
        functools.partial(_dispatch_kernel, tm=tm, n_tok=n, n_fill=fill_ranges.shape[0] // 2),
        grid_spec=grid_spec,
        out_shape=jax.ShapeDtypeStruct((n_slots, d), hn.dtype),
        compiler_params=_cparams(("arbitrary",)),
        name="row_dispatch",
    )(dest_flat, fill_ranges, hn.reshape(n // SUBLANES, SUBLANES, d))


def _moe_kernel(be_ref, nused_ref, x_ref, w1_hbm, w3_hbm, w2_hbm, out_ref, w1_ref, w3_ref, w2_ref,
                stage_cols_ref, stage_rows_ref, sems, *, ff_chunks, load_chunk):
    i = pl.program_id(0)
    used = i < nused_ref[0]
    expert = be_ref[i]
    prev_expert = be_ref[jnp.maximum(i - 1, 0)]
    d_ff = w1_ref.shape[1]

    @pl.when(used & ((i == 0) | (expert != prev_expert)))
    def _():
        pieces = []
        for lo in range(0, d_ff, load_chunk):
            cols = pl.ds(lo, load_chunk)
            pieces.append((w1_hbm.at[expert, :, cols], stage_cols_ref, w1_ref.at[:, cols]))
            pieces.append((w3_hbm.at[expert, :, cols], stage_cols_ref, w3_ref.at[:, cols]))
        for lo in range(0, d_ff, load_chunk):
            pieces.append((w2_hbm.at[expert, pl.ds(lo, load_chunk), :], stage_rows_ref,
                           w2_ref.at[pl.ds(lo, load_chunk), :]))
        copies = [pltpu.make_async_copy(src, ring.at[k % 2], sems.at[k % 2])
                  for k, (src, ring, _) in enumerate(pieces)]
        copies[0].start()
        for k, (_, ring, dst) in enumerate(pieces):
            if k + 1 < len(pieces):
                copies[k + 1].start()
            copies[k].wait()
            dst[...] = ring[k % 2].astype(BF16)

    @pl.when(used)
    def _():
        x = x_ref[...].astype(BF16)
        for n_done, (lo, hi) in enumerate(ff_chunks):
            a = jnp.dot(x, w1_ref[:, lo:hi], preferred_element_type=F32)
            g = jnp.dot(x, w3_ref[:, lo:hi], preferred_element_type=F32)
            y = jnp.dot((_silu(a) * g).astype(BF16), w2_ref[lo:hi, :], preferred_element_type=F32)
            if n_done == 0:
                out_ref[...] = y
            else:
                out_ref[...] += y

    @pl.when(jnp.logical_not(used))
    def _():
        out_ref[...] = jnp.zeros_like(out_ref)


def _moe_ffn(block_expert, n_used, slots, w1, w3, w2, load_chunk=512):
    n_slots, d = slots.shape
    n_blocks = n_slots // MOE_BLOCK
    d_ff = w1.shape[2]
    assert d_ff % load_chunk == 0
    hbm = pl.BlockSpec(memory_space=pl.ANY)
    grid_spec = pltpu.PrefetchScalarGridSpec(
        num_scalar_prefetch=2,
        grid=(n_blocks,),
        in_specs=[pl.BlockSpec((MOE_BLOCK, d), lambda i, be, nu: (jnp.minimum(i, nu[0] - 1), 0)), hbm, hbm, hbm],
        out_specs=pl.BlockSpec((MOE_BLOCK, d), lambda i, be, nu: (i, 0)),
        scratch_shapes=[pltpu.VMEM((d, d_ff), BF16), pltpu.VMEM((d, d_ff), BF16), pltpu.VMEM((d_ff, d), BF16),
                        pltpu.VMEM((2, d, load_chunk), w1.dtype), pltpu.VMEM((2, load_chunk, d), w2.dtype),
                        pltpu.SemaphoreType.DMA((2,))],
    )
    return pl.pallas_call(
        functools.partial(_moe_kernel, ff_chunks=_ff_chunks(d_ff), load_chunk=load_chunk),
        grid_spec=grid_spec,
        out_shape=jax.ShapeDtypeStruct((n_slots, d), F32),
        compiler_params=_cparams(("arbitrary",)),
        name="moe_ffn",
    )(block_expert, n_used, slots, w1, w3, w2)


def _combine_kernel(dest_ref, x_ref, gate_ref, nw_ref, y_ref, y_tiles_ref, out_ref, ybuf_ref, sems, *, tm, n_tok):
    i = pl.program_id(0)
    slot = i % 2
    d = x_ref.shape[1]

    def gather(tile, s):
        base = tile * tm

        def start(g, carry):
            for u in range(SUBLANES):
                r = base + g * SUBLANES + u
                for k in range(2):
                    pltpu.make_async_copy(y_ref.at[pl.ds(dest_ref[k * n_tok + r], 1)],
                                          ybuf_ref.at[s, k, g, pl.ds(u, 1)], sems.at[s]).start(priority=k)
            return carry

        lax.fori_loop(0, tm // SUBLANES, start, 0, unroll=DMA_UNROLL // SUBLANES)

    @pl.when(i == 0)
    def _():
        gather(0, 0)

    @pl.when(i + 1 < pl.num_programs(0))
    def _():
        gather(i + 1, 1 - slot)

    for k in range(2):
        pltpu.make_async_copy(y_tiles_ref.at[pl.ds(0, tm // SUBLANES)], ybuf_ref.at[slot, k], sems.at[slot]).wait()
    gate = gate_ref[...]
    y0 = ybuf_ref[slot, 0].reshape(tm, d)
    y1 = ybuf_ref[slot, 1].reshape(tm, d)
    out_ref[...] = _rms(x_ref[...] + (y0 * gate[:, 0:1] + y1 * gate[:, 1:2]), nw_ref[...])


def _combine(dest_flat, x2d, rgate, final_norm, yb, tm=512):
    n, d = x2d.shape
    tm = min(tm, n)
    grid_spec = pltpu.PrefetchScalarGridSpec(
        num_scalar_prefetch=1,
        grid=(n // tm,),
        in_specs=[pl.BlockSpec((tm, d), lambda i, dest: (i, 0)),
                  pl.BlockSpec((tm, GATE_LANES), lambda i, dest: (i, 0)),
                  pl.BlockSpec((1, d), lambda i, dest: (0, 0)),
                  pl.BlockSpec(memory_space=pl.ANY), pl.BlockSpec(memory_space=pl.ANY)],
        out_specs=pl.BlockSpec((tm, d), lambda i, dest: (i, 0)),
        scratch_shapes=[pltpu.VMEM((2, 2, tm // SUBLANES, SUBLANES, d), F32), pltpu.SemaphoreType.DMA((2,))],
    )
    return pl.pallas_call(
        functools.partial(_combine_kernel, tm=tm, n_tok=n),
        grid_spec=grid_spec,
        out_shape=jax.ShapeDtypeStruct((n, d), F32),
        compiler_params=_cparams(("arbitrary",)),
        name="combine_norm",
    )(dest_flat, x2d, rgate, final_norm.reshape(1, d), yb, yb.reshape(-1, SUBLANES, d))


def _split_in_proj(w_in, lo, hi):
    main = jnp.concatenate([w_in[:, :lo], w_in[:, hi:]], axis=1)
    gate = jnp.pad(w_in[:, lo:hi], ((0, 0), (0, GATE_LANES - (hi - lo))))
    return jnp.concatenate([main, gate], axis=1).astype(BF16), main.shape[1]


def kernel(x, mix_norm_0, w_in_0, conv_w_0, a_log_0, dt_bias_0, gdn_norm_0, ret_norm_0, w_out_0, ffn_norm_0,
           ffn_w1_0, ffn_w3_0, ffn_w2_0, mix_norm_1, w_in_1, gate_bias_1, mlstm_norm_1, w_out_1, ffn_norm_1,
           router_1, exp_w1_1, exp_w3_1, exp_w2_1, final_norm):
    b, t, d = x.shape
    n = b * t
    x2d = x.reshape(n, d)

    gdn_qkv = N_HEADS * (2 * HEAD_DK + GDN_DV)
    w0, n_main0 = _split_in_proj(w_in_0, gdn_qkv, gdn_qkv + 2 * N_HEADS)
    proj0, gates0 = _norm_proj(x2d, mix_norm_0, w0, n_main0, t, conv_w=conv_w_0)
    o_gdn = _gdn(proj0, gates0, _gates_transposed(gates0, b, t, CHUNK), a_log_0, dt_bias_0, gdn_norm_0, b, t)
    o_ret = _retention(proj0, ret_norm_0, b, t)
    x2 = _outproj_ffn(x2d, o_gdn.reshape(n, -1), o_ret.reshape(n, -1), w_out_0.astype(BF16), ffn_norm_0,
                      ffn_w1_0.astype(BF16), ffn_w3_0.astype(BF16), ffn_w2_0.astype(BF16))

    n_main1 = w_in_1.shape[1] - 2 * N_HEADS
    w1, _ = _split_in_proj(w_in_1, n_main1, n_main1 + 2 * N_HEADS)
    proj1, gates1 = _norm_proj(x2, mix_norm_1, w1, n_main1, t)
    h_mix = _mlstm(proj1, gates1, _gates_transposed(gates1, b, t, MLSTM_CHUNK), gate_bias_1, mlstm_norm_1, b, t)
    x3, hn, ridx, rgate = _outproj_router(x2, h_mix.reshape(n, -1), w_out_1.astype(BF16), ffn_norm_1, router_1)

    rank, counts = _expert_ranks(ridx)
    counts = counts[0, :N_EXPERTS]
    padded = (counts + MOE_BLOCK - 1) // MOE_BLOCK * MOE_BLOCK
    pad_end = jnp.cumsum(padded)
    pad_start = pad_end - padded
    pair_expert = rank[2:4].reshape(-1)
    pair_start = jnp.sum(jnp.where(pair_expert[:, None] == jnp.arange(N_EXPERTS, dtype=jnp.int32)[None, :],
                                   pad_start[None, :], 0), axis=1)
    dest_flat = pair_start + rank[0:2].reshape(-1)
    n_blocks = -(-(2 * n) // MOE_BLOCK) + N_EXPERTS
    block_row0 = jnp.arange(n_blocks, dtype=jnp.int32) * MOE_BLOCK
    block_expert = jnp.minimum(jnp.sum((pad_end[None, :] <= block_row0[:, None]).astype(jnp.int32), axis=1),
                               N_EXPERTS - 1)
    n_used = (pad_end[-1:] // MOE_BLOCK).astype(jnp.int32)

    n_slots = n_blocks * MOE_BLOCK
    fill_lo = jnp.concatenate([pad_start + counts, pad_end[-1:]])
    fill_hi = jnp.concatenate([pad_end, jnp.full((1,), n_slots, jnp.int32)])
    fill_ranges = jnp.stack([fill_lo, fill_hi], axis=1).reshape(-1).astype(jnp.int32)
    slots = _dispatch(dest_flat, fill_ranges, hn, n_slots)
    yb = _moe_ffn(block_expert, n_used, slots, exp_w1_1, exp_w3_1, exp_w2_1)
    out = _combine(dest_flat, x3, rgate, final_norm, yb)
    return out.reshape(b, t, d)
```

```python
import functools

import numpy as np
import jax
import jax.numpy as jnp
from jax import lax
from jax.experimental import pallas as pl
from jax.experimental.pallas import tpu as pltpu

F32 = jnp.float32
BF16 = jnp.bfloat16
HIGHEST = lax.Precision.HIGHEST

EPS = 1e-6
CHUNK = 128
RET_CHUNK = 128
MLSTM_CHUNK = 128
CONV_WIDTH = 4
N_HEADS = 8
HEAD_DK = 64
GDN_DV = 64
RET_DV = 64
MLSTM_DV = 128
ROPE_BASE = 10000.0
N_EXPERTS = 8
MOE_BLOCK = 512
GATE_LANES = 128
MXU_WIDTH = 256
VMEM_LIMIT = 52 * 1024 * 1024
SUBLANES = 8
DMA_UNROLL = 16


def _cparams(sem):
    return pltpu.CompilerParams(dimension_semantics=sem, vmem_limit_bytes=VMEM_LIMIT)


def _dot(a, b):
    return jnp.dot(a.astype(BF16), b.astype(BF16), preferred_element_type=F32)


def _dot_nt(a, b):
    return lax.dot_general(a.astype(BF16), b.astype(BF16), (((1,), (1,)), ((), ())),
                           preferred_element_type=F32)


def _dot_tn(a, b):
    return lax.dot_general(a.astype(BF16), b.astype(BF16), (((0,), (0,)), ((), ())),
                           preferred_element_type=F32)


def _dot_f32(a, b):
    return jnp.dot(a, b, precision=HIGHEST, preferred_element_type=F32)


def _rms(x, w):
    return x * lax.rsqrt(jnp.mean(x * x, axis=-1, keepdims=True) + EPS) * w


def _silu(x):
    return x * (1.0 / (1.0 + jnp.exp(-x)))


def _sigmoid(x):
    return 1.0 / (1.0 + jnp.exp(-x))


def _softplus(x):
    return jnp.maximum(x, 0.0) + jnp.log1p(jnp.exp(-jnp.abs(x)))


def _causal(n):
    r = lax.broadcasted_iota(jnp.int32, (n, n), 0)
    c = lax.broadcasted_iota(jnp.int32, (n, n), 1)
    return r >= c


def _norm_proj_kernel(x_ref, nw_ref, w_ref, conv_ref, main_ref, gate_ref, cbuf_ref, *, n_main, col_chunk,
                      conv_cols, tiles_per_seq):
    tm = x_ref.shape[0]
    h = _rms(x_ref[...], nw_ref[...]).astype(BF16)
    if conv_cols:
        @pl.when(pl.program_id(0) % tiles_per_seq == 0)
        def _():
            cbuf_ref[0:SUBLANES, :] = jnp.zeros((SUBLANES, conv_cols), F32)
    for c0 in range(0, n_main, col_chunk):
        cols = slice(c0, c0 + col_chunk)
        y = jnp.dot(h, w_ref[:, cols], preferred_element_type=F32)
        if c0 < conv_cols:
            cbuf_ref[SUBLANES:SUBLANES + tm, cols] = y
            pre = None
            for j in range(CONV_WIDTH):
                term = cbuf_ref[pl.ds(SUBLANES - (CONV_WIDTH - 1) + j, tm), cols] * conv_ref[j:j + 1, cols]
                pre = term if pre is None else pre + term
            cbuf_ref[0:SUBLANES, cols] = cbuf_ref[tm:tm + SUBLANES, cols]
            y = _silu(pre)
        main_ref[:, cols] = y.astype(BF16)
    gate_ref[...] = jnp.dot(h, w_ref[:, n_main:], preferred_element_type=F32)


def _norm_proj(x2d, norm_w, w_cat, n_main, seq_len, conv_w=None, tm=512, col_chunk=256):
    n, d = x2d.shape
    tm = min(tm, n, seq_len)
    n_cat = w_cat.shape[1]
    conv_cols = 0 if conv_w is None else conv_w.shape[1]
    assert seq_len % tm == 0 and conv_cols % col_chunk == 0
    if conv_w is None:
        conv_w = jnp.zeros((CONV_WIDTH, GATE_LANES), F32)
    return pl.pallas_call(
        functools.partial(_norm_proj_kernel, n_main=n_main, col_chunk=col_chunk, conv_cols=conv_cols,
                          tiles_per_seq=seq_len // tm),
        grid=(n // tm,),
        in_specs=[pl.BlockSpec((tm, d), lambda i: (i, 0)),
                  pl.BlockSpec((1, d), lambda i: (0, 0)),
                  pl.BlockSpec((d, n_cat), lambda i: (0, 0)),
                  pl.BlockSpec(conv_w.shape, lambda i: (0, 0))],
        out_specs=[pl.BlockSpec((tm, n_main), lambda i: (i, 0)),
                   pl.BlockSpec((tm, GATE_LANES), lambda i: (i, 0))],
        out_shape=[jax.ShapeDtypeStruct((n, n_main), BF16),
                   jax.ShapeDtypeStruct((n, GATE_LANES), F32)],
        scratch_shapes=[pltpu.VMEM((tm + SUBLANES, max(conv_cols, GATE_LANES)), F32)],
        compiler_params=_cparams(("arbitrary",)),
        name="norm_proj",
    )(x2d, norm_w.reshape(1, d), w_cat, conv_w.astype(F32))


def _gdn_kernel(qkv_ref, z_ref, gate_ref, gate_t_ref, arow_ref, dtrow_ref, acol_ref, dtcol_ref,
                norm_ref, ltri_ref, utri_ref, lvl_ref, ones_ref, out_ref, state_ref):
    n_b = qkv_ref.shape[0]
    n_qk = N_HEADS * HEAD_DK

    @pl.when(pl.program_id(1) == 0)
    def _():
        state_ref[...] = jnp.zeros_like(state_ref)

    causal = _causal(CHUNK)
    eye = (lax.broadcasted_iota(jnp.int32, (CHUNK, CHUNK), 0)
           == lax.broadcasted_iota(jnp.int32, (CHUNK, CHUNK), 1)).astype(F32)
    sl = [slice(h * HEAD_DK, (h + 1) * HEAD_DK) for h in range(N_HEADS)]

    q, k, v, bc, gcc, gcr = [], [], [], [], [], []
    for i in range(n_b):
        act = qkv_ref[i].astype(F32)
        q_all, k_all, v_all = act[:, :n_qk], act[:, n_qk:2 * n_qk], act[:, 2 * n_qk:]
        q_all = q_all * lax.rsqrt(_dot(q_all * q_all, ones_ref[...]) + EPS) * (HEAD_DK ** -0.5)
        k_all = k_all * lax.rsqrt(_dot(k_all * k_all, ones_ref[...]) + EPS)
        gl = gate_ref[i]
        beta_c = _sigmoid(gl)
        g_c = -jnp.exp(arow_ref[...]) * _softplus(gl + dtrow_ref[...])
        gc_c = _dot_f32(ltri_ref[...], g_c)
        g_r = -jnp.exp(acol_ref[...]) * _softplus(gate_t_ref[i] + dtcol_ref[...])
        gc_r = _dot_f32(g_r, utri_ref[...])
        for h in range(N_HEADS):
            q.append(q_all[:, sl[h]])
            k.append(k_all[:, sl[h]])
            v.append(v_all[:, sl[h]])
            bc.append(beta_c[:, h:h + 1])
            gcc.append(gc_c[:, 8 + h:9 + h])
            gcr.append(gc_r[8 + h:9 + h, :])

    chains = range(n_b * N_HEADS)
    decay = [jnp.exp(jnp.where(causal, gcc[c] - gcr[c], -jnp.inf)) for c in chains]
    kb = [k[c] * bc[c] for c in chains]
    scores = [_dot_nt(jnp.concatenate([kb[c], q[c]], axis=0), k[c]) for c in chains]
    lmat = [scores[c][:CHUNK] * decay[c] for c in chains]
    qk = [scores[c][CHUNK:] * decay[c] for c in chains]
    tinv = [eye - lmat[c] * lvl_ref[0].astype(F32) for c in chains]
    lmat_mxu = [lmat[c].astype(BF16) for c in chains]
    for lvl in range(1, lvl_ref.shape[0]):
        inner = [_dot(lmat_mxu[c] * lvl_ref[lvl], tinv[c]) for c in chains]
        tinv = [tinv[c] - _dot(tinv[c], inner[c]) for c in chains]
    egc = [jnp.exp(gcc[c]) for c in chains]
    uw = [_dot(tinv[c], jnp.concatenate([v[c] * bc[c], kb[c] * egc[c]], axis=1)) for c in chains]
    state = [state_ref[c] for c in chains]
    v_new = [uw[c][:, :GDN_DV] - _dot(uw[c][:, GDN_DV:], state[c]) for c in chains]
    o = [_dot(q[c] * egc[c], state[c]) + _dot(qk[c], v_new[c]) for c in chains]
    g_last = [gcc[c][CHUNK - 1:CHUNK, :] for c in chains]
    k_dec = [k[c] * jnp.exp(g_last[c] - gcc[c]) for c in chains]
    for c in chains:
        state_ref[c] = state[c] * jnp.exp(g_last[c]) + _dot_tn(k_dec[c], v_new[c])
    for i in range(n_b):
        o_all = jnp.concatenate(o[i * N_HEADS:(i + 1) * N_HEADS], axis=1)
        mean_sq = _dot(o_all * o_all, ones_ref[...]) * (1.0 / GDN_DV)
        out_ref[i] = (o_all * lax.rsqrt(mean_sq + EPS) * norm_ref[...]
                      * _silu(z_ref[i].astype(F32))).astype(out_ref.dtype)


def _level_masks():
    i = np.arange(CHUNK)[:, None]
    j = np.arange(CHUNK)[None, :]
    masks = []
    for lvl in range(CHUNK.bit_length() - 1):
        same_parent = (i >> (lvl + 1)) == (j >> (lvl + 1))
        masks.append((same_parent & (((i >> lvl) & 1) == 1) & (((j >> lvl) & 1) == 0)).astype(np.float32))
    return jnp.asarray(np.stack(masks)).astype(BF16)


def _tri_consts(ck):
    i = np.arange(ck)[:, None]
    j = np.arange(ck)[None, :]
    return jnp.asarray((j <= i).astype(np.float32)), jnp.asarray((i <= j).astype(np.float32))


def _lane_row(vals, offset):
    return jnp.zeros((1, GATE_LANES), F32).at[0, offset:offset + vals.shape[0]].set(vals.astype(F32))


def _sub_col(vals, offset):
    return jnp.zeros((16, 1), F32).at[offset:offset + vals.shape[0], 0].set(vals.astype(F32))


def _head_ones(n_heads, width):
    g = np.arange(n_heads * width) // width
    return jnp.asarray((g[:, None] == g[None, :]).astype(np.float32)).astype(BF16)


def _batch_per_step(b, want=2):
    return max(k for k in range(1, want + 1) if b % k == 0)


def _gdn(proj, gates, gates_t, a_log, dt_bias, gdn_norm, b, t):
    nc = t // CHUNK
    nb = _batch_per_step(b, want=2)
    n_qkv = N_HEADS * (2 * HEAD_DK + GDN_DV)
    n_z = N_HEADS * GDN_DV
    proj3 = proj.reshape(b, t, proj.shape[-1])
    ltri, utri = _tri_consts(CHUNK)
    const2 = lambda shape: pl.BlockSpec(shape, lambda i, c: (0,) * len(shape))
    return pl.pallas_call(
        _gdn_kernel,
        grid=(b // nb, nc),
        in_specs=[pl.BlockSpec((nb, CHUNK, n_qkv), lambda i, c: (i, c, 0)),
                  pl.BlockSpec((nb, CHUNK, n_z), lambda i, c: (i, c, n_qkv // n_z)),
                  pl.BlockSpec((nb, CHUNK, GATE_LANES), lambda i, c: (i, c, 0)),
                  pl.BlockSpec((nb, None, 16, CHUNK), lambda i, c: (i, c, 0, 0)),
                  const2((1, GATE_LANES)), const2((1, GATE_LANES)),
                  const2((16, 1)), const2((16, 1)), const2((1, n_z)),
                  const2((CHUNK, CHUNK)), const2((CHUNK, CHUNK)),
                  const2((CHUNK.bit_length() - 1, CHUNK, CHUNK)),
                  const2((n_z, n_z))],
        out_specs=pl.BlockSpec((nb, CHUNK, n_z), lambda i, c: (i, c, 0)),
        out_shape=jax.ShapeDtypeStruct((b, t, n_z), BF16),
        scratch_shapes=[pltpu.VMEM((nb * N_HEADS, HEAD_DK, GDN_DV), F32)],
        compiler_params=_cparams(("parallel", "arbitrary")),
        name="gdn_mixer",
    )(proj3, proj3, gates.reshape(b, t, GATE_LANES), gates_t,
      _lane_row(a_log, 8), _lane_row(dt_bias, 8), _sub_col(a_log, 8), _sub_col(dt_bias, 8),
      jnp.tile(gdn_norm.astype(F32), N_HEADS).reshape(1, n_z), ltri, utri, _level_masks(),
      _head_ones(N_HEADS, HEAD_DK))


def _ret_kernel(q_ref, k_ref, v_ref, g_ref, cos_ref, sin_ref, decay_ref, qs_ref, ks_ref, cd_ref, norm_ref,
                ones_ref, out_ref, state_ref):
    @pl.when(pl.program_id(1) == 0)
    def _():
        state_ref[...] = jnp.zeros_like(state_ref)

    width = N_HEADS * HEAD_DK
    half = HEAD_DK // 2
    first_half = (lax.broadcasted_iota(jnp.int32, (q_ref.shape[1], width), 1) % HEAD_DK) < half

    def rot(t_in):
        swapped = jnp.where(first_half, pltpu.roll(t_in, width - half, 1), pltpu.roll(t_in, half, 1))
        return t_in * cos_ref[...] + swapped * sin_ref[...]

    n_b = q_ref.shape[0]
    sl = [slice(h * HEAD_DK, (h + 1) * HEAD_DK) for h in range(N_HEADS)]
    q, k, v, q_in, k_st = [], [], [], [], []
    for i in range(n_b):
        q_all = rot(q_ref[i].astype(F32))
        k_all = rot(k_ref[i].astype(F32)) * (HEAD_DK ** -0.5)
        v_all = v_ref[i].astype(F32)
        qi_all = q_all * qs_ref[...]
        ks_all = k_all * ks_ref[...]
        for h in range(N_HEADS):
            q.append(q_all[:, sl[h]])
            k.append(k_all[:, sl[h]])
            v.append(v_all[:, sl[h]])
            q_in.append(qi_all[:, sl[h]])
            k_st.append(ks_all[:, sl[h]])
    chains = range(n_b * N_HEADS)
    state = [state_ref[c] for c in chains]
    scores = [_dot_nt(q[c], k[c]) * decay_ref[c % N_HEADS] for c in chains]
    inter = [_dot(q_in[c], state[c]) for c in chains]
    o = [_dot(scores[c], v[c]) + inter[c] for c in chains]
    for c in chains:
        state_ref[c] = state[c] * cd_ref[c % N_HEADS] + _dot_tn(k_st[c], v[c])
    inv_dv = 1.0 / RET_DV
    for i in range(n_b):
        o_all = jnp.concatenate(o[i * N_HEADS:(i + 1) * N_HEADS], axis=1)
        xc = o_all - _dot(o_all, ones_ref[...]) * inv_dv
        var = _dot(xc * xc, ones_ref[...]) * inv_dv
        out_ref[i] = (xc * lax.rsqrt(var + EPS) * norm_ref[...]
                      * _silu(g_ref[i].astype(F32))).astype(out_ref.dtype)


def _retention(proj, ret_norm, b, t):
    ck = RET_CHUNK
    nc = t // ck
    width = N_HEADS * HEAD_DK
    half = HEAD_DK // 2
    proj3 = proj.reshape(b, t, proj.shape[-1])
    pos = jnp.arange(t, dtype=F32)
    inv_freq = jnp.power(ROPE_BASE, -jnp.arange(half, dtype=F32) / half)
    ang = pos[:, None] * inv_freq[None, :]
    cos, sin = jnp.cos(ang), jnp.sin(ang)
    cos_t = jnp.tile(jnp.concatenate([cos, cos], axis=1), (1, N_HEADS))
    sin_t = jnp.tile(jnp.concatenate([-sin, sin], axis=1), (1, N_HEADS))
    log_gamma = jnp.log1p(-jnp.exp2(-5.0 - jnp.arange(N_HEADS, dtype=F32)))
    idx = jnp.arange(ck, dtype=F32)
    causal = jnp.tril(jnp.ones((ck, ck), bool))
    decay = jnp.exp(jnp.where(causal, (idx[:, None] - idx[None, :]) * log_gamma[:, None, None], -jnp.inf))
    qs = jnp.repeat(jnp.exp((idx + 1.0) * log_gamma[:, None]).T, HEAD_DK, axis=1)
    ks = jnp.repeat(jnp.exp((ck - 1.0 - idx) * log_gamma[:, None]).T, HEAD_DK, axis=1)
    cd = jnp.broadcast_to(jnp.exp(ck * log_gamma)[:, None, None], (N_HEADS, 1, RET_DV))
    base = (N_HEADS * (2 * HEAD_DK + GDN_DV) + N_HEADS * GDN_DV) // width
    nb = _batch_per_step(b, want=8)
    col = lambda k: pl.BlockSpec((nb, ck, width), lambda i, c: (i, c, base + k))
    const2 = lambda shape: pl.BlockSpec(shape, lambda i, c: (0,) * len(shape))
    return pl.pallas_call(
        _ret_kernel,
        grid=(b // nb, nc),
        in_specs=[col(0), col(1), col(2), col(3),
                  pl.BlockSpec((ck, width), lambda i, c: (c, 0)),
                  pl.BlockSpec((ck, width), lambda i, c: (c, 0)),
                  const2((N_HEADS, ck, ck)), const2((ck, width)), const2((ck, width)),
                  const2((N_HEADS, 1, RET_DV)), const2((1, width)), const2((width, width))],
        out_specs=pl.BlockSpec((nb, ck, width), lambda i, c: (i, c, 0)),
        out_shape=jax.ShapeDtypeStruct((b, t, width), BF16),
        scratch_shapes=[pltpu.VMEM((nb * N_HEADS, HEAD_DK, RET_DV), F32)],
        compiler_params=_cparams(("parallel", "arbitrary")),
        name="retention_mixer",
    )(proj3, proj3, proj3, proj3, cos_t, sin_t, decay, qs, ks, cd, ret_norm.reshape(1, width).astype(F32),
      _head_ones(N_HEADS, RET_DV))


def _mlstm_kernel(q_ref, k_ref, v_ref, o_ref, gate_ref, gate_t_ref, brow_ref, bcol_ref, norm_ref,
                  ltri_ref, utri_ref, out_ref, state_ref, m_ref):
    @pl.when(pl.program_id(1) == 0)
    def _():
        state_ref[...] = jnp.zeros_like(state_ref)
        m_ref[...] = jnp.zeros_like(m_ref)

    n_b, ck = q_ref.shape[0], q_ref.shape[1]
    causal = _causal(ck)
    ones_col = (lax.broadcasted_iota(jnp.int32, (ck, MLSTM_DV), 1) == 0).astype(F32)
    sl = [slice(h * HEAD_DK, (h + 1) * HEAD_DK) for h in range(N_HEADS)]
    sv = [slice(h * MLSTM_DV, (h + 1) * MLSTM_DV) for h in range(N_HEADS)]

    q, k, vx, bc, br, ig_c, ig_r = [], [], [], [], [], [], []
    for i in range(n_b):
        q_all = q_ref[i].astype(F32)
        k_all = k_ref[i].astype(F32) * (HEAD_DK ** -0.5)
        v_all = v_ref[i].astype(F32)
        gl = gate_ref[i] + brow_ref[...]
        bcum_c = _dot_f32(ltri_ref[...], -_softplus(-gl))
        gt = gate_t_ref[i] + bcol_ref[...]
        bcum_r = _dot_f32(-_softplus(-gt), utri_ref[...])
        for h in range(N_HEADS):
            q.append(q_all[:, sl[h]])
            k.append(k_all[:, sl[h]])
            vx.append(jnp.concatenate([v_all[:, sv[h]], ones_col], axis=1))
            bc.append(bcum_c[:, 8 + h:9 + h])
            br.append(bcum_r[8 + h:9 + h, :])
            ig_c.append(gl[:, h:h + 1])
            ig_r.append(gt[h:h + 1, :])

    chains = range(n_b * N_HEADS)
    log_w = [jnp.where(causal, bc[c] - br[c] + ig_r[c], -jnp.inf) for c in chains]
    m_intra = [jnp.max(log_w[c], axis=-1, keepdims=True) for c in chains]
    b_end = [bc[c][ck - 1:ck, :] for c in chains]
    lw_end = [b_end[c] - bc[c] + ig_c[c] for c in chains]
    m_end = [jnp.max(lw_end[c], axis=0, keepdims=True) for c in chains]
    m_s = [m_ref[c][:, 0:1] for c in chains]
    cx = [state_ref[c] for c in chains]
    m_t = [jnp.maximum(bc[c] + m_s[c], m_intra[c]) for c in chains]
    inter = [jnp.exp(bc[c] + m_s[c] - m_t[c]) for c in chains]
    qk = [_dot_nt(q[c], k[c]) for c in chains]
    qc = [_dot(q[c], cx[c]) for c in chains]
    s = [qk[c] * jnp.exp(log_w[c] - m_t[c]) for c in chains]
    numx = [inter[c] * qc[c] + _dot(s[c], vx[c]) for c in chains]
    m_new = [jnp.maximum(b_end[c] + m_s[c], m_end[c]) for c in chains]
    wk = [k[c] * jnp.exp(lw_end[c] - m_new[c]) for c in chains]
    for c in chains:
        state_ref[c] = jnp.exp(b_end[c] + m_s[c] - m_new[c]) * cx[c] + _dot_tn(wk[c], vx[c])
        m_ref[c] = jnp.broadcast_to(m_new[c], (1, GATE_LANES))
    hh = [numx[c][:, :MLSTM_DV] / jnp.maximum(jnp.abs(numx[c][:, MLSTM_DV:MLSTM_DV + 1]), jnp.exp(-m_t[c]))
          for c in chains]
    outs = [_rms(hh[c], norm_ref[:, sv[c % N_HEADS]]) for c in chains]
    for i in range(n_b):
        out_ref[i] = (jnp.concatenate(outs[i * N_HEADS:(i + 1) * N_HEADS], axis=1)
                      * _sigmoid(o_ref[i].astype(F32))).astype(out_ref.dtype)


def _mlstm(proj, gates, gates_t, gate_bias, mlstm_norm, b, t):
    ck = MLSTM_CHUNK
    nc = t // ck
    n_qk = N_HEADS * HEAD_DK
    n_v = N_HEADS * MLSTM_DV
    proj3 = proj.reshape(b, t, proj.shape[-1])
    ltri, utri = _tri_consts(ck)
    nb = _batch_per_step(b, want=4)
    const2 = lambda shape: pl.BlockSpec(shape, lambda i, c: (0,) * len(shape))
    return pl.pallas_call(
        _mlstm_kernel,
        grid=(b // nb, nc),
        in_specs=[pl.BlockSpec((nb, ck, n_qk), lambda i, c: (i, c, 0)),
                  pl.BlockSpec((nb, ck, n_qk), lambda i, c: (i, c, 1)),
                  pl.BlockSpec((nb, ck, n_v), lambda i, c: (i, c, 2 * n_qk // n_v)),
                  pl.BlockSpec((nb, ck, n_v), lambda i, c: (i, c, 2 * n_qk // n_v + 1)),
                  pl.BlockSpec((nb, ck, GATE_LANES), lambda i, c: (i, c, 0)),
                  pl.BlockSpec((nb, None, 16, ck), lambda i, c: (i, c, 0, 0)),
                  const2((1, GATE_LANES)), const2((16, 1)), const2((1, n_v)),
                  const2((ck, ck)), const2((ck, ck))],
        out_specs=pl.BlockSpec((nb, ck, n_v), lambda i, c: (i, c, 0)),
        out_shape=jax.ShapeDtypeStruct((b, t, n_v), BF16),
        scratch_shapes=[pltpu.VMEM((nb * N_HEADS, HEAD_DK, 2 * MLSTM_DV), F32),
                        pltpu.VMEM((nb * N_HEADS, 1, GATE_LANES), F32)],
        compiler_params=_cparams(("parallel", "arbitrary")),
        name="mlstm_mixer",
    )(proj3, proj3, proj3, proj3, gates.reshape(b, t, GATE_LANES), gates_t,
      _lane_row(gate_bias, 0), _sub_col(gate_bias, 0), mlstm_norm.reshape(1, n_v).astype(F32), ltri, utri)


def _gates_transposed(gates, b, t, ck):
    return jnp.transpose(gates[:, :16].reshape(b, t // ck, ck, 16), (0, 1, 3, 2))


def _outproj_ffn_kernel(x_ref, ma_ref, mb_ref, nw_ref, wo_hbm, w1_hbm, w3_hbm, w2_hbm, out_ref,
                        wo_ref, w1_ref, w3_ref, w2_ref, sems, *, ff_chunks):
    @pl.when(pl.program_id(0) == 0)
    def _():
        copies = [pltpu.make_async_copy(src, dst, sems.at[k]) for k, (src, dst) in enumerate(
            ((wo_hbm, wo_ref), (w1_hbm, w1_ref), (w3_hbm, w3_ref), (w2_hbm, w2_ref)))]
        for cp in copies:
            cp.start()
        for cp in copies:
            cp.wait()

    n_a = ma_ref.shape[1]
    x1 = (x_ref[...] + jnp.dot(ma_ref[...], wo_ref[:n_a, :], preferred_element_type=F32)
          + jnp.dot(mb_ref[...], wo_ref[n_a:, :], preferred_element_type=F32))
    h = _rms(x1, nw_ref[...]).astype(BF16)
    out_ref[...] = x1
    for lo, hi in ff_chunks:
        a = jnp.dot(h, w1_ref[:, lo:hi], preferred_element_type=F32)
        g = jnp.dot(h, w3_ref[:, lo:hi], preferred_element_type=F32)
        out_ref[...] += jnp.dot((_silu(a) * g).astype(BF16), w2_ref[lo:hi, :], preferred_element_type=F32)


def _ff_chunks(d_ff, max_chunk=1024):
    assert d_ff % MXU_WIDTH == 0 and max_chunk % MXU_WIDTH == 0
    bounds = list(range(0, d_ff, max_chunk)) + [d_ff]
    return tuple(zip(bounds[:-1], bounds[1:]))


def _outproj_ffn(x2d, mix_a, mix_b, w_out, ffn_norm, w1, w3, w2, tm=512):
    n, d = x2d.shape
    tm = min(tm, n)
    d_ff = w1.shape[1]
    n_a, n_b = mix_a.shape[1], mix_b.shape[1]
    row = lambda w: pl.BlockSpec((tm, w), lambda i: (i, 0))
    hbm = pl.BlockSpec(memory_space=pl.ANY)
    return pl.pallas_call(
        functools.partial(_outproj_ffn_kernel, ff_chunks=_ff_chunks(d_ff)),
        grid=(n // tm,),
        in_specs=[row(d), row(n_a), row(n_b), pl.BlockSpec((1, d), lambda i: (0, 0)), hbm, hbm, hbm, hbm],
        out_specs=row(d),
        out_shape=jax.ShapeDtypeStruct((n, d), F32),
        scratch_shapes=[pltpu.VMEM(w_out.shape, BF16), pltpu.VMEM(w1.shape, BF16), pltpu.VMEM(w3.shape, BF16),
                        pltpu.VMEM(w2.shape, BF16), pltpu.SemaphoreType.DMA((4,))],
        compiler_params=_cparams(("arbitrary",)),
        name="outproj_ffn",
    )(x2d, mix_a, mix_b, ffn_norm.reshape(1, d), w_out, w1, w3, w2)


def _outproj_router_kernel(x_ref, m_ref, wo_ref, nw_ref, r_ref, x3_ref, hn_ref, idx_ref, gate_ref):
    x3 = x_ref[...] + jnp.dot(m_ref[...], wo_ref[...], preferred_element_type=F32)
    x3_ref[...] = x3
    hn = _rms(x3, nw_ref[...])
    hn_ref[...] = hn
    hn_hi = hn.astype(BF16)
    hn_lo = (hn - hn_hi.astype(F32)).astype(BF16)
    r_hi, r_lo = r_ref[:, :GATE_LANES], r_ref[:, GATE_LANES:]
    logits = (jnp.dot(hn_hi, r_hi, preferred_element_type=F32)
              + (jnp.dot(hn_hi, r_lo, preferred_element_type=F32)
                 + jnp.dot(hn_lo, r_hi, preferred_element_type=F32)))
    lane = lax.broadcasted_iota(jnp.int32, logits.shape, 1)
    logits = jnp.where(lane < N_EXPERTS, logits, -jnp.inf)
    m0 = jnp.max(logits, axis=-1, keepdims=True)
    i0 = jnp.min(jnp.where(logits == m0, lane, GATE_LANES), axis=-1, keepdims=True)
    rest = jnp.where(lane == i0, -jnp.inf, logits)
    m1 = jnp.max(rest, axis=-1, keepdims=True)
    i1 = jnp.min(jnp.where(rest == m1, lane, GATE_LANES), axis=-1, keepdims=True)
    e1 = jnp.exp(m1 - m0)
    inv = 1.0 / (1.0 + e1)
    idx_ref[...] = jnp.where(lane == 0, i0, jnp.where(lane == 1, i1, 0))
    gate_ref[...] = jnp.where(lane == 0, inv, jnp.where(lane == 1, e1 * inv, 0.0))


def _outproj_router(x2d, mix, w_out, ffn_norm, router, tm=1024):
    n, d = x2d.shape
    tm = min(tm, n)
    n_m = mix.shape[1]
    router_pad = jnp.zeros((d, GATE_LANES), F32).at[:, :N_EXPERTS].set(router.astype(F32))
    router_hi = router_pad.astype(BF16)
    router_lo = (router_pad - router_hi.astype(F32)).astype(BF16)
    router_split = jnp.concatenate([router_hi, router_lo], axis=1)
    row = lambda w: pl.BlockSpec((tm, w), lambda i: (i, 0))
    return pl.pallas_call(
        _outproj_router_kernel,
        grid=(n // tm,),
        in_specs=[row(d), row(n_m),
                  pl.BlockSpec((n_m, d), lambda i: (0, 0)),
                  pl.BlockSpec((1, d), lambda i: (0, 0)),
                  pl.BlockSpec((d, 2 * GATE_LANES), lambda i: (0, 0))],
        out_specs=[row(d), row(d), row(GATE_LANES), row(GATE_LANES)],
        out_shape=[jax.ShapeDtypeStruct((n, d), F32), jax.ShapeDtypeStruct((n, d), F32),
                   jax.ShapeDtypeStruct((n, GATE_LANES), jnp.int32),
                   jax.ShapeDtypeStruct((n, GATE_LANES), F32)],
        compiler_params=_cparams(("parallel",)),
        name="outproj_router",
    )(x2d, mix, w_out, ffn_norm.reshape(1, d), router_split)


def _rank_kernel(idx_ref, tri_ref, rank_ref, count_ref, carry_ref):
    @pl.when(pl.program_id(0) == 0)
    def _():
        carry_ref[...] = jnp.zeros_like(carry_ref)

    idx = idx_ref[...]
    lane = lax.broadcasted_iota(jnp.int32, idx.shape, 1)
    e0, e1 = idx[:, 0:1], idx[:, 1:2]
    member = ((lane == e0) | (lane == e1)).astype(F32)
    before = _dot(tri_ref[...], member) + carry_ref[...]
    r0 = jnp.sum(jnp.where(lane == e0, before, 0.0), axis=-1, keepdims=True)
    r1 = jnp.sum(jnp.where(lane == e1, before, 0.0), axis=-1, keepdims=True)
    packed = jnp.where(lane == 0, r0, jnp.where(lane == 1, r1, jnp.where(
        lane == 2, e0.astype(F32), jnp.where(lane == 3, e1.astype(F32), 0.0))))
    rank_ref[...] = jnp.transpose(packed)[0:8, :].astype(jnp.int32)
    carry_ref[...] += jnp.sum(member, axis=0, keepdims=True)
    count_ref[...] = carry_ref[...].astype(jnp.int32)


def _expert_ranks(ridx, tm=512):
    n = ridx.shape[0]
    tm = min(tm, n)
    i = np.arange(tm)
    tri = jnp.asarray((i[None, :] < i[:, None]).astype(np.float32)).astype(BF16)
    return pl.pallas_call(
        _rank_kernel,
        grid=(n // tm,),
        in_specs=[pl.BlockSpec((tm, GATE_LANES), lambda i: (i, 0)),
                  pl.BlockSpec((tm, tm), lambda i: (0, 0))],
        out_specs=[pl.BlockSpec((8, tm), lambda i: (0, i)),
                   pl.BlockSpec((1, GATE_LANES), lambda i: (0, 0))],
        out_shape=[jax.ShapeDtypeStruct((8, n), jnp.int32),
                   jax.ShapeDtypeStruct((1, GATE_LANES), jnp.int32)],
        scratch_shapes=[pltpu.VMEM((1, GATE_LANES), F32)],
        compiler_params=_cparams(("arbitrary",)),
        name="expert_ranks",
    )(ridx, tri)


def _row_copy(src_ref, src_row, dst_ref, dst_row, sem):
    return pltpu.make_async_copy(src_ref.at[pl.ds(src_row, 1)], dst_ref.at[pl.ds(dst_row, 1)], sem)


def _dispatch_kernel(dest_ref, fill_ref, hn_ref, slots_ref, buf_ref, zero_ref, load_sems, row_sems, fill_sem,
                     *, tm, n_tok, n_fill, n_unrouted):
    i = pl.program_id(0)

    @pl.when(i == 0)
    def _():
        zero_ref[...] = jnp.zeros_like(zero_ref)
        for s in range(n_fill - 1):
            def fill_row(r, carry):
                _row_copy(zero_ref, 0, slots_ref, r, fill_sem).start()
                return carry
            lax.fori_loop(fill_ref[2 * s], fill_ref[2 * s + 1], fill_row, 0)

        def fill_block(blk, carry):
            pltpu.make_async_copy(zero_ref, slots_ref.at[pl.ds(blk * MOE_BLOCK, MOE_BLOCK)], fill_sem).start()
            return carry
        lax.fori_loop(fill_ref[2 * n_fill - 2] // MOE_BLOCK, fill_ref[2 * n_fill - 1] // MOE_BLOCK, fill_block, 0)
        for _ in range(n_unrouted // MOE_BLOCK):
            pltpu.make_async_copy(zero_ref, slots_ref.at[pl.ds(0, MOE_BLOCK)], fill_sem).wait()

    last = pl.num_programs(0) - 1
    base = i * tm
    cur = i % 3
    rsem = i % 2

    groups = tm // SUBLANES

    def load(tile, b):
        return pltpu.make_async_copy(hn_ref.at[pl.ds(tile * groups, groups)], buf_ref.at[b], load_sems.at[b])

    def drain(s):
        for _ in range(2):
            pltpu.make_async_copy(hn_ref.at[pl.ds(0, groups)], buf_ref.at[0], row_sems.at[s]).wait()

    @pl.when(i == 0)
    def _():
        load(0, 0).start()

    @pl.when(i < last)
    def _():
        load(i + 1, (i + 1) % 3).start()

    load(i, cur).wait()

    def start(g, carry):
        for u in range(SUBLANES):
            r = base + g * SUBLANES + u
            for k in range(2):
                pltpu.make_async_copy(buf_ref.at[cur, g, pl.ds(u, 1)], slots_ref.at[pl.ds(dest_ref[k * n_tok + r], 1)],
                                      row_sems.at[rsem]).start(priority=k)
        return carry

    lax.fori_loop(0, groups, start, 0, unroll=DMA_UNROLL // SUBLANES)

    @pl.when(i > 0)
    def _():
        drain(1 - rsem)

    @pl.when(i == last)
    def _():
        drain(rsem)


def _dispatch(dest_flat, fill_ranges, hn, n_slots, tm=512):
    n, d = hn.shape
    tm = min(tm, n)
    grid_spec = pltpu.PrefetchScalarGridSpec(
        num_scalar_prefetch=2,
        grid=(n // tm,),
        in_specs=[pl.BlockSpec(memory_space=pl.ANY)],
        out_specs=pl.BlockSpec(memory_space=pl.ANY),
        scratch_shapes=[pltpu.VMEM((3, tm // SUBLANES, SUBLANES, d), hn.dtype), pltpu.VMEM((MOE_BLOCK, d), hn.dtype),
                        pltpu.SemaphoreType.DMA((3,)), pltpu.SemaphoreType.DMA((2,)),
                        pltpu.SemaphoreType.DMA(())],
    )
    return pl.pallas_call(
        functools.partial(_dispatch_kernel, tm=tm, n_tok=n, n_fill=fill_ranges.shape[0] // 2,
                          n_unrouted=n_slots - 2 * n),
        grid_spec=grid_spec,
        out_shape=jax.ShapeDtypeStruct((n_slots, d), hn.dtype),
        compiler_params=_cparams(("arbitrary",)),
        name="row_dispatch",
    )(dest_flat, fill_ranges, hn.reshape(n // SUBLANES, SUBLANES, d))


def _moe_kernel(be_ref, nused_ref, x_ref, w1_hbm, w3_hbm, w2_hbm, out_ref, w1_ref, w3_ref, w2_ref,
                stage_cols_ref, stage_rows_ref, sems, *, ff_chunks, load_chunk):
    i = pl.program_id(0)
    used = i < nused_ref[0]
    expert = be_ref[i]
    prev_expert = be_ref[jnp.maximum(i - 1, 0)]
    d_ff = w1_ref.shape[1]

    @pl.when(used & ((i == 0) | (expert != prev_expert)))
    def _():
        pieces = []
        for lo in range(0, d_ff, load_chunk):
            cols = pl.ds(lo, load_chunk)
            pieces.append((w1_hbm.at[expert, :, cols], stage_cols_ref, w1_ref.at[:, cols]))
            pieces.append((w3_hbm.at[expert, :, cols], stage_cols_ref, w3_ref.at[:, cols]))
        for lo in range(0, d_ff, load_chunk):
            pieces.append((w2_hbm.at[expert, pl.ds(lo, load_chunk), :], stage_rows_ref,
                           w2_ref.at[pl.ds(lo, load_chunk), :]))
        copies = [pltpu.make_async_copy(src, ring.at[k % 2], sems.at[k % 2])
                  for k, (src, ring, _) in enumerate(pieces)]
        copies[0].start()
        for k, (_, ring, dst) in enumerate(pieces):
            if k + 1 < len(pieces):
                copies[k + 1].start()
            copies[k].wait()
            dst[...] = ring[k % 2].astype(BF16)

    @pl.when(used)
    def _():
        x = x_ref[...].astype(BF16)
        for n_done, (lo, hi) in enumerate(ff_chunks):
            a = jnp.dot(x, w1_ref[:, lo:hi], preferred_element_type=F32)
            g = jnp.dot(x, w3_ref[:, lo:hi], preferred_element_type=F32)
            y = jnp.dot((_silu(a) * g).astype(BF16), w2_ref[lo:hi, :], preferred_element_type=F32)
            if n_done == 0:
                out_ref[...] = y
            else:
                out_ref[...] += y

    @pl.when(jnp.logical_not(used))
    def _():
        out_ref[...] = jnp.zeros_like(out_ref)


def _moe_ffn(block_expert, n_used, slots, w1, w3, w2, load_chunk=512):
    n_slots, d = slots.shape
    n_blocks = n_slots // MOE_BLOCK
    d_ff = w1.shape[2]
    assert d_ff % load_chunk == 0
    hbm = pl.BlockSpec(memory_space=pl.ANY)
    grid_spec = pltpu.PrefetchScalarGridSpec(
        num_scalar_prefetch=2,
        grid=(n_blocks,),
        in_specs=[pl.BlockSpec((MOE_BLOCK, d), lambda i, be, nu: (jnp.minimum(i, nu[0] - 1), 0)), hbm, hbm, hbm],
        out_specs=pl.BlockSpec((MOE_BLOCK, d), lambda i, be, nu: (i, 0)),
        scratch_shapes=[pltpu.VMEM((d, d_ff), BF16), pltpu.VMEM((d, d_ff), BF16), pltpu.VMEM((d_ff, d), BF16),
                        pltpu.VMEM((2, d, load_chunk), w1.dtype), pltpu.VMEM((2, load_chunk, d), w2.dtype),
                        pltpu.SemaphoreType.DMA((2,))],
    )
    return pl.pallas_call(
        functools.partial(_moe_kernel, ff_chunks=_ff_chunks(d_ff), load_chunk=load_chunk),
        grid_spec=grid_spec,
        out_shape=jax.ShapeDtypeStruct((n_slots, d), F32),
        compiler_params=_cparams(("arbitrary",)),
        name="moe_ffn",
    )(block_expert, n_used, slots, w1, w3, w2)


def _combine_kernel(dest_ref, x_ref, gate_ref, nw_ref, y_ref, y_tiles_ref, out_ref, ybuf_ref, sems, *, tm, n_tok):
    i = pl.program_id(0)
    slot = i % 2
    d = x_ref.shape[1]

    def gather(tile, s):
        base = tile * tm

        def start(g, carry):
            for u in range(SUBLANES):
                r = base + g * SUBLANES + u
                for k in range(2):
                    pltpu.make_async_copy(y_ref.at[pl.ds(dest_ref[k * n_tok + r], 1)],
                                          ybuf_ref.at[s, k, g, pl.ds(u, 1)], sems.at[s]).start(priority=k)
            return carry

        lax.fori_loop(0, tm // SUBLANES, start, 0, unroll=DMA_UNROLL // SUBLANES)

    @pl.when(i == 0)
    def _():
        gather(0, 0)

    @pl.when(i + 1 < pl.num_programs(0))
    def _():
        gather(i + 1, 1 - slot)

    for k in range(2):
        pltpu.make_async_copy(y_tiles_ref.at[pl.ds(0, tm // SUBLANES)], ybuf_ref.at[slot, k], sems.at[slot]).wait()
    gate = gate_ref[...]
    y0 = ybuf_ref[slot, 0].reshape(tm, d)
    y1 = ybuf_ref[slot, 1].reshape(tm, d)
    out_ref[...] = _rms(x_ref[...] + (y0 * gate[:, 0:1] + y1 * gate[:, 1:2]), nw_ref[...])


def _combine(dest_flat, x2d, rgate, final_norm, yb, tm=512):
    n, d = x2d.shape
    tm = min(tm, n)
    grid_spec = pltpu.PrefetchScalarGridSpec(
        num_scalar_prefetch=1,
        grid=(n // tm,),
        in_specs=[pl.BlockSpec((tm, d), lambda i, dest: (i, 0)),
                  pl.BlockSpec((tm, GATE_LANES), lambda i, dest: (i, 0)),
                  pl.BlockSpec((1, d), lambda i, dest: (0, 0)),
                  pl.BlockSpec(memory_space=pl.ANY), pl.BlockSpec(memory_space=pl.ANY)],
        out_specs=pl.BlockSpec((tm, d), lambda i, dest: (i, 0)),
        scratch_shapes=[pltpu.VMEM((2, 2, tm // SUBLANES, SUBLANES, d), F32), pltpu.SemaphoreType.DMA((2,))],
    )
    return pl.pallas_call(
        functools.partial(_combine_kernel, tm=tm, n_tok=n),
        grid_spec=grid_spec,
        out_shape=jax.ShapeDtypeStruct((n, d), F32),
        compiler_params=_cparams(("arbitrary",)),
        name="combine_norm",
    )(dest_flat, x2d, rgate, final_norm.reshape(1, d), yb, yb.reshape(-1, SUBLANES, d))


def _split_in_proj(w_in, lo, hi):
    main = jnp.concatenate([w_in[:, :lo], w_in[:, hi:]], axis=1)
    gate = jnp.pad(w_in[:, lo:hi], ((0, 0), (0, GATE_LANES - (hi - lo))))
    return jnp.concatenate([main, gate], axis=1).astype(BF16), main.shape[1]


def kernel(x, mix_norm_0, w_in_0, conv_w_0, a_log_0, dt_bias_0, gdn_norm_0, ret_norm_0, w_out_0, ffn_norm_0,
           ffn_w1_0, ffn_w3_0, ffn_w2_0, mix_norm_1, w_in_1, gate_bias_1, mlstm_norm_1, w_out_1, ffn_norm_1,
           router_1, exp_w1_1, exp_w3_1, exp_w2_1, final_norm):
    b, t, d = x.shape
    n = b * t
    x2d = x.reshape(n, d)

    gdn_qkv = N_HEADS * (2 * HEAD_DK + GDN_DV)
    w0, n_main0 = _split_in_proj(w_in_0, gdn_qkv, gdn_qkv + 2 * N_HEADS)
    proj0, gates0 = _norm_proj(x2d, mix_norm_0, w0, n_main0, t, conv_w=conv_w_0)
    o_gdn = _gdn(proj0, gates0, _gates_transposed(gates0, b, t, CHUNK), a_log_0, dt_bias_0, gdn_norm_0, b, t)
    o_ret = _retention(proj0, ret_norm_0, b, t)
    x2 = _outproj_ffn(x2d, o_gdn.reshape(n, -1), o_ret.reshape(n, -1), w_out_0.astype(BF16), ffn_norm_0,
                      ffn_w1_0.astype(BF16), ffn_w3_0.astype(BF16), ffn_w2_0.astype(BF16))

    n_main1 = w_in_1.shape[1] - 2 * N_HEADS
    w1, _ = _split_in_proj(w_in_1, n_main1, n_main1 + 2 * N_HEADS)
    proj1, gates1 = _norm_proj(x2, mix_norm_1, w1, n_main1, t)
    h_mix = _mlstm(proj1, gates1, _gates_transposed(gates1, b, t, MLSTM_CHUNK), gate_bias_1, mlstm_norm_1, b, t)
    x3, hn, ridx, rgate = _outproj_router(x2, h_mix.reshape(n, -1), w_out_1.astype(BF16), ffn_norm_1, router_1)

    rank, counts = _expert_ranks(ridx)
    counts = counts[0, :N_EXPERTS]
    padded = (counts + MOE_BLOCK - 1) // MOE_BLOCK * MOE_BLOCK
    pad_end = jnp.cumsum(padded)
    pad_start = pad_end - padded
    pair_expert = rank[2:4].reshape(-1)
    pair_start = jnp.sum(jnp.where(pair_expert[:, None] == jnp.arange(N_EXPERTS, dtype=jnp.int32)[None, :],
                                   pad_start[None, :], 0), axis=1)
    dest_flat = pair_start + rank[0:2].reshape(-1)
    n_blocks = -(-(2 * n) // MOE_BLOCK) + N_EXPERTS
    block_row0 = jnp.arange(n_blocks, dtype=jnp.int32) * MOE_BLOCK
    block_expert = jnp.minimum(jnp.sum((pad_end[None, :] <= block_row0[:, None]).astype(jnp.int32), axis=1),
                               N_EXPERTS - 1)
    n_used = (pad_end[-1:] // MOE_BLOCK).astype(jnp.int32)

    n_slots = n_blocks * MOE_BLOCK
    fill_lo = jnp.concatenate([pad_start + counts, pad_end[-1:]])
    fill_hi = jnp.concatenate([pad_end, jnp.full((1,), n_slots, jnp.int32)])
    fill_ranges = jnp.stack([fill_lo, fill_hi], axis=1).reshape(-1).astype(jnp.int32)
    slots = _dispatch(dest_flat, fill_ranges, hn, n_slots)
    yb = _moe_ffn(block_expert, n_used, slots, exp_w1_1, exp_w3_1, exp_w2_1)
    out = _combine(dest_flat, x3, rgate, final_norm, yb)
    return out.reshape(b, t, d)
```

```python
import functools

import numpy as np
import jax
import jax.numpy as jnp
from jax import lax
from jax.experimental import pallas as pl
from jax.experimental.pallas import tpu as pltpu

F32 = jnp.float32
BF16 = jnp.bfloat16
HIGHEST = lax.Precision.HIGHEST

EPS = 1e-6
CHUNK = 128
RET_CHUNK = 128
MLSTM_CHUNK = 128
CONV_WIDTH = 4
N_HEADS = 8
HEAD_DK = 64
GDN_DV = 64
RET_DV = 64
MLSTM_DV = 128
ROPE_BASE = 10000.0
N_EXPERTS = 8
MOE_BLOCK = 512
GATE_LANES = 128
MXU_WIDTH = 256
VMEM_LIMIT = 52 * 1024 * 1024
SUBLANES = 8
DMA_UNROLL = 16


def _cparams(sem):
    return pltpu.CompilerParams(dimension_semantics=sem, vmem_limit_bytes=VMEM_LIMIT)


def _dot(a, b):
    return jnp.dot(a.astype(BF16), b.astype(BF16), preferred_element_type=F32)


def _dot_nt(a, b):
    return lax.dot_general(a.astype(BF16), b.astype(BF16), (((1,), (1,)), ((), ())),
                           preferred_element_type=F32)


def _dot_tn(a, b):
    return lax.dot_general(a.astype(BF16), b.astype(BF16), (((0,), (0,)), ((), ())),
                           preferred_element_type=F32)


def _dot_f32(a, b):
    return jnp.dot(a, b, precision=HIGHEST, preferred_element_type=F32)


def _rms(x, w):
    return x * lax.rsqrt(jnp.mean(x * x, axis=-1, keepdims=True) + EPS) * w


def _silu(x):
    return x * (1.0 / (1.0 + jnp.exp(-x)))


def _sigmoid(x):
    return 1.0 / (1.0 + jnp.exp(-x))


def _softplus(x):
    return jnp.maximum(x, 0.0) + jnp.log1p(jnp.exp(-jnp.abs(x)))


def _causal(n):
    r = lax.broadcasted_iota(jnp.int32, (n, n), 0)
    c = lax.broadcasted_iota(jnp.int32, (n, n), 1)
    return r >= c


def _norm_proj_kernel(x_ref, nw_ref, w_ref, conv_ref, main_ref, gate_ref, cbuf_ref, *, n_main, col_chunk,
                      conv_cols, tiles_per_seq):
    tm = x_ref.shape[0]
    h = _rms(x_ref[...], nw_ref[...]).astype(BF16)
    if conv_cols:
        @pl.when(pl.program_id(0) % tiles_per_seq == 0)
        def _():
            cbuf_ref[0:SUBLANES, :] = jnp.zeros((SUBLANES, conv_cols), F32)
    for c0 in range(0, n_main, col_chunk):
        cols = slice(c0, c0 + col_chunk)
        y = jnp.dot(h, w_ref[:, cols], preferred_element_type=F32)
        if c0 < conv_cols:
            cbuf_ref[SUBLANES:SUBLANES + tm, cols] = y
            pre = None
            for j in range(CONV_WIDTH):
                term = cbuf_ref[pl.ds(SUBLANES - (CONV_WIDTH - 1) + j, tm), cols] * conv_ref[j:j + 1, cols]
                pre = term if pre is None else pre + term
            cbuf_ref[0:SUBLANES, cols] = cbuf_ref[tm:tm + SUBLANES, cols]
            y = _silu(pre)
        main_ref[:, cols] = y.astype(BF16)
    gate_ref[...] = jnp.dot(h, w_ref[:, n_main:], preferred_element_type=F32)


def _norm_proj(x2d, norm_w, w_cat, n_main, seq_len, conv_w=None, tm=512, col_chunk=256):
    n, d = x2d.shape
    tm = min(tm, n, seq_len)
    n_cat = w_cat.shape[1]
    conv_cols = 0 if conv_w is None else conv_w.shape[1]
    assert seq_len % tm == 0 and conv_cols % col_chunk == 0
    if conv_w is None:
        conv_w = jnp.zeros((CONV_WIDTH, GATE_LANES), F32)
    return pl.pallas_call(
        functools.partial(_norm_proj_kernel, n_main=n_main, col_chunk=col_chunk, conv_cols=conv_cols,
                          tiles_per_seq=seq_len // tm),
        grid=(n // tm,),
        in_specs=[pl.BlockSpec((tm, d), lambda i: (i, 0)),
                  pl.BlockSpec((1, d), lambda i: (0, 0)),
                  pl.BlockSpec((d, n_cat), lambda i: (0, 0)),
                  pl.BlockSpec(conv_w.shape, lambda i: (0, 0))],
        out_specs=[pl.BlockSpec((tm, n_main), lambda i: (i, 0)),
                   pl.BlockSpec((tm, GATE_LANES), lambda i: (i, 0))],
        out_shape=[jax.ShapeDtypeStruct((n, n_main), BF16),
                   jax.ShapeDtypeStruct((n, GATE_LANES), F32)],
        scratch_shapes=[pltpu.VMEM((tm + SUBLANES, max(conv_cols, GATE_LANES)), F32)],
        compiler_params=_cparams(("arbitrary",)),
        name="norm_proj",
    )(x2d, norm_w.reshape(1, d), w_cat, conv_w.astype(F32))


def _gdn_kernel(qkv_ref, z_ref, gate_ref, gate_t_ref, arow_ref, dtrow_ref, acol_ref, dtcol_ref,
                norm_ref, ltri_ref, utri_ref, lvl_ref, ones_ref, out_ref, state_ref):
    n_b = qkv_ref.shape[0]
    n_qk = N_HEADS * HEAD_DK

    @pl.when(pl.program_id(1) == 0)
    def _():
        state_ref[...] = jnp.zeros_like(state_ref)

    causal = _causal(CHUNK)
    eye = (lax.broadcasted_iota(jnp.int32, (CHUNK, CHUNK), 0)
           == lax.broadcasted_iota(jnp.int32, (CHUNK, CHUNK), 1)).astype(F32)
    sl = [slice(h * HEAD_DK, (h + 1) * HEAD_DK) for h in range(N_HEADS)]

    q, k, v, bc, gcc, gcr = [], [], [], [], [], []
    for i in range(n_b):
        act = qkv_ref[i].astype(F32)
        q_all, k_all, v_all = act[:, :n_qk], act[:, n_qk:2 * n_qk], act[:, 2 * n_qk:]
        q_all = q_all * lax.rsqrt(_dot(q_all * q_all, ones_ref[...]) + EPS) * (HEAD_DK ** -0.5)
        k_all = k_all * lax.rsqrt(_dot(k_all * k_all, ones_ref[...]) + EPS)
        gl = gate_ref[i]
        beta_c = _sigmoid(gl)
        g_c = -jnp.exp(arow_ref[...]) * _softplus(gl + dtrow_ref[...])
        gc_c = _dot_f32(ltri_ref[...], g_c)
        g_r = -jnp.exp(acol_ref[...]) * _softplus(gate_t_ref[i] + dtcol_ref[...])
        gc_r = _dot_f32(g_r, utri_ref[...])
        for h in range(N_HEADS):
            q.append(q_all[:, sl[h]])
            k.append(k_all[:, sl[h]])
            v.append(v_all[:, sl[h]])
            bc.append(beta_c[:, h:h + 1])
            gcc.append(gc_c[:, 8 + h:9 + h])
            gcr.append(gc_r[8 + h:9 + h, :])

    chains = range(n_b * N_HEADS)
    decay = [jnp.exp(jnp.where(causal, gcc[c] - gcr[c], -jnp.inf)) for c in chains]
    kb = [k[c] * bc[c] for c in chains]
    scores = [_dot_nt(jnp.concatenate([kb[c], q[c]], axis=0), k[c]) for c in chains]
    lmat = [scores[c][:CHUNK] * decay[c] for c in chains]
    qk = [scores[c][CHUNK:] * decay[c] for c in chains]
    tinv = [eye - lmat[c] * lvl_ref[0].astype(F32) for c in chains]
    lmat_mxu = [lmat[c].astype(BF16) for c in chains]
    for lvl in range(1, lvl_ref.shape[0]):
        inner = [_dot(lmat_mxu[c] * lvl_ref[lvl], tinv[c]) for c in chains]
        tinv = [tinv[c] - _dot(tinv[c], inner[c]) for c in chains]
    egc = [jnp.exp(gcc[c]) for c in chains]
    uw = [_dot(tinv[c], jnp.concatenate([v[c] * bc[c], kb[c] * egc[c]], axis=1)) for c in chains]
    state = [state_ref[c] for c in chains]
    v_new = [uw[c][:, :GDN_DV] - _dot(uw[c][:, GDN_DV:], state[c]) for c in chains]
    o = [_dot(q[c] * egc[c], state[c]) + _dot(qk[c], v_new[c]) for c in chains]
    g_last = [gcc[c][CHUNK - 1:CHUNK, :] for c in chains]
    k_dec = [k[c] * jnp.exp(g_last[c] - gcc[c]) for c in chains]
    for c in chains:
        state_ref[c] = state[c] * jnp.exp(g_last[c]) + _dot_tn(k_dec[c], v_new[c])
    for i in range(n_b):
        o_all = jnp.concatenate(o[i * N_HEADS:(i + 1) * N_HEADS], axis=1)
        mean_sq = _dot(o_all * o_all, ones_ref[...]) * (1.0 / GDN_DV)
        out_ref[i] = (o_all * lax.rsqrt(mean_sq + EPS) * norm_ref[...]
                      * _silu(z_ref[i].astype(F32))).astype(out_ref.dtype)


def _level_masks():
    i = np.arange(CHUNK)[:, None]
    j = np.arange(CHUNK)[None, :]
    masks = []
    for lvl in range(CHUNK.bit_length() - 1):
        same_parent = (i >> (lvl + 1)) == (j >> (lvl + 1))
        masks.append((same_parent & (((i >> lvl) & 1) == 1) & (((j >> lvl) & 1) == 0)).astype(np.float32))
    return jnp.asarray(np.stack(masks)).astype(BF16)


def _tri_consts(ck):
    i = np.arange(ck)[:, None]
    j = np.arange(ck)[None, :]
    return jnp.asarray((j <= i).astype(np.float32)), jnp.asarray((i <= j).astype(np.float32))


def _lane_row(vals, offset):
    return jnp.zeros((1, GATE_LANES), F32).at[0, offset:offset + vals.shape[0]].set(vals.astype(F32))


def _sub_col(vals, offset):
    return jnp.zeros((16, 1), F32).at[offset:offset + vals.shape[0], 0].set(vals.astype(F32))


def _head_ones(n_heads, width):
    g = np.arange(n_heads * width) // width
    return jnp.asarray((g[:, None] == g[None, :]).astype(np.float32)).astype(BF16)


def _batch_per_step(b, want=2):
    return max(k for k in range(1, want + 1) if b % k == 0)


def _gdn(proj, gates, gates_t, a_log, dt_bias, gdn_norm, b, t):
    nc = t // CHUNK
    nb = _batch_per_step(b, want=2)
    n_qkv = N_HEADS * (2 * HEAD_DK + GDN_DV)
    n_z = N_HEADS * GDN_DV
    proj3 = proj.reshape(b, t, proj.shape[-1])
    ltri, utri = _tri_consts(CHUNK)
    const2 = lambda shape: pl.BlockSpec(shape, lambda i, c: (0,) * len(shape))
    return pl.pallas_call(
        _gdn_kernel,
        grid=(b // nb, nc),
        in_specs=[pl.BlockSpec((nb, CHUNK, n_qkv), lambda i, c: (i, c, 0)),
                  pl.BlockSpec((nb, CHUNK, n_z), lambda i, c: (i, c, n_qkv // n_z)),
                  pl.BlockSpec((nb, CHUNK, GATE_LANES), lambda i, c: (i, c, 0)),
                  pl.BlockSpec((nb, None, 16, CHUNK), lambda i, c: (i, c, 0, 0)),
                  const2((1, GATE_LANES)), const2((1, GATE_LANES)),
                  const2((16, 1)), const2((16, 1)), const2((1, n_z)),
                  const2((CHUNK, CHUNK)), const2((CHUNK, CHUNK)),
                  const2((CHUNK.bit_length() - 1, CHUNK, CHUNK)),
                  const2((n_z, n_z))],
        out_specs=pl.BlockSpec((nb, CHUNK, n_z), lambda i, c: (i, c, 0)),
        out_shape=jax.ShapeDtypeStruct((b, t, n_z), BF16),
        scratch_shapes=[pltpu.VMEM((nb * N_HEADS, HEAD_DK, GDN_DV), F32)],
        compiler_params=_cparams(("parallel", "arbitrary")),
        name="gdn_mixer",
    )(proj3, proj3, gates.reshape(b, t, GATE_LANES), gates_t,
      _lane_row(a_log, 8), _lane_row(dt_bias, 8), _sub_col(a_log, 8), _sub_col(dt_bias, 8),
      jnp.tile(gdn_norm.astype(F32), N_HEADS).reshape(1, n_z), ltri, utri, _level_masks(),
      _head_ones(N_HEADS, HEAD_DK))


def _ret_kernel(q_ref, k_ref, v_ref, g_ref, cos_ref, sin_ref, decay_ref, qs_ref, ks_ref, cd_ref, norm_ref,
                ones_ref, out_ref, state_ref):
    @pl.when(pl.program_id(1) == 0)
    def _():
        state_ref[...] = jnp.zeros_like(state_ref)

    width = N_HEADS * HEAD_DK
    half = HEAD_DK // 2
    first_half = (lax.broadcasted_iota(jnp.int32, (q_ref.shape[1], width), 1) % HEAD_DK) < half

    def rot(t_in):
        swapped = jnp.where(first_half, pltpu.roll(t_in, width - half, 1), pltpu.roll(t_in, half, 1))
        return t_in * cos_ref[...] + swapped * sin_ref[...]

    n_b = q_ref.shape[0]
    sl = [slice(h * HEAD_DK, (h + 1) * HEAD_DK) for h in range(N_HEADS)]
    q, k, v, q_in, k_st = [], [], [], [], []
    for i in range(n_b):
        q_all = rot(q_ref[i].astype(F32))
        k_all = rot(k_ref[i].astype(F32)) * (HEAD_DK ** -0.5)
        v_all = v_ref[i].astype(F32)
        qi_all = q_all * qs_ref[...]
        ks_all = k_all * ks_ref[...]
        for h in range(N_HEADS):
            q.append(q_all[:, sl[h]])
            k.append(k_all[:, sl[h]])
            v.append(v_all[:, sl[h]])
            q_in.append(qi_all[:, sl[h]])
            k_st.append(ks_all[:, sl[h]])
    chains = range(n_b * N_HEADS)
    state = [state_ref[c] for c in chains]
    scores = [_dot_nt(q[c], k[c]) * decay_ref[c % N_HEADS] for c in chains]
    inter = [_dot(q_in[c], state[c]) for c in chains]
    o = [_dot(scores[c], v[c]) + inter[c] for c in chains]
    for c in chains:
        state_ref[c] = state[c] * cd_ref[c % N_HEADS] + _dot_tn(k_st[c], v[c])
    inv_dv = 1.0 / RET_DV
    for i in range(n_b):
        o_all = jnp.concatenate(o[i * N_HEADS:(i + 1) * N_HEADS], axis=1)
        xc = o_all - _dot(o_all, ones_ref[...]) * inv_dv
        var = _dot(xc * xc, ones_ref[...]) * inv_dv
        out_ref[i] = (xc * lax.rsqrt(var + EPS) * norm_ref[...]
                      * _silu(g_ref[i].astype(F32))).astype(out_ref.dtype)


def _retention(proj, ret_norm, b, t):
    ck = RET_CHUNK
    nc = t // ck
    width = N_HEADS * HEAD_DK
    half = HEAD_DK // 2
    proj3 = proj.reshape(b, t, proj.shape[-1])
    pos = jnp.arange(t, dtype=F32)
    inv_freq = jnp.power(ROPE_BASE, -jnp.arange(half, dtype=F32) / half)
    ang = pos[:, None] * inv_freq[None, :]
    cos, sin = jnp.cos(ang), jnp.sin(ang)
    cos_t = jnp.tile(jnp.concatenate([cos, cos], axis=1), (1, N_HEADS))
    sin_t = jnp.tile(jnp.concatenate([-sin, sin], axis=1), (1, N_HEADS))
    log_gamma = jnp.log1p(-jnp.exp2(-5.0 - jnp.arange(N_HEADS, dtype=F32)))
    idx = jnp.arange(ck, dtype=F32)
    causal = jnp.tril(jnp.ones((ck, ck), bool))
    decay = jnp.exp(jnp.where(causal, (idx[:, None] - idx[None, :]) * log_gamma[:, None, None], -jnp.inf))
    qs = jnp.repeat(jnp.exp((idx + 1.0) * log_gamma[:, None]).T, HEAD_DK, axis=1)
    ks = jnp.repeat(jnp.exp((ck - 1.0 - idx) * log_gamma[:, None]).T, HEAD_DK, axis=1)
    cd = jnp.broadcast_to(jnp.exp(ck * log_gamma)[:, None, None], (N_HEADS, 1, RET_DV))
    base = (N_HEADS * (2 * HEAD_DK + GDN_DV) + N_HEADS * GDN_DV) // width
    nb = _batch_per_step(b, want=8)
    col = lambda k: pl.BlockSpec((nb, ck, width), lambda i, c: (i, c, base + k))
    const2 = lambda shape: pl.BlockSpec(shape, lambda i, c: (0,) * len(shape))
    return pl.pallas_call(
        _ret_kernel,
        grid=(b // nb, nc),
        in_specs=[col(0), col(1), col(2), col(3),
                  pl.BlockSpec((ck, width), lambda i, c: (c, 0)),
                  pl.BlockSpec((ck, width), lambda i, c: (c, 0)),
                  const2((N_HEADS, ck, ck)), const2((ck, width)), const2((ck, width)),
                  const2((N_HEADS, 1, RET_DV)), const2((1, width)), const2((width, width))],
        out_specs=pl.BlockSpec((nb, ck, width), lambda i, c: (i, c, 0)),
        out_shape=jax.ShapeDtypeStruct((b, t, width), BF16),
        scratch_shapes=[pltpu.VMEM((nb * N_HEADS, HEAD_DK, RET_DV), F32)],
        compiler_params=_cparams(("parallel", "arbitrary")),
        name="retention_mixer",
    )(proj3, proj3, proj3, proj3, cos_t, sin_t, decay, qs, ks, cd, ret_norm.reshape(1, width).astype(F32),
      _head_ones(N_HEADS, RET_DV))


def _mlstm_kernel(q_ref, k_ref, v_ref, o_ref, gate_ref, gate_t_ref, brow_ref, bcol_ref, norm_ref,
                  ltri_ref, utri_ref, out_ref, state_ref, m_ref):
    @pl.when(pl.program_id(1) == 0)
    def _():
        state_ref[...] = jnp.zeros_like(state_ref)
        m_ref[...] = jnp.zeros_like(m_ref)

    n_b, ck = q_ref.shape[0], q_ref.shape[1]
    causal = _causal(ck)
    ones_col = (lax.broadcasted_iota(jnp.int32, (ck, MLSTM_DV), 1) == 0).astype(F32)
    sl = [slice(h * HEAD_DK, (h + 1) * HEAD_DK) for h in range(N_HEADS)]
    sv = [slice(h * MLSTM_DV, (h + 1) * MLSTM_DV) for h in range(N_HEADS)]

    q, k, vx, bc, br, ig_c, ig_r = [], [], [], [], [], [], []
    for i in range(n_b):
        q_all = q_ref[i].astype(F32)
        k_all = k_ref[i].astype(F32) * (HEAD_DK ** -0.5)
        v_all = v_ref[i].astype(F32)
        gl = gate_ref[i] + brow_ref[...]
        bcum_c = _dot_f32(ltri_ref[...], -_softplus(-gl))
        gt = gate_t_ref[i] + bcol_ref[...]
        bcum_r = _dot_f32(-_softplus(-gt), utri_ref[...])
        for h in range(N_HEADS):
            q.append(q_all[:, sl[h]])
            k.append(k_all[:, sl[h]])
            vx.append(jnp.concatenate([v_all[:, sv[h]], ones_col], axis=1))
            bc.append(bcum_c[:, 8 + h:9 + h])
            br.append(bcum_r[8 + h:9 + h, :])
            ig_c.append(gl[:, h:h + 1])
            ig_r.append(gt[h:h + 1, :])

    chains = range(n_b * N_HEADS)
    log_w = [jnp.where(causal, bc[c] - br[c] + ig_r[c], -jnp.inf) for c in chains]
    m_intra = [jnp.max(log_w[c], axis=-1, keepdims=True) for c in chains]
    b_end = [bc[c][ck - 1:ck, :] for c in chains]
    lw_end = [b_end[c] - bc[c] + ig_c[c] for c in chains]
    m_end = [jnp.max(lw_end[c], axis=0, keepdims=True) for c in chains]
    m_s = [m_ref[c][:, 0:1] for c in chains]
    cx = [state_ref[c] for c in chains]
    m_t = [jnp.maximum(bc[c] + m_s[c], m_intra[c]) for c in chains]
    inter = [jnp.exp(bc[c] + m_s[c] - m_t[c]) for c in chains]
    qk = [_dot_nt(q[c], k[c]) for c in chains]
    qc = [_dot(q[c], cx[c]) for c in chains]
    s = [qk[c] * jnp.exp(log_w[c] - m_t[c]) for c in chains]
    numx = [inter[c] * qc[c] + _dot(s[c], vx[c]) for c in chains]
    m_new = [jnp.maximum(b_end[c] + m_s[c], m_end[c]) for c in chains]
    wk = [k[c] * jnp.exp(lw_end[c] - m_new[c]) for c in chains]
    for c in chains:
        state_ref[c] = jnp.exp(b_end[c] + m_s[c] - m_new[c]) * cx[c] + _dot_tn(wk[c], vx[c])
        m_ref[c] = jnp.broadcast_to(m_new[c], (1, GATE_LANES))
    hh = [numx[c][:, :MLSTM_DV] / jnp.maximum(jnp.abs(numx[c][:, MLSTM_DV:MLSTM_DV + 1]), jnp.exp(-m_t[c]))
          for c in chains]
    outs = [_rms(hh[c], norm_ref[:, sv[c % N_HEADS]]) for c in chains]
    for i in range(n_b):
        out_ref[i] = (jnp.concatenate(outs[i * N_HEADS:(i + 1) * N_HEADS], axis=1)
                      * _sigmoid(o_ref[i].astype(F32))).astype(out_ref.dtype)


def _mlstm(proj, gates, gates_t, gate_bias, mlstm_norm, b, t):
    ck = MLSTM_CHUNK
    nc = t // ck
    n_qk = N_HEADS * HEAD_DK
    n_v = N_HEADS * MLSTM_DV
    proj3 = proj.reshape(b, t, proj.shape[-1])
    ltri, utri = _tri_consts(ck)
    nb = _batch_per_step(b, want=8)
    const2 = lambda shape: pl.BlockSpec(shape, lambda i, c: (0,) * len(shape))
    return pl.pallas_call(
        _mlstm_kernel,
        grid=(b // nb, nc),
        in_specs=[pl.BlockSpec((nb, ck, n_qk), lambda i, c: (i, c, 0)),
                  pl.BlockSpec((nb, ck, n_qk), lambda i, c: (i, c, 1)),
                  pl.BlockSpec((nb, ck, n_v), lambda i, c: (i, c, 2 * n_qk // n_v)),
                  pl.BlockSpec((nb, ck, n_v), lambda i, c: (i, c, 2 * n_qk // n_v + 1)),
                  pl.BlockSpec((nb, ck, GATE_LANES), lambda i, c: (i, c, 0)),
                  pl.BlockSpec((nb, None, 16, ck), lambda i, c: (i, c, 0, 0)),
                  const2((1, GATE_LANES)), const2((16, 1)), const2((1, n_v)),
                  const2((ck, ck)), const2((ck, ck))],
        out_specs=pl.BlockSpec((nb, ck, n_v), lambda i, c: (i, c, 0)),
        out_shape=jax.ShapeDtypeStruct((b, t, n_v), BF16),
        scratch_shapes=[pltpu.VMEM((nb * N_HEADS, HEAD_DK, 2 * MLSTM_DV), F32),
                        pltpu.VMEM((nb * N_HEADS, 1, GATE_LANES), F32)],
        compiler_params=_cparams(("parallel", "arbitrary")),
        name="mlstm_mixer",
    )(proj3, proj3, proj3, proj3, gates.reshape(b, t, GATE_LANES), gates_t,
      _lane_row(gate_bias, 0), _sub_col(gate_bias, 0), mlstm_norm.reshape(1, n_v).astype(F32), ltri, utri)


def _gates_transposed(gates, b, t, ck):
    return jnp.transpose(gates[:, :16].reshape(b, t // ck, ck, 16), (0, 1, 3, 2))


def _outproj_ffn_kernel(x_ref, ma_ref, mb_ref, nw_ref, wo_hbm, w1_hbm, w3_hbm, w2_hbm, out_ref,
                        wo_ref, w1_ref, w3_ref, w2_ref, sems, *, ff_chunks):
    @pl.when(pl.program_id(0) == 0)
    def _():
        copies = [pltpu.make_async_copy(src, dst, sems.at[k]) for k, (src, dst) in enumerate(
            ((wo_hbm, wo_ref), (w1_hbm, w1_ref), (w3_hbm, w3_ref), (w2_hbm, w2_ref)))]
        for cp in copies:
            cp.start()
        for cp in copies:
            cp.wait()

    n_a = ma_ref.shape[1]
    x1 = (x_ref[...] + jnp.dot(ma_ref[...], wo_ref[:n_a, :], preferred_element_type=F32)
          + jnp.dot(mb_ref[...], wo_ref[n_a:, :], preferred_element_type=F32))
    h = _rms(x1, nw_ref[...]).astype(BF16)
    out_ref[...] = x1
    for lo, hi in ff_chunks:
        a = jnp.dot(h, w1_ref[:, lo:hi], preferred_element_type=F32)
        g = jnp.dot(h, w3_ref[:, lo:hi], preferred_element_type=F32)
        out_ref[...] += jnp.dot((_silu(a) * g).astype(BF16), w2_ref[lo:hi, :], preferred_element_type=F32)


def _ff_chunks(d_ff, max_chunk=1024):
    assert d_ff % MXU_WIDTH == 0 and max_chunk % MXU_WIDTH == 0
    bounds = list(range(0, d_ff, max_chunk)) + [d_ff]
    return tuple(zip(bounds[:-1], bounds[1:]))


def _outproj_ffn(x2d, mix_a, mix_b, w_out, ffn_norm, w1, w3, w2, tm=512):
    n, d = x2d.shape
    tm = min(tm, n)
    d_ff = w1.shape[1]
    n_a, n_b = mix_a.shape[1], mix_b.shape[1]
    row = lambda w: pl.BlockSpec((tm, w), lambda i: (i, 0))
    hbm = pl.BlockSpec(memory_space=pl.ANY)
    return pl.pallas_call(
        functools.partial(_outproj_ffn_kernel, ff_chunks=_ff_chunks(d_ff)),
        grid=(n // tm,),
        in_specs=[row(d), row(n_a), row(n_b), pl.BlockSpec((1, d), lambda i: (0, 0)), hbm, hbm, hbm, hbm],
        out_specs=row(d),
        out_shape=jax.ShapeDtypeStruct((n, d), F32),
        scratch_shapes=[pltpu.VMEM(w_out.shape, BF16), pltpu.VMEM(w1.shape, BF16), pltpu.VMEM(w3.shape, BF16),
                        pltpu.VMEM(w2.shape, BF16), pltpu.SemaphoreType.DMA((4,))],
        compiler_params=_cparams(("arbitrary",)),
        name="outproj_ffn",
    )(x2d, mix_a, mix_b, ffn_norm.reshape(1, d), w_out, w1, w3, w2)


def _outproj_router_kernel(x_ref, m_ref, wo_ref, nw_ref, r_ref, x3_ref, hn_ref, idx_ref, gate_ref):
    x3 = x_ref[...] + jnp.dot(m_ref[...], wo_ref[...], preferred_element_type=F32)
    x3_ref[...] = x3
    hn = _rms(x3, nw_ref[...])
    hn_ref[...] = hn
    hn_hi = hn.astype(BF16)
    hn_lo = (hn - hn_hi.astype(F32)).astype(BF16)
    r_hi, r_lo = r_ref[:, :GATE_LANES], r_ref[:, GATE_LANES:]
    logits = (jnp.dot(hn_hi, r_hi, preferred_element_type=F32)
              + (jnp.dot(hn_hi, r_lo, preferred_element_type=F32)
                 + jnp.dot(hn_lo, r_hi, preferred_element_type=F32)))
    lane = lax.broadcasted_iota(jnp.int32, logits.shape, 1)
    logits = jnp.where(lane < N_EXPERTS, logits, -jnp.inf)
    m0 = jnp.max(logits, axis=-1, keepdims=True)
    i0 = jnp.min(jnp.where(logits == m0, lane, GATE_LANES), axis=-1, keepdims=True)
    rest = jnp.where(lane == i0, -jnp.inf, logits)
    m1 = jnp.max(rest, axis=-1, keepdims=True)
    i1 = jnp.min(jnp.where(rest == m1, lane, GATE_LANES), axis=-1, keepdims=True)
    e1 = jnp.exp(m1 - m0)
    inv = 1.0 / (1.0 + e1)
    idx_ref[...] = jnp.where(lane == 0, i0, jnp.where(lane == 1, i1, 0))
    gate_ref[...] = jnp.where(lane == 0, inv, jnp.where(lane == 1, e1 * inv, 0.0))


def _outproj_router(x2d, mix, w_out, ffn_norm, router, tm=1024):
    n, d = x2d.shape
    tm = min(tm, n)
    n_m = mix.shape[1]
    router_pad = jnp.zeros((d, GATE_LANES), F32).at[:, :N_EXPERTS].set(router.astype(F32))
    router_hi = router_pad.astype(BF16)
    router_lo = (router_pad - router_hi.astype(F32)).astype(BF16)
    router_split = jnp.concatenate([router_hi, router_lo], axis=1)
    row = lambda w: pl.BlockSpec((tm, w), lambda i: (i, 0))
    return pl.pallas_call(
        _outproj_router_kernel,
        grid=(n // tm,),
        in_specs=[row(d), row(n_m),
                  pl.BlockSpec((n_m, d), lambda i: (0, 0)),
                  pl.BlockSpec((1, d), lambda i: (0, 0)),
                  pl.BlockSpec((d, 2 * GATE_LANES), lambda i: (0, 0))],
        out_specs=[row(d), row(d), row(GATE_LANES), row(GATE_LANES)],
        out_shape=[jax.ShapeDtypeStruct((n, d), F32), jax.ShapeDtypeStruct((n, d), F32),
                   jax.ShapeDtypeStruct((n, GATE_LANES), jnp.int32),
                   jax.ShapeDtypeStruct((n, GATE_LANES), F32)],
        compiler_params=_cparams(("parallel",)),
        name="outproj_router",
    )(x2d, mix, w_out, ffn_norm.reshape(1, d), router_split)


def _rank_kernel(idx_ref, tri_ref, rank_ref, count_ref, carry_ref):
    @pl.when(pl.program_id(0) == 0)
    def _():
        carry_ref[...] = jnp.zeros_like(carry_ref)

    idx = idx_ref[...]
    lane = lax.broadcasted_iota(jnp.int32, idx.shape, 1)
    e0, e1 = idx[:, 0:1], idx[:, 1:2]
    member = ((lane == e0) | (lane == e1)).astype(F32)
    before = _dot(tri_ref[...], member) + carry_ref[...]
    r0 = jnp.sum(jnp.where(lane == e0, before, 0.0), axis=-1, keepdims=True)
    r1 = jnp.sum(jnp.where(lane == e1, before, 0.0), axis=-1, keepdims=True)
    packed = jnp.where(lane == 0, r0, jnp.where(lane == 1, r1, jnp.where(
        lane == 2, e0.astype(F32), jnp.where(lane == 3, e1.astype(F32), 0.0))))
    rank_ref[...] = jnp.transpose(packed)[0:8, :].astype(jnp.int32)
    carry_ref[...] += jnp.sum(member, axis=0, keepdims=True)
    count_ref[...] = carry_ref[...].astype(jnp.int32)


def _expert_ranks(ridx, tm=512):
    n = ridx.shape[0]
    tm = min(tm, n)
    i = np.arange(tm)
    tri = jnp.asarray((i[None, :] < i[:, None]).astype(np.float32)).astype(BF16)
    return pl.pallas_call(
        _rank_kernel,
        grid=(n // tm,),
        in_specs=[pl.BlockSpec((tm, GATE_LANES), lambda i: (i, 0)),
                  pl.BlockSpec((tm, tm), lambda i: (0, 0))],
        out_specs=[pl.BlockSpec((8, tm), lambda i: (0, i)),
                   pl.BlockSpec((1, GATE_LANES), lambda i: (0, 0))],
        out_shape=[jax.ShapeDtypeStruct((8, n), jnp.int32),
                   jax.ShapeDtypeStruct((1, GATE_LANES), jnp.int32)],
        scratch_shapes=[pltpu.VMEM((1, GATE_LANES), F32)],
        compiler_params=_cparams(("arbitrary",)),
        name="expert_ranks",
    )(ridx, tri)


def _row_copy(src_ref, src_row, dst_ref, dst_row, sem):
    return pltpu.make_async_copy(src_ref.at[pl.ds(src_row, 1)], dst_ref.at[pl.ds(dst_row, 1)], sem)


def _dispatch_kernel(dest_ref, fill_ref, hn_ref, slots_ref, buf_ref, zero_ref, load_sems, row_sems, fill_sem,
                     *, tm, n_tok, n_fill, n_unrouted):
    i = pl.program_id(0)

    @pl.when(i == 0)
    def _():
        zero_ref[...] = jnp.zeros_like(zero_ref)
        for s in range(n_fill - 1):
            def fill_row(r, carry):
                _row_copy(zero_ref, 0, slots_ref, r, fill_sem).start()
                return carry
            lax.fori_loop(fill_ref[2 * s], fill_ref[2 * s + 1], fill_row, 0)

        def fill_block(blk, carry):
            pltpu.make_async_copy(zero_ref, slots_ref.at[pl.ds(blk * MOE_BLOCK, MOE_BLOCK)], fill_sem).start()
            return carry
        lax.fori_loop(fill_ref[2 * n_fill - 2] // MOE_BLOCK, fill_ref[2 * n_fill - 1] // MOE_BLOCK, fill_block, 0)
        for _ in range(n_unrouted // MOE_BLOCK):
            pltpu.make_async_copy(zero_ref, slots_ref.at[pl.ds(0, MOE_BLOCK)], fill_sem).wait()

    last = pl.num_programs(0) - 1
    base = i * tm
    cur = i % 3
    rsem = i % 2

    groups = tm // SUBLANES

    def load(tile, b):
        return pltpu.make_async_copy(hn_ref.at[pl.ds(tile * groups, groups)], buf_ref.at[b], load_sems.at[b])

    def drain(s):
        for _ in range(2):
            pltpu.make_async_copy(hn_ref.at[pl.ds(0, groups)], buf_ref.at[0], row_sems.at[s]).wait()

    @pl.when(i == 0)
    def _():
        load(0, 0).start()

    @pl.when(i < last)
    def _():
        load(i + 1, (i + 1) % 3).start()

    load(i, cur).wait()

    def start(g, carry):
        for u in range(SUBLANES):
            r = base + g * SUBLANES + u
            for k in range(2):
                pltpu.make_async_copy(buf_ref.at[cur, g, pl.ds(u, 1)], slots_ref.at[pl.ds(dest_ref[k * n_tok + r], 1)],
                                      row_sems.at[rsem]).start(priority=k)
        return carry

    lax.fori_loop(0, groups, start, 0, unroll=DMA_UNROLL // SUBLANES)

    @pl.when(i > 0)
    def _():
        drain(1 - rsem)

    @pl.when(i == last)
    def _():
        drain(rsem)


def _dispatch(dest_flat, fill_ranges, hn, n_slots, tm=512):
    n, d = hn.shape
    tm = min(tm, n)
    grid_spec = pltpu.PrefetchScalarGridSpec(
        num_scalar_prefetch=2,
        grid=(n // tm,),
        in_specs=[pl.BlockSpec(memory_space=pl.ANY)],
        out_specs=pl.BlockSpec(memory_space=pl.ANY),
        scratch_shapes=[pltpu.VMEM((3, tm // SUBLANES, SUBLANES, d), hn.dtype), pltpu.VMEM((MOE_BLOCK, d), hn.dtype),
                        pltpu.SemaphoreType.DMA((3,)), pltpu.SemaphoreType.DMA((2,)),
                        pltpu.SemaphoreType.DMA(())],
    )
    return pl.pallas_call(
        functools.partial(_dispatch_kernel, tm=tm, n_tok=n, n_fill=fill_ranges.shape[0] // 2,
                          n_unrouted=n_slots - 2 * n),
        grid_spec=grid_spec,
        out_shape=jax.ShapeDtypeStruct((n_slots, d), hn.dtype),
        compiler_params=_cparams(("arbitrary",)),
        name="row_dispatch",
    )(dest_flat, fill_ranges, hn.reshape(n // SUBLANES, SUBLANES, d))


def _moe_kernel(be_ref, nused_ref, x_ref, w1_hbm, w3_hbm, w2_hbm, out_ref, w1_ref, w3_ref, w2_ref,
                stage_cols_ref, stage_rows_ref, sems, *, ff_chunks, load_chunk):
    i = pl.program_id(0)
    used = i < nused_ref[0]
    expert = be_ref[i]
    prev_expert = be_ref[jnp.maximum(i - 1, 0)]
    d_ff = w1_ref.shape[1]

    @pl.when(used & ((i == 0) | (expert != prev_expert)))
    def _():
        pieces = []
        for lo in range(0, d_ff, load_chunk):
            cols = pl.ds(lo, load_chunk)
            pieces.append((w1_hbm.at[expert, :, cols], stage_cols_ref, w1_ref.at[:, cols]))
            pieces.append((w3_hbm.at[expert, :, cols], stage_cols_ref, w3_ref.at[:, cols]))
        for lo in range(0, d_ff, load_chunk):
            pieces.append((w2_hbm.at[expert, pl.ds(lo, load_chunk), :], stage_rows_ref,
                           w2_ref.at[pl.ds(lo, load_chunk), :]))
        copies = [pltpu.make_async_copy(src, ring.at[k % 2], sems.at[k % 2])
                  for k, (src, ring, _) in enumerate(pieces)]
        copies[0].start()
        for k, (_, ring, dst) in enumerate(pieces):
            if k + 1 < len(pieces):
                copies[k + 1].start(priority=(k + 1) % 2)
            copies[k].wait()
            dst[...] = ring[k % 2].astype(BF16)

    @pl.when(used)
    def _():
        x = x_ref[...].astype(BF16)
        for n_done, (lo, hi) in enumerate(ff_chunks):
            a = jnp.dot(x, w1_ref[:, lo:hi], preferred_element_type=F32)
            g = jnp.dot(x, w3_ref[:, lo:hi], preferred_element_type=F32)
            y = jnp.dot((_silu(a) * g).astype(BF16), w2_ref[lo:hi, :], preferred_element_type=F32)
            if n_done == 0:
                out_ref[...] = y
            else:
                out_ref[...] += y

    @pl.when(jnp.logical_not(used))
    def _():
        out_ref[...] = jnp.zeros_like(out_ref)


def _moe_ffn(block_expert, n_used, slots, w1, w3, w2, load_chunk=512):
    n_slots, d = slots.shape
    n_blocks = n_slots // MOE_BLOCK
    d_ff = w1.shape[2]
    assert d_ff % load_chunk == 0
    hbm = pl.BlockSpec(memory_space=pl.ANY)
    grid_spec = pltpu.PrefetchScalarGridSpec(
        num_scalar_prefetch=2,
        grid=(n_blocks,),
        in_specs=[pl.BlockSpec((MOE_BLOCK, d), lambda i, be, nu: (jnp.minimum(i, nu[0] - 1), 0)), hbm, hbm, hbm],
        out_specs=pl.BlockSpec((MOE_BLOCK, d), lambda i, be, nu: (i, 0)),
        scratch_shapes=[pltpu.VMEM((d, d_ff), BF16), pltpu.VMEM((d, d_ff), BF16), pltpu.VMEM((d_ff, d), BF16),
                        pltpu.VMEM((2, d, load_chunk), w1.dtype), pltpu.VMEM((2, load_chunk, d), w2.dtype),
                        pltpu.SemaphoreType.DMA((2,))],
    )
    return pl.pallas_call(
        functools.partial(_moe_kernel, ff_chunks=_ff_chunks(d_ff), load_chunk=load_chunk),
        grid_spec=grid_spec,
        out_shape=jax.ShapeDtypeStruct((n_slots, d), F32),
        compiler_params=_cparams(("arbitrary",)),
        name="moe_ffn",
    )(block_expert, n_used, slots, w1, w3, w2)


def _combine_kernel(dest_ref, x_ref, gate_ref, nw_ref, y_ref, y_tiles_ref, out_ref, ybuf_ref, sems, *, tm, n_tok):
    i = pl.program_id(0)
    slot = i % 2
    d = x_ref.shape[1]

    def gather(tile, s):
        base = tile * tm

        def start(g, carry):
            for u in range(SUBLANES):
                r = base + g * SUBLANES + u
                for k in range(2):
                    pltpu.make_async_copy(y_ref.at[pl.ds(dest_ref[k * n_tok + r], 1)],
                                          ybuf_ref.at[s, k, g, pl.ds(u, 1)], sems.at[s]).start(priority=k)
            return carry

        lax.fori_loop(0, tm // SUBLANES, start, 0, unroll=DMA_UNROLL // SUBLANES)

    @pl.when(i == 0)
    def _():
        gather(0, 0)

    @pl.when(i + 1 < pl.num_programs(0))
    def _():
        gather(i + 1, 1 - slot)

    for k in range(2):
        pltpu.make_async_copy(y_tiles_ref.at[pl.ds(0, tm // SUBLANES)], ybuf_ref.at[slot, k], sems.at[slot]).wait()
    gate = gate_ref[...]
    y0 = ybuf_ref[slot, 0].reshape(tm, d)
    y1 = ybuf_ref[slot, 1].reshape(tm, d)
    out_ref[...] = _rms(x_ref[...] + (y0 * gate[:, 0:1] + y1 * gate[:, 1:2]), nw_ref[...])


def _combine(dest_flat, x2d, rgate, final_norm, yb, tm=512):
    n, d = x2d.shape
    tm = min(tm, n)
    grid_spec = pltpu.PrefetchScalarGridSpec(
        num_scalar_prefetch=1,
        grid=(n // tm,),
        in_specs=[pl.BlockSpec((tm, d), lambda i, dest: (i, 0)),
                  pl.BlockSpec((tm, GATE_LANES), lambda i, dest: (i, 0)),
                  pl.BlockSpec((1, d), lambda i, dest: (0, 0)),
                  pl.BlockSpec(memory_space=pl.ANY), pl.BlockSpec(memory_space=pl.ANY)],
        out_specs=pl.BlockSpec((tm, d), lambda i, dest: (i, 0)),
        scratch_shapes=[pltpu.VMEM((2, 2, tm // SUBLANES, SUBLANES, d), F32), pltpu.SemaphoreType.DMA((2,))],
    )
    return pl.pallas_call(
        functools.partial(_combine_kernel, tm=tm, n_tok=n),
        grid_spec=grid_spec,
        out_shape=jax.ShapeDtypeStruct((n, d), F32),
        compiler_params=_cparams(("arbitrary",)),
        name="combine_norm",
    )(dest_flat, x2d, rgate, final_norm.reshape(1, d), yb, yb.reshape(-1, SUBLANES, d))


def _split_in_proj(w_in, lo, hi):
    main = jnp.concatenate([w_in[:, :lo], w_in[:, hi:]], axis=1)
    gate = jnp.pad(w_in[:, lo:hi], ((0, 0), (0, GATE_LANES - (hi - lo))))
    return jnp.concatenate([main, gate], axis=1).astype(BF16), main.shape[1]


def kernel(x, mix_norm_0, w_in_0, conv_w_0, a_log_0, dt_bias_0, gdn_norm_0, ret_norm_0, w_out_0, ffn_norm_0,
           ffn_w1_0, ffn_w3_0, ffn_w2_0, mix_norm_1, w_in_1, gate_bias_1, mlstm_norm_1, w_out_1, ffn_norm_1,
           router_1, exp_w1_1, exp_w3_1, exp_w2_1, final_norm):
    b, t, d = x.shape
    n = b * t
    x2d = x.reshape(n, d)

    gdn_qkv = N_HEADS * (2 * HEAD_DK + GDN_DV)
    w0, n_main0 = _split_in_proj(w_in_0, gdn_qkv, gdn_qkv + 2 * N_HEADS)
    proj0, gates0 = _norm_proj(x2d, mix_norm_0, w0, n_main0, t, conv_w=conv_w_0)
    o_gdn = _gdn(proj0, gates0, _gates_transposed(gates0, b, t, CHUNK), a_log_0, dt_bias_0, gdn_norm_0, b, t)
    o_ret = _retention(proj0, ret_norm_0, b, t)
    x2 = _outproj_ffn(x2d, o_gdn.reshape(n, -1), o_ret.reshape(n, -1), w_out_0.astype(BF16), ffn_norm_0,
                      ffn_w1_0.astype(BF16), ffn_w3_0.astype(BF16), ffn_w2_0.astype(BF16))

    n_main1 = w_in_1.shape[1] - 2 * N_HEADS
    w1, _ = _split_in_proj(w_in_1, n_main1, n_main1 + 2 * N_HEADS)
    proj1, gates1 = _norm_proj(x2, mix_norm_1, w1, n_main1, t)
    h_mix = _mlstm(proj1, gates1, _gates_transposed(gates1, b, t, MLSTM_CHUNK), gate_bias_1, mlstm_norm_1, b, t)
    x3, hn, ridx, rgate = _outproj_router(x2, h_mix.reshape(n, -1), w_out_1.astype(BF16), ffn_norm_1, router_1)

    rank, counts = _expert_ranks(ridx)
    counts = counts[0, :N_EXPERTS]
    padded = (counts + MOE_BLOCK - 1) // MOE_BLOCK * MOE_BLOCK
    pad_end = jnp.cumsum(padded)
    pad_start = pad_end - padded
    pair_expert = rank[2:4].reshape(-1)
    pair_start = jnp.sum(jnp.where(pair_expert[:, None] == jnp.arange(N_EXPERTS, dtype=jnp.int32)[None, :],
                                   pad_start[None, :], 0), axis=1)
    dest_flat = pair_start + rank[0:2].reshape(-1)
    n_blocks = -(-(2 * n) // MOE_BLOCK) + N_EXPERTS
    block_row0 = jnp.arange(n_blocks, dtype=jnp.int32) * MOE_BLOCK
    block_expert = jnp.minimum(jnp.sum((pad_end[None, :] <= block_row0[:, None]).astype(jnp.int32), axis=1),
                               N_EXPERTS - 1)
    n_used = (pad_end[-1:] // MOE_BLOCK).astype(jnp.int32)

    n_slots = n_blocks * MOE_BLOCK
    fill_lo = jnp.concatenate([pad_start + counts, pad_end[-1:]])
    fill_hi = jnp.concatenate([pad_end, jnp.full((1,), n_slots, jnp.int32)])
    fill_ranges = jnp.stack([fill_lo, fill_hi], axis=1).reshape(-1).astype(jnp.int32)
    slots = _dispatch(dest_flat, fill_ranges, hn, n_slots)
    yb = _moe_ffn(block_expert, n_used, slots, exp_w1_1, exp_w3_1, exp_w2_1)
    out = _combine(dest_flat, x3, rgate, final_norm, yb)
    return out.reshape(b, t, d)
```
